```python
import jax, jax.numpy as jnp
from jax import lax
import numpy as np

D_MODEL = 2048
BATCH = 8
SEQ = 2048
DEPTH = 1

CHUNK = 64
N_META = 16
POOL_W = D_MODEL // 2
POOL_WINDOWS = (2, 4, 8, 16)
POOL_GROUPS = len(POOL_WINDOWS)
POOL_GW = POOL_W // POOL_GROUPS
V_DIM = 128
N_HEADS = (D_MODEL - POOL_W) // V_DIM
QK_NOPE = 128
QK_ROPE = 64
Q_LORA = D_MODEL // 4
KV_LORA = D_MODEL // 8
MLA_W = N_HEADS * V_DIM
MIX_W = POOL_W + MLA_W
IN_W = POOL_W + Q_LORA + KV_LORA + QK_ROPE
ROPE_BASE = 10000.0
Q_BLOCK = 128
SOFTMAX_SCALE = (QK_NOPE + QK_ROPE) ** -0.5
N_EXPERTS = 32
TOP_K = 4
D_FF = D_MODEL
SWIGLU_LIMIT = 7.0
SWIGLU_ALPHA = 1.702
EXPERT_BLOCK = 256
EPS = 1e-5

kernel_name = "hymba_pool_mla_moe_streaming"


def rmsnorm(x, g):
    xf = x.astype(jnp.float32)
    y = xf * lax.rsqrt(jnp.mean(xf * xf, axis=-1, keepdims=True) + EPS)
    return (y * g.astype(jnp.float32)).astype(x.dtype)


def chunk_ids(pos):
    return jnp.where(pos < N_META, 0, 1 + (pos - N_META) // CHUNK)


def rope(x, cos, sin):
    half = x.shape[-1] // 2
    xf = x.astype(jnp.float32)
    x1, x2 = xf[..., :half], xf[..., half:]
    return jnp.concatenate([x1 * cos - x2 * sin, x2 * cos + x1 * sin], axis=-1).astype(x.dtype)


def pool_mixer(u, pool_w, pool_scale):
    B, L, _ = u.shape
    uf = u.astype(jnp.float32)
    t = jnp.arange(L)
    outs = []
    for g, w in enumerate(POOL_WINDOWS):
        ug = uf[..., g * POOL_GW:(g + 1) * POOL_GW]
        cs = jnp.concatenate([jnp.zeros((B, 1, POOL_GW), jnp.float32), jnp.cumsum(ug, axis=1)], axis=1)
        lo = jnp.maximum(t + 1 - w, 0)
        cnt = (t + 1 - lo).astype(jnp.float32)
        mean = (cs[:, 1:] - cs[:, lo]) / cnt[None, :, None]
        outs.append(mean - ug)
    d = jnp.stack(outs, axis=2).astype(u.dtype)
    y = jnp.einsum('blgc,gcd->blgd', d, pool_w).reshape(B, L, POOL_W)
    return y * pool_scale


def mla(q_c, kv_c, k_r, q_norm_g, w_uq, kv_norm_g, w_ukv):
    B, L, _ = q_c.shape
    q = (rmsnorm(q_c, q_norm_g) @ w_uq).reshape(B, L, N_HEADS, QK_NOPE + QK_ROPE)
    kv = (rmsnorm(kv_c, kv_norm_g) @ w_ukv).reshape(B, L, N_HEADS, QK_NOPE + V_DIM)
    pos = jnp.arange(L)
    inv_freq = 1.0 / (ROPE_BASE ** (jnp.arange(0, QK_ROPE, 2, dtype=jnp.float32) / QK_ROPE))
    ang = pos.astype(jnp.float32)[:, None] * inv_freq[None, :]
    cos, sin = jnp.cos(ang), jnp.sin(ang)
    q_rope = rope(q[..., QK_NOPE:], cos[None, :, None], sin[None, :, None])
    k_rope = rope(k_r, cos[None], sin[None])
    q = jnp.concatenate([q[..., :QK_NOPE], q_rope], axis=-1)
    k = jnp.concatenate([kv[..., :QK_NOPE], jnp.broadcast_to(k_rope[:, :, None, :], (B, L, N_HEADS, QK_ROPE))], axis=-1)
    v = kv[..., QK_NOPE:]
    n_blk = -(-L // Q_BLOCK)
    Lp = n_blk * Q_BLOCK
    padw = ((0, 0), (0, Lp - L), (0, 0), (0, 0))
    q, k, v = jnp.pad(q, padw), jnp.pad(k, padw), jnp.pad(v, padw)
    ppos = jnp.arange(Lp)
    q_chunk = chunk_ids(ppos)
    k_chunk = jnp.where(ppos < L, q_chunk, jnp.iinfo(jnp.int32).max)
    qb = q.reshape(B, n_blk, Q_BLOCK, N_HEADS, QK_NOPE + QK_ROPE).transpose(1, 0, 2, 3, 4)
    qcb = q_chunk.reshape(n_blk, Q_BLOCK)

    def attend(args):
        qblk, qc = args
        s = jnp.einsum('bqhd,bkhd->bhqk', qblk, k).astype(jnp.float32) * SOFTMAX_SCALE
        mask = k_chunk[None, :] <= qc[:, None]
        s = jnp.where(mask[None, None], s, -jnp.inf)
        p = jax.nn.softmax(s, axis=-1).astype(v.dtype)
        return jnp.einsum('bhqk,bkhd->bqhd', p, v)

    o = lax.map(attend, (qb, qcb))
    o = o.transpose(1, 0, 2, 3, 4).reshape(B, Lp, MLA_W)
    return o[:, :L]


def moe(h, w_router, b_router, w1, b1, w2, b2):
    B, L, D = h.shape
    xf = h.reshape(-1, D)
    T = xf.shape[0]
    logits = (xf @ w_router + b_router).astype(jnp.float32)
    top_val, top_idx = lax.top_k(logits, TOP_K)
    gates = jax.nn.softmax(top_val, axis=-1).astype(h.dtype)
    n_assign = T * TOP_K
    e_flat = top_idx.reshape(-1)
    g_flat = gates.reshape(-1)
    tok_flat = jnp.arange(n_assign, dtype=jnp.int32) // TOP_K
    order = jnp.argsort(e_flat)
    sorted_e = e_flat[order]
    counts = jnp.bincount(e_flat, length=N_EXPERTS)
    starts = jnp.cumsum(counts) - counts
    padded = (counts + EXPERT_BLOCK - 1) // EXPERT_BLOCK * EXPERT_BLOCK
    pad_ends = jnp.cumsum(padded)
    pad_starts = pad_ends - padded
    dest = pad_starts[sorted_e] + (jnp.arange(n_assign) - starts[sorted_e])
    n_blocks = -(-(n_assign + N_EXPERTS * (EXPERT_BLOCK - 1)) // EXPERT_BLOCK)
    P = n_blocks * EXPERT_BLOCK
    row_tok = jnp.full((P,), T, jnp.int32).at[dest].set(tok_flat[order])
    row_gate = jnp.zeros((P,), h.dtype).at[dest].set(g_flat[order])
    block_e = jnp.minimum(jnp.searchsorted(pad_ends, jnp.arange(n_blocks) * EXPERT_BLOCK, side='right'), N_EXPERTS - 1)
    x_pad = jnp.concatenate([xf, jnp.zeros((1, D), xf.dtype)], axis=0)

    def expert_block(args):
        tok, gate, e = args
        xb = x_pad[tok]
        hc = xb @ w1[e] + b1[e]
        glu = jnp.minimum(hc[:, ::2], SWIGLU_LIMIT)
        lin = jnp.clip(hc[:, 1::2], -SWIGLU_LIMIT, SWIGLU_LIMIT)
        a = glu * jax.nn.sigmoid(SWIGLU_ALPHA * glu) * (lin + 1)
        y = a @ w2[e] + b2[e]
        return y * gate[:, None]

    ys = lax.map(expert_block, (row_tok.reshape(n_blocks, EXPERT_BLOCK), row_gate.reshape(n_blocks, EXPERT_BLOCK), block_e))
    out = jnp.zeros((T + 1, D), h.dtype).at[row_tok].add(ys.reshape(P, D))[:T]
    return out.reshape(B, L, D)


def setup_inputs(seed: int = 0) -> dict:
    key = jax.random.key(seed)
    ks = jax.random.split(key, 20)

    def nrm(k, shape, scale):
        return jax.random.normal(k, shape, jnp.float32) * scale

    def gain(k, shape):
        return 1.0 + 0.05 * jax.random.normal(k, shape, jnp.float32)

    return {
        "x": nrm(ks[0], (BATCH, SEQ, D_MODEL), 1.0),
        "meta_tokens": nrm(ks[1], (N_META, D_MODEL), 1.0),
        "attn_norm_g": gain(ks[2], (DEPTH, D_MODEL)),
        "w_in": nrm(ks[3], (DEPTH, D_MODEL, IN_W), D_MODEL ** -0.5),
        "q_norm_g": gain(ks[4], (DEPTH, Q_LORA)),
        "w_uq": nrm(ks[5], (DEPTH, Q_LORA, N_HEADS * (QK_NOPE + QK_ROPE)), Q_LORA ** -0.5),
        "kv_norm_g": gain(ks[6], (DEPTH, KV_LORA)),
        "w_ukv": nrm(ks[7], (DEPTH, KV_LORA, N_HEADS * (QK_NOPE + V_DIM)), KV_LORA ** -0.5),
        "pool_w": nrm(ks[8], (DEPTH, POOL_GROUPS, POOL_GW, POOL_GW), POOL_GW ** -0.5),
        "pool_scale": gain(ks[9], (DEPTH, POOL_W)),
        "w_o": nrm(ks[10], (DEPTH, MIX_W, D_MODEL), MIX_W ** -0.5),
        "ffn_norm_g": gain(ks[11], (DEPTH, D_MODEL)),
        "w_router": nrm(ks[12], (DEPTH, D_MODEL, N_EXPERTS), D_MODEL ** -0.5),
        "b_router": nrm(ks[13], (DEPTH, N_EXPERTS), 0.01),
        "w1": nrm(ks[14], (DEPTH, N_EXPERTS, D_MODEL, 2 * D_FF), D_MODEL ** -0.5),
        "b1": nrm(ks[15], (DEPTH, N_EXPERTS, 2 * D_FF), 0.01),
        "w2": nrm(ks[16], (DEPTH, N_EXPERTS, D_FF, D_MODEL), D_FF ** -0.5),
        "b2": nrm(ks[17], (DEPTH, N_EXPERTS, D_MODEL), 0.01),
        "final_norm_g": gain(ks[18], (D_MODEL,)),
    }


def reference(x, meta_tokens, attn_norm_g, w_in, q_norm_g, w_uq, kv_norm_g, w_ukv, pool_w, pool_scale, w_o, ffn_norm_g, w_router, b_router, w1, b1, w2, b2, final_norm_g):
    B = x.shape[0]
    meta = jnp.broadcast_to(meta_tokens[None].astype(x.dtype), (B, N_META, D_MODEL))
    h = jnp.concatenate([meta, x], axis=1)
    cuts = [POOL_W, POOL_W + Q_LORA, POOL_W + Q_LORA + KV_LORA]
    for layer in range(DEPTH):
        hn = rmsnorm(h, attn_norm_g[layer])
        proj = hn @ w_in[layer]
        pool_in, q_c, kv_c, k_r = jnp.split(proj, cuts, axis=-1)
        y_pool = pool_mixer(pool_in, pool_w[layer], pool_scale[layer])
        y_mla = mla(q_c, kv_c, k_r, q_norm_g[layer], w_uq[layer], kv_norm_g[layer], w_ukv[layer])
        h = h + jnp.concatenate([y_pool, y_mla], axis=-1) @ w_o[layer]
        h = h + moe(rmsnorm(h, ffn_norm_g[layer]), w_router[layer], b_router[layer], w1[layer], b1[layer], w2[layer], b2[layer])
    h = rmsnorm(h, final_norm_g)
    return h[:, N_META:]
```

```python
import functools

import jax
import jax.numpy as jnp
from jax import lax
from jax.experimental import pallas as pl
from jax.experimental.pallas import tpu as pltpu

F32 = jnp.float32
BF16 = jnp.bfloat16
I32 = jnp.int32

CHUNK = 64
POOL_WINDOWS = (2, 4, 8, 16)
V_DIM = 128
QK_NOPE = 128
QK_ROPE = 64
ROPE_BASE = 10000.0
TOP_K = 4
SWIGLU_LIMIT = 7.0
SWIGLU_ALPHA = 1.702
EPS = 1e-5
EXPERT_BLOCK = 256

LANES = 128
HEAD_PAD = 2 * LANES
VMEM_LIMIT_BYTES = 56 * 1024 * 1024

HALO = 16
NEG_BIG = -1e30


def _rms(x, g):
    ms = jnp.mean(x * x, axis=-1, keepdims=True)
    return x * lax.rsqrt(ms + EPS) * g


def _dot(a, b):
    return jnp.dot(a, b, preferred_element_type=F32)


def _dot_nt(a, b):
    return lax.dot_general(a, b, (((1,), (1,)), ((), ())), preferred_element_type=F32)


def _const_spec(shape):
    nd = len(shape)
    return pl.BlockSpec(shape, lambda *_: (0,) * nd)


def _store_row_tiles(ref, x):
    n, d = x.shape
    nt = d // LANES
    for c in range(nt):
        ref[pl.ds(c, n, stride=nt), :] = x[:, c * LANES:(c + 1) * LANES]


def _load_row_tiles(ref, n, nt):
    return [ref[pl.ds(c, n, stride=nt), :] for c in range(nt)]


def _front_kernel(x_ref, mpool_ref, cs_ref, sn_ref, ag_ref, win_ref, pw_ref, ps_ref, qg_ref, wq_ref,
                  wqr_ref, kg_ref, wk_ref, wv_ref,
                  ypool_ref, q_ref, k_ref, v_ref, ptail_ref, ext_ref, *, ts, pool_w, q_lora, kv_lora,
                  n_heads):
    st = pl.program_id(1)
    hn = _rms(x_ref[0], ag_ref[...]).astype(BF16)
    proj = _dot(hn, win_ref[...])
    pool_in = proj[:, :pool_w]

    @pl.when(st == 0)
    def _():
        ext_ref[0:HALO, :] = mpool_ref[...]

    ext_ref[HALO:HALO + ts, :] = pool_in
    gw = pool_w // len(POOL_WINDOWS)
    for g, w in enumerate(POOL_WINDOWS):
        c0 = g * gw
        u = pool_in[:, c0:c0 + gw]
        s = u
        for k in range(1, w):
            s = s + ext_ref[HALO - k:HALO - k + ts, c0:c0 + gw]
        d = (s * (1.0 / w) - u).astype(BF16)
        y = _dot(d, pw_ref[g]) * ps_ref[:, c0:c0 + gw]
        ypool_ref[0, :, c0:c0 + gw] = y.astype(BF16)
    tail = pool_in[ts - HALO:ts, :]
    ext_ref[0:HALO, :] = tail
    ptail_ref[0, 0] = tail

    o = pool_w
    q_c = proj[:, o:o + q_lora]
    o += q_lora
    kv_c = proj[:, o:o + kv_lora]
    o += kv_lora
    kr = proj[:, o:o + LANES]
    kr_rot = proj[:, o + LANES:o + 2 * LANES]
    cs = cs_ref[...]
    sn = sn_ref[...]
    krope = (kr * cs + kr_rot * sn).astype(BF16)
    qn = _rms(q_c, qg_ref[...]).astype(BF16)
    qm = _dot(qn, wq_ref[...])
    qr = _dot(qn, wqr_ref[...])
    kvn = _rms(kv_c, kg_ref[...]).astype(BF16)
    kn = _dot(kvn, wk_ref[...])
    v_ref[0] = _dot(kvn, wv_ref[...]).astype(BF16)
    for h in range(n_heads):
        a = h * HEAD_PAD
        b = h * LANES
        q_ref[0, :, a:a + LANES] = qm[:, a:a + LANES].astype(BF16)
        q_ref[0, :, a + LANES:a + HEAD_PAD] = (
            qm[:, a + LANES:a + HEAD_PAD] * cs + qr[:, b:b + LANES] * sn).astype(BF16)
        k_ref[0, :, a:a + LANES] = kn[:, b:b + LANES].astype(BF16)
        k_ref[0, :, a + LANES:a + HEAD_PAD] = krope


def _front(x3, mpool, cs, sn, wts, *, ts):
    nb, s_len, d = x3.shape
    (ag, win, pw, ps, qg, wq, wqr, kg, wk, wv) = wts
    pool_w = ps.shape[1]
    q_lora = qg.shape[1]
    kv_lora = kg.shape[1]
    n_heads = wk.shape[1] // LANES
    n_st = s_len // ts
    kern = functools.partial(_front_kernel, ts=ts, pool_w=pool_w, q_lora=q_lora, kv_lora=kv_lora,
                             n_heads=n_heads)
    row = lambda b, s: (b, s, 0)
    in_specs = [
        pl.BlockSpec((1, ts, d), row),
        _const_spec(mpool.shape),
        pl.BlockSpec((ts, LANES), lambda b, s: (s, 0)),
        pl.BlockSpec((ts, LANES), lambda b, s: (s, 0)),
    ] + [_const_spec(w.shape) for w in wts]
    out_shape = (
        jax.ShapeDtypeStruct((nb, s_len, pool_w), BF16),
        jax.ShapeDtypeStruct((nb, s_len, n_heads * HEAD_PAD), BF16),
        jax.ShapeDtypeStruct((nb, s_len, n_heads * HEAD_PAD), BF16),
        jax.ShapeDtypeStruct((nb, s_len, n_heads * V_DIM), BF16),
        jax.ShapeDtypeStruct((nb, n_st, HALO, pool_w), F32),
    )
    out_specs = (
        pl.BlockSpec((1, ts, pool_w), row),
        pl.BlockSpec((1, ts, n_heads * HEAD_PAD), row),
        pl.BlockSpec((1, ts, n_heads * HEAD_PAD), row),
        pl.BlockSpec((1, ts, n_heads * V_DIM), row),
        pl.BlockSpec((1, 1, HALO, pool_w), lambda b, s: (b, s, 0, 0)),
    )
    return pl.pallas_call(
        kern, grid=(nb, n_st), in_specs=in_specs, out_specs=out_specs, out_shape=out_shape,
        scratch_shapes=[pltpu.VMEM((HALO + ts, pool_w), F32)],
        compiler_params=pltpu.CompilerParams(dimension_semantics=("arbitrary", "arbitrary"),
                                             vmem_limit_bytes=VMEM_LIMIT_BYTES),
        name="front",
    )(x3, mpool, cs, sn, *wts)


def _attn_kernel(q_ref, k_ref, v_ref, km_ref, vm_ref, o_ref, *, tq, n_meta, scale):
    qi = pl.program_id(2)
    q = q_ref[0]

    s = _dot_nt(q, km_ref[...]) * scale
    lane = lax.broadcasted_iota(I32, s.shape, 1)
    s = jnp.where(lane < n_meta, s, -jnp.inf)
    m = jnp.max(s, axis=1, keepdims=True)
    p = jnp.exp(s - m)
    l = jnp.sum(p, axis=1, keepdims=True)
    acc = _dot(p.astype(BF16), vm_ref[...])

    def step(j, carry, masked):
        m, l, acc = carry
        r0 = pl.multiple_of(j * tq, tq)
        kb = k_ref[0, pl.ds(r0, tq), :]
        vb = v_ref[0, pl.ds(r0, tq), :]
        s = _dot_nt(q, kb) * scale
        if masked:
            rc = lax.broadcasted_iota(I32, s.shape, 0) // CHUNK
            cc = lax.broadcasted_iota(I32, s.shape, 1) // CHUNK
            s = jnp.where(cc <= rc, s, -jnp.inf)
        m_new = jnp.maximum(m, jnp.max(s, axis=1, keepdims=True))
        alpha = jnp.exp(m - m_new)
        p = jnp.exp(s - m_new)
        l = alpha * l + jnp.sum(p, axis=1, keepdims=True)
        acc = alpha * acc + _dot(p.astype(BF16), vb)
        return m_new, l, acc

    carry = lax.fori_loop(0, qi, lambda j, c: step(j, c, False), (m, l, acc))
    m, l, acc = step(qi, carry, True)
    o_ref[0] = (acc / l).astype(BF16)


def _attn(q, k, v, km, vm, *, tq, n_meta):
    nb, s_len, hw = q.shape
    n_heads = hw // HEAD_PAD
    kern = functools.partial(_attn_kernel, tq=tq, n_meta=n_meta,
                             scale=float((QK_NOPE + QK_ROPE) ** -0.5))
    return pl.pallas_call(
        kern, grid=(nb, n_heads, s_len // tq),
        in_specs=[
            pl.BlockSpec((1, tq, HEAD_PAD), lambda b, h, i: (b, i, h)),
            pl.BlockSpec((1, s_len, HEAD_PAD), lambda b, h, i: (b, 0, h)),
            pl.BlockSpec((1, s_len, V_DIM), lambda b, h, i: (b, 0, h)),
            pl.BlockSpec((km.shape[0], HEAD_PAD), lambda b, h, i: (0, h)),
            pl.BlockSpec((vm.shape[0], V_DIM), lambda b, h, i: (0, h)),
        ],
        out_specs=pl.BlockSpec((1, tq, V_DIM), lambda b, h, i: (b, i, h)),
        out_shape=jax.ShapeDtypeStruct((nb, s_len, n_heads * V_DIM), BF16),
        compiler_params=pltpu.CompilerParams(
            dimension_semantics=("arbitrary", "arbitrary", "arbitrary"),
            vmem_limit_bytes=VMEM_LIMIT_BYTES),
        name="attn",
    )(q, k, v, km, vm)


def _mid_kernel(yp_ref, ym_ref, x_ref, woa_ref, wob_ref, fg_ref, wr_ref, br_ref,
                h1_ref, xnp_ref, tokmeta_ref, counts_ref, run_ref, *, tm):
    i = pl.program_id(0)

    @pl.when(i == 0)
    def _():
        run_ref[...] = jnp.zeros_like(run_ref)

    h1 = x_ref[...] + _dot(yp_ref[...], woa_ref[...]) + _dot(ym_ref[...], wob_ref[...])
    h1_ref[...] = h1
    xn = _rms(h1, fg_ref[...])
    _store_row_tiles(xnp_ref, xn)
    xb = xn.astype(BF16)

    logits = _dot(xb, wr_ref[...]) + br_ref[...]
    lane = lax.broadcasted_iota(I32, logits.shape, 1).astype(F32)
    work = logits
    idxs, vals = [], []
    for _ in range(TOP_K):
        mx = jnp.max(work, axis=1, keepdims=True)
        ix = jnp.min(jnp.where(work == mx, lane, float(LANES)), axis=1, keepdims=True)
        idxs.append(ix)
        vals.append(mx)
        work = jnp.where(lane == ix, -jnp.inf, work)
    es = [jnp.exp(vv - vals[0]) for vv in vals]
    den = es[0]
    for e in es[1:]:
        den = den + e
    hot = [jnp.where(lane == ix, 1.0, 0.0) for ix in idxs]
    cnt = hot[0]
    for hh in hot[1:]:
        cnt = cnt + hh
    rr = lax.broadcasted_iota(I32, (tm, tm), 0)
    cc = lax.broadcasted_iota(I32, (tm, tm), 1)
    ltri = jnp.where(rr > cc, 1.0, 0.0).astype(BF16)
    base = run_ref[0:1, :] + _dot(ltri, cnt.astype(BF16))
    out = jnp.zeros(logits.shape, F32)
    for k in range(TOP_K):
        rank = jnp.sum(hot[k] * base, axis=1, keepdims=True)
        out = jnp.where(lane == float(k), idxs[k], out)
        out = jnp.where(lane == float(TOP_K + k), rank, out)
        out = jnp.where(lane == float(2 * TOP_K + k), es[k] / den, out)
    tokmeta_ref[...] = out
    run = run_ref[...] + jnp.sum(cnt, axis=0, keepdims=True)
    run_ref[...] = run
    counts_ref[...] = run


def _mid(yp, ym, x2, woa, wob, fg, wr, br, *, tm):
    t, d = x2.shape
    pw = yp.shape[1]
    mw = ym.shape[1]
    nt = d // LANES
    kern = functools.partial(_mid_kernel, tm=tm)
    row = lambda i: (i, 0)
    return pl.pallas_call(
        kern, grid=(t // tm,),
        in_specs=[
            pl.BlockSpec((tm, pw), row), pl.BlockSpec((tm, mw), row), pl.BlockSpec((tm, d), row),
            _const_spec(woa.shape), _const_spec(wob.shape), _const_spec(fg.shape),
            _const_spec(wr.shape), _const_spec(br.shape),
        ],
        out_specs=(
            pl.BlockSpec((tm, d), row), pl.BlockSpec((tm * nt, LANES), row), pl.BlockSpec((tm, LANES), row),
            pl.BlockSpec((8, LANES), lambda i: (0, 0)),
        ),
        out_shape=(
            jax.ShapeDtypeStruct((t, d), F32), jax.ShapeDtypeStruct((t * nt, LANES), F32),
            jax.ShapeDtypeStruct((t, LANES), F32), jax.ShapeDtypeStruct((8, LANES), F32),
        ),
        scratch_shapes=[pltpu.VMEM((8, LANES), F32)],
        compiler_params=pltpu.CompilerParams(dimension_semantics=("arbitrary",),
                                             vmem_limit_bytes=VMEM_LIMIT_BYTES),
        name="mid",
    )(yp, ym, x2, woa, wob, fg, wr, br)


def _dispatch_kernel(dest_ref, xnp_ref, xs_in_ref, xs_ref, sem, *, td, nt):
    del xs_in_ref

    def row_copy(r, d):
        return pltpu.make_async_copy(xnp_ref.at[pl.ds(pl.multiple_of(r * nt, nt), nt)],
                                     xs_ref.at[pl.ds(pl.multiple_of(d * nt, nt), nt)], sem)

    def body(r, c):
        for k in range(TOP_K):
            row_copy(r, dest_ref[r * TOP_K + k]).start()
        return c

    lax.fori_loop(0, td, body, 0, unroll=8)
    for _ in range(TOP_K):
        pltpu.make_async_copy(xnp_ref, xs_ref.at[pl.ds(0, td * nt)], sem).wait()


def _dispatch(dest_flat, xnp, xs_zero, *, td, nt):
    t = xnp.shape[0] // nt
    kern = functools.partial(_dispatch_kernel, td=td, nt=nt)
    return pl.pallas_call(
        kern, grid=(t // td,),
        in_specs=[
            pl.BlockSpec((td * TOP_K,), lambda i: (i,), memory_space=pltpu.SMEM),
            pl.BlockSpec((td * nt, LANES), lambda i: (i, 0)),
            pl.BlockSpec(memory_space=pl.ANY),
        ],
        out_specs=pl.BlockSpec(memory_space=pl.ANY),
        out_shape=jax.ShapeDtypeStruct(xs_zero.shape, F32),
        scratch_shapes=[pltpu.SemaphoreType.DMA],
        input_output_aliases={2: 0},
        compiler_params=pltpu.CompilerParams(dimension_semantics=("arbitrary",),
                                             vmem_limit_bytes=VMEM_LIMIT_BYTES),
        name="dispatch",
    )(dest_flat, xnp, xs_zero)


def _expert_kernel(item_e_ref, item_row_ref, item_n_ref, item_valid_ref,
                   xs_ref, w1_ref, b1_ref, w2_ref, b2_ref, ys_ref,
                   xbuf, abuf, xstage, wperm, w2b, ystage, sem_x, sem_y, *, rb, j1, tw, nt):
    del item_e_ref, item_valid_ref, xs_ref
    w = pl.program_id(0)
    j = pl.program_id(1)
    nrows = item_n_ref[w]
    row0 = pl.multiple_of(item_row_ref[w], rb)
    nrb = nrows // rb
    d = w2_ref.shape[1]
    tf = w2_ref.shape[0]

    @pl.when(j == 0)
    def _load_rows():
        def body(i, c):
            r = pl.multiple_of(i * rb, rb)
            src = pl.multiple_of((row0 + r) * nt, rb * nt)
            cp = pltpu.make_async_copy(ys_ref.at[pl.ds(src, rb * nt)], xstage, sem_x)
            cp.start()
            cp.wait()
            for cc, piece in enumerate(_load_row_tiles(xstage, rb, nt)):
                xbuf[pl.ds(r, rb), cc * LANES:(cc + 1) * LANES] = piece.astype(BF16)
            return c

        lax.fori_loop(0, nrb, body, 0)

    @pl.when((j < j1) & (nrows > 0))
    def _phase1():
        for s in range(d // LANES):
            for g in range(tf // LANES):
                top = w2_ref[g * LANES:g * LANES + LANES // 2, s * LANES:(s + 1) * LANES]
                bot = w2_ref[g * LANES + LANES // 2:(g + 1) * LANES, s * LANES:(s + 1) * LANES]
                wperm.at[s][pl.ds(g * LANES, LANES // 2, stride=2), :] = top
                wperm.at[s][pl.ds(g * LANES + 1, LANES // 2, stride=2), :] = bot
        k0 = pl.multiple_of(j * tf, tf)
        for s in range(d // LANES):
            w2b[pl.ds(k0, tf), s * LANES:(s + 1) * LANES] = wperm[s].astype(BF16)

        lane = lax.broadcasted_iota(I32, (rb, LANES), 1)
        even = (lane & 1) == 0

        def body(i, c):
            r = pl.multiple_of(i * rb, rb)
            hc = _dot(xbuf[pl.ds(r, rb), :], w1_ref[...].astype(BF16)) + b1_ref[...]
            outs = []
            for q in range(tw // (2 * LANES)):
                c0 = hc[:, 2 * q * LANES:(2 * q + 1) * LANES]
                c1 = hc[:, (2 * q + 1) * LANES:(2 * q + 2) * LANES]
                glu = jnp.where(even, c0, pltpu.roll(c1, 1, 1))
                lin = jnp.where(even, pltpu.roll(c0, LANES - 1, 1), c1)
                glu = jnp.minimum(glu, SWIGLU_LIMIT)
                lin = jnp.clip(lin, -SWIGLU_LIMIT, SWIGLU_LIMIT)
                act = glu * (1.0 / (1.0 + jnp.exp(-SWIGLU_ALPHA * glu))) * (lin + 1.0)
                outs.append(act.astype(BF16))
            abuf[j, pl.ds(r, rb), :] = jnp.concatenate(outs, axis=1)
            return c

        lax.fori_loop(0, nrb, body, 0)

    @pl.when((j == j1) & (nrows > 0))
    def _phase2():
        def y_copy(slot, r):
            dst = pl.multiple_of((row0 + r) * nt, rb * nt)
            return pltpu.make_async_copy(ystage.at[slot], ys_ref.at[pl.ds(dst, rb * nt)], sem_y.at[slot])

        def body(i, c):
            r = pl.multiple_of(i * rb, rb)
            slot = i & 1
            a = jnp.concatenate([abuf[jj, pl.ds(r, rb), :] for jj in range(j1)], axis=1)
            y = _dot(a, w2b[...]) + b2_ref[...]

            @pl.when(i >= 2)
            def _():
                y_copy(slot, r).wait()

            _store_row_tiles(ystage.at[slot], y)
            y_copy(slot, r).start()
            return c

        lax.fori_loop(0, nrb, body, 0)

        @pl.when(nrb >= 2)
        def _():
            y_copy(nrb & 1, 0).wait()

        y_copy((nrb - 1) & 1, 0).wait()


def _experts(item_e, item_row, item_n, item_valid, xs, w1, b1, w2, b2, *, r_max, tw, nt):
    n_exp, d, f2 = w1.shape
    f_dim = w2.shape[1]
    j1 = f2 // tw
    tf = f_dim // j1
    assert tf == tw // 2 and tf % LANES == 0 and nt * LANES == d
    rb = EXPERT_BLOCK
    kern = functools.partial(_expert_kernel, rb=rb, j1=j1, tw=tw, nt=nt)

    def w_step(w, j, iv):
        return jnp.minimum(jnp.where(iv[w] == 1, j, j1), j1 - 1)

    def w1_map(w, j, ie, ir, inn, iv):
        return (ie[w], 0, w_step(w, j, iv))

    def w2_map(w, j, ie, ir, inn, iv):
        return (ie[w], w_step(w, j, iv), 0)

    def e_map(w, j, ie, ir, inn, iv):
        return (ie[w], 0, 0)

    grid_spec = pltpu.PrefetchScalarGridSpec(
        num_scalar_prefetch=4, grid=(item_e.shape[0], j1 + 1),
        in_specs=[
            pl.BlockSpec(memory_space=pl.ANY),
            pl.BlockSpec((None, d, tw), w1_map),
            pl.BlockSpec((None, 1, tw), w1_map),
            pl.BlockSpec((None, tf, d), w2_map),
            pl.BlockSpec((None, 1, d), e_map),
        ],
        out_specs=pl.BlockSpec(memory_space=pl.ANY),
        scratch_shapes=[
            pltpu.VMEM((r_max, d), BF16),
            pltpu.VMEM((j1, r_max, tf), BF16),
            pltpu.VMEM((rb * nt, LANES), F32),
            pltpu.VMEM((d // LANES, tf, LANES), F32),
            pltpu.VMEM((f_dim, d), BF16),
            pltpu.VMEM((2, rb * nt, LANES), F32),
            pltpu.SemaphoreType.DMA,
            pltpu.SemaphoreType.DMA((2,)),
        ],
    )
    return pl.pallas_call(
        kern, grid_spec=grid_spec,
        out_shape=jax.ShapeDtypeStruct(xs.shape, F32),
        input_output_aliases={4: 0},
        compiler_params=pltpu.CompilerParams(dimension_semantics=("arbitrary", "arbitrary"),
                                             vmem_limit_bytes=VMEM_LIMIT_BYTES),
        name="experts",
    )(item_e, item_row, item_n, item_valid, xs, w1, b1.reshape(n_exp, 1, f2), w2, b2.reshape(n_exp, 1, d))


def _combine_kernel(dest_ref, tokmeta_ref, h1_ref, fg_ref, ys_ref, o_ref, buf, sem, *, tc, nt):
    def body(r, c):
        for k in range(TOP_K):
            src = pl.multiple_of(dest_ref[r * TOP_K + k] * nt, nt)
            pltpu.make_async_copy(ys_ref.at[pl.ds(src, nt)],
                                  buf.at[k, pl.ds(pl.multiple_of(r * nt, nt), nt)], sem).start()
        return c

    lax.fori_loop(0, tc, body, 0, unroll=8)
    for k in range(TOP_K):
        pltpu.make_async_copy(ys_ref.at[pl.ds(0, tc * nt)], buf.at[k], sem).wait()
    tm = tokmeta_ref[...]
    cols = [h1_ref[:, c * LANES:(c + 1) * LANES] for c in range(nt)]
    for k in range(TOP_K):
        gate = tm[:, 2 * TOP_K + k:2 * TOP_K + k + 1]
        cols = [a + gate * b for a, b in zip(cols, _load_row_tiles(buf.at[k], tc, nt))]
    o_ref[...] = _rms(jnp.concatenate(cols, axis=1), fg_ref[...])


def _combine(dest_flat, tokmeta, h1, fg, ys, *, tc, nt):
    t, d = h1.shape
    kern = functools.partial(_combine_kernel, tc=tc, nt=nt)
    row = lambda i: (i, 0)
    return pl.pallas_call(
        kern, grid=(t // tc,),
        in_specs=[
            pl.BlockSpec((tc * TOP_K,), lambda i: (i,), memory_space=pltpu.SMEM),
            pl.BlockSpec((tc, LANES), row), pl.BlockSpec((tc, d), row), _const_spec(fg.shape),
            pl.BlockSpec(memory_space=pl.ANY),
        ],
        out_specs=pl.BlockSpec((tc, d), row),
        out_shape=jax.ShapeDtypeStruct((t, d), F32),
        scratch_shapes=[pltpu.VMEM((TOP_K, tc * nt, LANES), F32), pltpu.SemaphoreType.DMA],
        compiler_params=pltpu.CompilerParams(dimension_semantics=("arbitrary",),
                                             vmem_limit_bytes=VMEM_LIMIT_BYTES),
        name="combine",
    )(dest_flat, tokmeta, h1, fg, ys)


def _rot_cols(w):
    h = QK_ROPE // 2
    return jnp.concatenate([-w[..., h:], w[..., :h]], axis=-1)


def _tile_rows(n, cap):
    t = min(n, cap)
    assert n % t == 0, (n, cap)
    return t


def kernel(x, meta_tokens, attn_norm_g, w_in, q_norm_g, w_uq, kv_norm_g, w_ukv, pool_w, pool_scale, w_o,
           ffn_norm_g, w_router, b_router, w1, b1, w2, b2, final_norm_g):
    nb, s_len, d = x.shape
    n_meta = meta_tokens.shape[0]
    assert w_in.shape[0] == 1, "one layer"
    assert n_meta == HALO and max(POOL_WINDOWS) - 1 <= HALO
    pw = pool_scale.shape[1]
    q_lora = q_norm_g.shape[1]
    kv_lora = kv_norm_g.shape[1]
    n_heads = w_uq.shape[2] // (QK_NOPE + QK_ROPE)
    n_exp = w_router.shape[2]
    f_dim = w2.shape[2]
    t = nb * s_len
    assert s_len % CHUNK == 0 and n_exp <= LANES and pw // len(POOL_WINDOWS) % LANES == 0

    win = w_in[0]
    o = pw + q_lora + kv_lora
    w_kr = win[:, o:o + QK_ROPE]
    zc = jnp.zeros((d, LANES - QK_ROPE), F32)
    win_b = jnp.concatenate([win[:, :o], w_kr, zc, _rot_cols(w_kr), zc], axis=1).astype(BF16)
    wq3 = w_uq[0].reshape(q_lora, n_heads, QK_NOPE + QK_ROPE)
    zq = jnp.zeros((q_lora, n_heads, LANES - QK_ROPE), F32)
    wq_b = jnp.concatenate([wq3, zq], axis=2).reshape(q_lora, n_heads * HEAD_PAD).astype(BF16)
    wqr_b = jnp.concatenate([_rot_cols(wq3[:, :, QK_NOPE:]), zq], axis=2).reshape(
        q_lora, n_heads * LANES).astype(BF16)
    wkv3 = w_ukv[0].reshape(kv_lora, n_heads, QK_NOPE + V_DIM)
    wk_b = wkv3[:, :, :QK_NOPE].reshape(kv_lora, n_heads * QK_NOPE).astype(BF16)
    wv_b = wkv3[:, :, QK_NOPE:].reshape(kv_lora, n_heads * V_DIM).astype(BF16)
    front_w = (attn_norm_g, win_b, pool_w[0].astype(BF16), pool_scale, q_norm_g, wq_b, wqr_b, kv_norm_g,
               wk_b, wv_b)
    woa = w_o[0, :pw].astype(BF16)
    wob = w_o[0, pw:].astype(BF16)
    wr_b = jnp.pad(w_router[0], ((0, 0), (0, LANES - n_exp))).astype(BF16)
    br = jnp.pad(b_router, ((0, 0), (0, LANES - n_exp)), constant_values=NEG_BIG)

    pos = jnp.arange(n_meta + s_len, dtype=F32)
    inv_freq = 1.0 / (ROPE_BASE ** (jnp.arange(0, QK_ROPE, 2, dtype=F32) / QK_ROPE))
    ang = pos[:, None] * inv_freq[None, :]
    ones = jnp.ones((n_meta + s_len, LANES - QK_ROPE), F32)
    cs = jnp.concatenate([jnp.cos(ang), jnp.cos(ang), ones], axis=1)
    sn = jnp.concatenate([jnp.sin(ang), jnp.sin(ang), 0.0 * ones], axis=1)

    zero_halo = jnp.zeros((HALO, pw), F32)
    _, _, k_meta, v_meta, p_meta = _front(meta_tokens[None], zero_halo, cs[:n_meta], sn[:n_meta], front_w,
                                          ts=n_meta)
    ts = _tile_rows(s_len, 256)
    y_pool, q, k, v, _ = _front(x, p_meta[0, 0], cs[n_meta:], sn[n_meta:], front_w, ts=ts)

    km = jnp.pad(k_meta[0], ((0, LANES - n_meta), (0, 0)))
    vm = jnp.pad(v_meta[0], ((0, LANES - n_meta), (0, 0)))
    y_mla = _attn(q, k, v, km, vm, tq=_tile_rows(s_len, 256), n_meta=n_meta)

    tm = _tile_rows(t, 256)
    h1, xnp, tokmeta, counts = _mid(y_pool.reshape(t, pw), y_mla.reshape(t, -1), x.reshape(t, d), woa, wob,
                                    ffn_norm_g, wr_b, br, tm=tm)

    r_max = 10 * EXPERT_BLOCK
    cnt = counts[0, :n_exp].astype(I32)
    padded = (cnt + EXPERT_BLOCK - 1) // EXPERT_BLOCK * EXPERT_BLOCK
    pad_end = jnp.cumsum(padded)
    pad_start = pad_end - padded
    n_assign = t * TOP_K
    p_rows = -(-(n_assign + n_exp * (EXPERT_BLOCK - 1)) // EXPERT_BLOCK) * EXPERT_BLOCK
    idx = tokmeta[:, 0:TOP_K].astype(I32)
    rank = tokmeta[:, TOP_K:2 * TOP_K].astype(I32)
    start_of = jnp.sum(jnp.where(idx[:, :, None] == jnp.arange(n_exp, dtype=I32), pad_start, 0), axis=-1)
    dest = (start_of + rank).reshape(-1)

    n_items = n_exp + p_rows // r_max
    per_e = (padded + r_max - 1) // r_max
    item_end = jnp.cumsum(per_e)
    total = item_end[-1]
    wi = jnp.arange(n_items, dtype=I32)
    valid = wi < total
    wc = jnp.minimum(wi, total - 1)
    ie = jnp.minimum(jnp.sum((item_end[None, :] <= wc[:, None]).astype(I32), axis=1), n_exp - 1)
    local = wc - (item_end[ie] - per_e[ie])
    item_row = jnp.where(valid, pad_start[ie] + local * r_max, 0).astype(I32)
    item_n = jnp.where(valid, jnp.clip(padded[ie] - local * r_max, 0, r_max), 0).astype(I32)

    nt = d // LANES
    xs = _dispatch(dest, xnp, jnp.zeros((p_rows * nt, LANES), F32), td=_tile_rows(t, 256), nt=nt)
    ys = _experts(ie, item_row, item_n, valid.astype(I32), xs, w1[0], b1[0], w2[0], b2[0],
                  r_max=r_max, tw=min(512, 2 * f_dim), nt=nt)
    out = _combine(dest, tokmeta, h1, final_norm_g.reshape(1, d), ys, tc=_tile_rows(t, 256), nt=nt)
    return out.reshape(nb, s_len, d)
```

```python
import functools

import jax
import jax.numpy as jnp
from jax import lax
from jax.experimental import pallas as pl
from jax.experimental.pallas import tpu as pltpu

F32 = jnp.float32
BF16 = jnp.bfloat16
I32 = jnp.int32

CHUNK = 64
POOL_WINDOWS = (2, 4, 8, 16)
V_DIM = 128
QK_NOPE = 128
QK_ROPE = 64
ROPE_BASE = 10000.0
TOP_K = 4
SWIGLU_LIMIT = 7.0
SWIGLU_ALPHA = 1.702
EPS = 1e-5
EXPERT_BLOCK = 256

LANES = 128
SUBLANES = 8
HEAD_PAD = 2 * LANES
VMEM_LIMIT_BYTES = 56 * 1024 * 1024

HALO = 16
NEG_BIG = -1e30


def _rms(x, g):
    ms = jnp.mean(x * x, axis=-1, keepdims=True)
    return x * lax.rsqrt(ms + EPS) * g


def _dot(a, b):
    return jnp.dot(a, b, preferred_element_type=F32)


def _dot_nt(a, b):
    return lax.dot_general(a, b, (((1,), (1,)), ((), ())), preferred_element_type=F32)


def _const_spec(shape):
    nd = len(shape)
    return pl.BlockSpec(shape, lambda *_: (0,) * nd)


def _row_of(ref, r):
    if isinstance(r, int):
        return ref.at[r // SUBLANES, :, r % SUBLANES, :]
    return ref.at[lax.shift_right_logical(r, 3), :, r & (SUBLANES - 1), :]


def _front_kernel(x_ref, mpool_ref, cs_ref, sn_ref, ag_ref, win_ref, pw_ref, ps_ref, qg_ref, wq_ref,
                  wqr_ref, kg_ref, wk_ref, wv_ref,
                  ypool_ref, q_ref, k_ref, v_ref, ptail_ref, ext_ref, *, ts, pool_w, q_lora, kv_lora,
                  n_heads):
    st = pl.program_id(1)
    hn = _rms(x_ref[0], ag_ref[...]).astype(BF16)
    proj = _dot(hn, win_ref[...])
    pool_in = proj[:, :pool_w]

    @pl.when(st == 0)
    def _():
        ext_ref[0:HALO, :] = mpool_ref[...]

    ext_ref[HALO:HALO + ts, :] = pool_in
    gw = pool_w // len(POOL_WINDOWS)
    for g, w in enumerate(POOL_WINDOWS):
        c0 = g * gw
        u = pool_in[:, c0:c0 + gw]
        s = u
        for k in range(1, w):
            s = s + ext_ref[HALO - k:HALO - k + ts, c0:c0 + gw]
        d = (s * (1.0 / w) - u).astype(BF16)
        y = _dot(d, pw_ref[g]) * ps_ref[:, c0:c0 + gw]
        ypool_ref[0, :, c0:c0 + gw] = y.astype(BF16)
    tail = pool_in[ts - HALO:ts, :]
    ext_ref[0:HALO, :] = tail
    ptail_ref[0, 0] = tail

    o = pool_w
    q_c = proj[:, o:o + q_lora]
    o += q_lora
    kv_c = proj[:, o:o + kv_lora]
    o += kv_lora
    kr = proj[:, o:o + LANES]
    kr_rot = proj[:, o + LANES:o + 2 * LANES]
    cs = cs_ref[...]
    sn = sn_ref[...]
    krope = (kr * cs + kr_rot * sn).astype(BF16)
    qn = _rms(q_c, qg_ref[...]).astype(BF16)
    qm = _dot(qn, wq_ref[...])
    qr = _dot(qn, wqr_ref[...])
    kvn = _rms(kv_c, kg_ref[...]).astype(BF16)
    kn = _dot(kvn, wk_ref[...])
    v_ref[0] = _dot(kvn, wv_ref[...]).astype(BF16)
    for h in range(n_heads):
        a = h * HEAD_PAD
        b = h * LANES
        q_ref[0, :, a:a + LANES] = qm[:, a:a + LANES].astype(BF16)
        q_ref[0, :, a + LANES:a + HEAD_PAD] = (
            qm[:, a + LANES:a + HEAD_PAD] * cs + qr[:, b:b + LANES] * sn).astype(BF16)
        k_ref[0, :, a:a + LANES] = kn[:, b:b + LANES].astype(BF16)
        k_ref[0, :, a + LANES:a + HEAD_PAD] = krope


def _front(x3, mpool, cs, sn, wts, *, ts):
    nb, s_len, d = x3.shape
    (ag, win, pw, ps, qg, wq, wqr, kg, wk, wv) = wts
    pool_w = ps.shape[1]
    q_lora = qg.shape[1]
    kv_lora = kg.shape[1]
    n_heads = wk.shape[1] // LANES
    n_st = s_len // ts
    kern = functools.partial(_front_kernel, ts=ts, pool_w=pool_w, q_lora=q_lora, kv_lora=kv_lora,
                             n_heads=n_heads)
    row = lambda b, s: (b, s, 0)
    in_specs = [
        pl.BlockSpec((1, ts, d), row),
        _const_spec(mpool.shape),
        pl.BlockSpec((ts, LANES), lambda b, s: (s, 0)),
        pl.BlockSpec((ts, LANES), lambda b, s: (s, 0)),
    ] + [_const_spec(w.shape) for w in wts]
    out_shape = (
        jax.ShapeDtypeStruct((nb, s_len, pool_w), BF16),
        jax.ShapeDtypeStruct((nb, s_len, n_heads * HEAD_PAD), BF16),
        jax.ShapeDtypeStruct((nb, s_len, n_heads * HEAD_PAD), BF16),
        jax.ShapeDtypeStruct((nb, s_len, n_heads * V_DIM), BF16),
        jax.ShapeDtypeStruct((nb, n_st, HALO, pool_w), F32),
    )
    out_specs = (
        pl.BlockSpec((1, ts, pool_w), row),
        pl.BlockSpec((1, ts, n_heads * HEAD_PAD), row),
        pl.BlockSpec((1, ts, n_heads * HEAD_PAD), row),
        pl.BlockSpec((1, ts, n_heads * V_DIM), row),
        pl.BlockSpec((1, 1, HALO, pool_w), lambda b, s: (b, s, 0, 0)),
    )
    return pl.pallas_call(
        kern, grid=(nb, n_st), in_specs=in_specs, out_specs=out_specs, out_shape=out_shape,
        scratch_shapes=[pltpu.VMEM((HALO + ts, pool_w), F32)],
        compiler_params=pltpu.CompilerParams(dimension_semantics=("arbitrary", "arbitrary"),
                                             vmem_limit_bytes=VMEM_LIMIT_BYTES),
        name="front",
    )(x3, mpool, cs, sn, *wts)


def _attn_kernel(q_ref, k_ref, v_ref, km_ref, vm_ref, o_ref, m_ref, l_ref, acc_ref, *, tq, n_meta, n_heads,
                 scale):
    qi = pl.program_id(1)
    c2 = scale * 1.4426950408889634

    def update(h, s, vb, first):
        s_max = jnp.max(s, axis=1, keepdims=True)
        if first:
            m_new = jnp.broadcast_to(s_max, (tq, LANES))
        else:
            m_old = m_ref[h]
            m_new = jnp.maximum(m_old, s_max)
            alpha = jnp.exp2((m_old - m_new) * c2)
        p = jnp.exp2((s - jnp.concatenate([m_new] * (s.shape[1] // LANES), axis=1)) * c2)
        v1 = jnp.concatenate([vb, jnp.ones(vb.shape, BF16)], axis=1)
        pv = _dot(p.astype(BF16), v1)
        if first:
            l_ref[h] = pv[:, V_DIM:]
            acc_ref[h] = pv[:, :V_DIM]
        else:
            l_ref[h] = alpha * l_ref[h] + pv[:, V_DIM:]
            acc_ref[h] = alpha * acc_ref[h] + pv[:, :V_DIM]
        m_ref[h] = m_new

    def q_of(h):
        return q_ref[0, :, h * HEAD_PAD:(h + 1) * HEAD_PAD]

    r_diag = pl.multiple_of(qi * tq, tq)
    rc = lax.broadcasted_iota(I32, (tq, tq), 0) // CHUNK
    cc = lax.broadcasted_iota(I32, (tq, tq), 1) // CHUNK
    vis = jnp.concatenate([cc <= rc, lax.broadcasted_iota(I32, (tq, LANES), 1) < n_meta], axis=1)
    for h in range(n_heads):
        kd = jnp.concatenate([k_ref[0, pl.ds(r_diag, tq), h * HEAD_PAD:(h + 1) * HEAD_PAD],
                              km_ref[:, h * HEAD_PAD:(h + 1) * HEAD_PAD]], axis=0)
        vd = jnp.concatenate([v_ref[0, pl.ds(r_diag, tq), h * V_DIM:(h + 1) * V_DIM],
                              vm_ref[:, h * V_DIM:(h + 1) * V_DIM]], axis=0)
        s = jnp.where(vis, _dot_nt(q_of(h), kd), -jnp.inf)
        update(h, s, vd, True)

    def body(j, c):
        r0 = pl.multiple_of(j * tq, tq)
        for h in range(n_heads):
            s = _dot_nt(q_of(h), k_ref[0, pl.ds(r0, tq), h * HEAD_PAD:(h + 1) * HEAD_PAD])
            update(h, s, v_ref[0, pl.ds(r0, tq), h * V_DIM:(h + 1) * V_DIM], False)
        return c

    lax.fori_loop(0, qi, body, 0)
    for h in range(n_heads):
        o_ref[0, :, h * V_DIM:(h + 1) * V_DIM] = (acc_ref[h] / l_ref[h]).astype(BF16)


def _attn(q, k, v, km, vm, *, tq, n_meta):
    nb, s_len, hw = q.shape
    n_heads = hw // HEAD_PAD
    kern = functools.partial(_attn_kernel, tq=tq, n_meta=n_meta, n_heads=n_heads,
                             scale=float((QK_NOPE + QK_ROPE) ** -0.5))
    return pl.pallas_call(
        kern, grid=(nb, s_len // tq),
        in_specs=[
            pl.BlockSpec((1, tq, hw), lambda b, i: (b, i, 0)),
            pl.BlockSpec((1, s_len, hw), lambda b, i: (b, 0, 0)),
            pl.BlockSpec((1, s_len, n_heads * V_DIM), lambda b, i: (b, 0, 0)),
            _const_spec(km.shape),
            _const_spec(vm.shape),
        ],
        out_specs=pl.BlockSpec((1, tq, n_heads * V_DIM), lambda b, i: (b, i, 0)),
        out_shape=jax.ShapeDtypeStruct((nb, s_len, n_heads * V_DIM), BF16),
        scratch_shapes=[pltpu.VMEM((n_heads, tq, LANES), F32)] * 3,
        compiler_params=pltpu.CompilerParams(dimension_semantics=("arbitrary", "arbitrary"),
                                             vmem_limit_bytes=VMEM_LIMIT_BYTES),
        name="attn",
    )(q, k, v, km, vm)


def _mid_kernel(yp_ref, ym_ref, x_ref, woa_ref, wob_ref, fg_ref, wr_ref, br_ref,
                h1_ref, xn4_ref, tokmeta_ref, counts_ref, run_ref, *, tm):
    i = pl.program_id(0)

    @pl.when(i == 0)
    def _():
        run_ref[...] = jnp.zeros_like(run_ref)

    h1 = x_ref[...] + _dot(yp_ref[...], woa_ref[...]) + _dot(ym_ref[...], wob_ref[...])
    h1_ref[...] = h1
    xn = _rms(h1, fg_ref[...])
    for c in range(xn4_ref.shape[1]):
        xn4_ref[:, c] = xn[:, c * LANES:(c + 1) * LANES].reshape(tm // SUBLANES, SUBLANES, LANES)
    xb = xn.astype(BF16)

    logits = _dot(xb, wr_ref[...]) + br_ref[...]
    lane = lax.broadcasted_iota(I32, logits.shape, 1).astype(F32)
    work = logits
    idxs, vals = [], []
    for _ in range(TOP_K):
        mx = jnp.max(work, axis=1, keepdims=True)
        ix = jnp.min(jnp.where(work == mx, lane, float(LANES)), axis=1, keepdims=True)
        idxs.append(ix)
        vals.append(mx)
        work = jnp.where(lane == ix, -jnp.inf, work)
    es = [jnp.exp(vv - vals[0]) for vv in vals]
    den = es[0]
    for e in es[1:]:
        den = den + e
    hot = [jnp.where(lane == ix, 1.0, 0.0) for ix in idxs]
    cnt = hot[0]
    for hh in hot[1:]:
        cnt = cnt + hh
    rr = lax.broadcasted_iota(I32, (tm, tm), 0)
    cc = lax.broadcasted_iota(I32, (tm, tm), 1)
    ltri = jnp.where(rr > cc, 1.0, 0.0).astype(BF16)
    base = run_ref[0:1, :] + _dot(ltri, cnt.astype(BF16))
    out = jnp.zeros(logits.shape, F32)
    for k in range(TOP_K):
        rank = jnp.sum(hot[k] * base, axis=1, keepdims=True)
        out = jnp.where(lane == float(k), idxs[k], out)
        out = jnp.where(lane == float(TOP_K + k), rank, out)
        out = jnp.where(lane == float(2 * TOP_K + k), es[k] / den, out)
    tokmeta_ref[...] = out
    run = run_ref[...] + jnp.sum(cnt, axis=0, keepdims=True)
    run_ref[...] = run
    counts_ref[...] = run


def _mid(yp, ym, x2, woa, wob, fg, wr, br, *, tm):
    t, d = x2.shape
    pw = yp.shape[1]
    mw = ym.shape[1]
    nt = d // LANES
    kern = functools.partial(_mid_kernel, tm=tm)
    row = lambda i: (i, 0)
    return pl.pallas_call(
        kern, grid=(t // tm,),
        in_specs=[
            pl.BlockSpec((tm, pw), row), pl.BlockSpec((tm, mw), row), pl.BlockSpec((tm, d), row),
            _const_spec(woa.shape), _const_spec(wob.shape), _const_spec(fg.shape),
            _const_spec(wr.shape), _const_spec(br.shape),
        ],
        out_specs=(
            pl.BlockSpec((tm, d), row), pl.BlockSpec((tm // SUBLANES, nt, SUBLANES, LANES), lambda i: (i, 0, 0, 0)),
            pl.BlockSpec((tm, LANES), row),
            pl.BlockSpec((8, LANES), lambda i: (0, 0)),
        ),
        out_shape=(
            jax.ShapeDtypeStruct((t, d), F32), jax.ShapeDtypeStruct((t // SUBLANES, nt, SUBLANES, LANES), F32),
            jax.ShapeDtypeStruct((t, LANES), F32), jax.ShapeDtypeStruct((8, LANES), F32),
        ),
        scratch_shapes=[pltpu.VMEM((8, LANES), F32)],
        compiler_params=pltpu.CompilerParams(dimension_semantics=("arbitrary",),
                                             vmem_limit_bytes=VMEM_LIMIT_BYTES),
        name="mid",
    )(yp, ym, x2, woa, wob, fg, wr, br)


ZERO_GROUPS = 16


def _dispatch_kernel(zstart_ref, zlen_ref, tail_ref, dest_ref, xn4_ref, xs_ref, zbuf, sem, sem_z, *, td, n_exp):
    def body(r, c):
        for k in range(TOP_K):
            pltpu.make_async_copy(_row_of(xn4_ref, r), _row_of(xs_ref, dest_ref[r * TOP_K + k]), sem).start()
        return c

    lax.fori_loop(0, td, body, 0, unroll=8)
    for _ in range(TOP_K):
        pltpu.make_async_copy(xn4_ref, xs_ref.at[pl.ds(0, td // SUBLANES)], sem).wait()

    @pl.when(pl.program_id(0) == pl.num_programs(0) - 1)
    def _zero_fill():
        zbuf[...] = jnp.zeros_like(zbuf)
        tail0 = tail_ref[0]
        n_tail = tail_ref[1]

        def pad_copies(e, wait):
            zs = zstart_ref[e]
            zl = zlen_ref[e]
            head = jnp.minimum((-zs) & (SUBLANES - 1), zl)
            for h in range(SUBLANES - 1):
                @pl.when(h < head)
                def _(h=h):
                    cp = pltpu.make_async_copy(_row_of(zbuf, 0), _row_of(xs_ref, zs + h), sem_z)
                    cp.wait() if wait else cp.start()
            g0 = lax.shift_right_logical(zs + head, 3)
            ng = lax.shift_right_logical(zl - head, 3)
            v = ZERO_GROUPS
            while v >= 1:
                @pl.when((ng & v) != 0)
                def _(v=v):
                    off = g0 + (ng & (-2 * v))
                    cp = pltpu.make_async_copy(zbuf.at[pl.ds(0, v)], xs_ref.at[pl.ds(off, v)], sem_z)
                    cp.wait() if wait else cp.start()
                v //= 2

        def tail_copy(i, wait):
            cp = pltpu.make_async_copy(zbuf, xs_ref.at[pl.ds(tail0 + i * ZERO_GROUPS, ZERO_GROUPS)], sem_z)
            cp.wait() if wait else cp.start()

        for wait in (False, True):
            lax.fori_loop(0, n_exp, lambda e, c, wait=wait: (pad_copies(e, wait), c)[1], 0)
            lax.fori_loop(0, n_tail, lambda i, c, wait=wait: (tail_copy(i, wait), c)[1], 0)


def _dispatch(zstart, zlen, tail, dest_flat, xn4, p_rows, *, td):
    tg, nt, _, _ = xn4.shape
    n_exp = zstart.shape[0]
    kern = functools.partial(_dispatch_kernel, td=td, n_exp=n_exp)
    grid_spec = pltpu.PrefetchScalarGridSpec(
        num_scalar_prefetch=3, grid=(tg * SUBLANES // td,),
        in_specs=[
            pl.BlockSpec((td * TOP_K,), lambda i, *_: (i,), memory_space=pltpu.SMEM),
            pl.BlockSpec((td // SUBLANES, nt, SUBLANES, LANES), lambda i, *_: (i, 0, 0, 0)),
        ],
        out_specs=pl.BlockSpec(memory_space=pl.ANY),
        scratch_shapes=[pltpu.VMEM((ZERO_GROUPS, nt, SUBLANES, LANES), F32), pltpu.SemaphoreType.DMA,
                        pltpu.SemaphoreType.DMA],
    )
    return pl.pallas_call(
        kern, grid_spec=grid_spec,
        out_shape=jax.ShapeDtypeStruct((p_rows // SUBLANES, nt, SUBLANES, LANES), F32),
        compiler_params=pltpu.CompilerParams(dimension_semantics=("arbitrary",),
                                             vmem_limit_bytes=VMEM_LIMIT_BYTES),
        name="dispatch",
    )(zstart, zlen, tail, dest_flat, xn4)


def _expert_kernel(item_e_ref, item_row_ref, item_n_ref, item_valid_ref,
                   xs_ref, w1_ref, b1_ref, w2_ref, b2_ref, ys_ref,
                   xbuf, abuf, stage, wperm, w2b, hcbuf, sem_s, *, rb, j1, tw, nt):
    del item_e_ref, item_valid_ref, xs_ref
    w = pl.program_id(0)
    j = pl.program_id(1)
    nrows = item_n_ref[w]
    row0 = pl.multiple_of(item_row_ref[w], rb)
    g0 = lax.shift_right_logical(row0, 3)
    nrb = nrows // rb
    gb = rb // SUBLANES
    d = w2_ref.shape[1]
    tf = w2_ref.shape[0]

    def rows(i):
        return pl.ds(pl.multiple_of(i * rb, rb), rb)

    def stage_copy(i, slot, to_hbm):
        hbm = ys_ref.at[pl.ds(g0 + i * gb, gb)]
        vmem = stage.at[slot]
        return pltpu.make_async_copy(vmem, hbm, sem_s.at[slot]) if to_hbm else pltpu.make_async_copy(
            hbm, vmem, sem_s.at[slot])

    @pl.when((j == 0) & (nrows > 0))
    def _load_rows():
        stage_copy(0, 0, False).start()

        def body(i, c):
            slot = i & 1

            @pl.when(i + 1 < nrb)
            def _():
                stage_copy(i + 1, 1 - slot, False).start()

            stage_copy(i, slot, False).wait()
            for cc in range(nt):
                xbuf[rows(i), cc * LANES:(cc + 1) * LANES] = stage[slot, :, cc].reshape(rb, LANES).astype(BF16)
            return c

        lax.fori_loop(0, nrb, body, 0)

    @pl.when((j < j1) & (nrows > 0))
    def _phase1():
        for s in range(d // LANES):
            for g in range(tf // LANES):
                top = w2_ref[g * LANES:g * LANES + LANES // 2, s * LANES:(s + 1) * LANES]
                bot = w2_ref[g * LANES + LANES // 2:(g + 1) * LANES, s * LANES:(s + 1) * LANES]
                wperm.at[s][pl.ds(g * LANES, LANES // 2, stride=2), :] = top
                wperm.at[s][pl.ds(g * LANES + 1, LANES // 2, stride=2), :] = bot
        k0 = pl.multiple_of(j * tf, tf)
        for s in range(d // LANES):
            w2b[pl.ds(k0, tf), s * LANES:(s + 1) * LANES] = wperm[s].astype(BF16)

        lane = lax.broadcasted_iota(I32, (rb, LANES), 1)
        even = (lane & 1) == 0

        def dot1(i):
            hcbuf[i & 1] = _dot(xbuf[rows(i), :], w1_ref[...].astype(BF16))

        def tail1(i):
            hc = hcbuf[i & 1] + b1_ref[...]
            outs = []
            for q in range(tw // (2 * LANES)):
                c0 = hc[:, 2 * q * LANES:(2 * q + 1) * LANES]
                c1 = hc[:, (2 * q + 1) * LANES:(2 * q + 2) * LANES]
                glu = jnp.where(even, c0, pltpu.roll(c1, 1, 1))
                lin = jnp.where(even, pltpu.roll(c0, LANES - 1, 1), c1)
                glu = jnp.minimum(glu, SWIGLU_LIMIT)
                lin = jnp.clip(lin, -SWIGLU_LIMIT, SWIGLU_LIMIT)
                act = glu * (1.0 / (1.0 + jnp.exp(-SWIGLU_ALPHA * glu))) * (lin + 1.0)
                outs.append(act.astype(BF16))
            abuf[j, rows(i), :] = jnp.concatenate(outs, axis=1)

        dot1(0)

        def body(i, c):
            tail1(i - 1)
            dot1(i)
            return c

        lax.fori_loop(1, nrb, body, 0)
        tail1(nrb - 1)

    @pl.when((j == j1) & (nrows > 0))
    def _phase2():
        def body(i, c):
            @pl.when(i >= 2)
            def _():
                stage_copy(i - 2, i & 1, True).wait()

            @pl.when(i >= 1)
            def _():
                stage_copy(i - 1, (i - 1) & 1, True).start()

            a = jnp.concatenate([abuf[jj, rows(i), :] for jj in range(j1)], axis=1)
            y = _dot(a, w2b[...]) + b2_ref[...]
            for cc in range(nt):
                stage[i & 1, :, cc] = y[:, cc * LANES:(cc + 1) * LANES].reshape(gb, SUBLANES, LANES)
            return c

        lax.fori_loop(0, nrb, body, 0)

        @pl.when(nrb >= 2)
        def _():
            stage_copy(nrb - 2, nrb & 1, True).wait()

        last = stage_copy(nrb - 1, (nrb - 1) & 1, True)
        last.start()
        last.wait()


def _experts(item_e, item_row, item_n, item_valid, xs, w1, b1, w2, b2, *, r_max, tw):
    n_exp, d, f2 = w1.shape
    f_dim = w2.shape[1]
    nt = xs.shape[1]
    j1 = f2 // tw
    tf = f_dim // j1
    assert tf == tw // 2 and tf % LANES == 0 and nt * LANES == d
    rb = EXPERT_BLOCK
    kern = functools.partial(_expert_kernel, rb=rb, j1=j1, tw=tw, nt=nt)

    def w_step(w, j, iv):
        return jnp.minimum(jnp.where(iv[w] == 1, j, j1), j1 - 1)

    def w1_map(w, j, ie, ir, inn, iv):
        return (ie[w], 0, w_step(w, j, iv))

    def w2_map(w, j, ie, ir, inn, iv):
        return (ie[w], w_step(w, j, iv), 0)

    def e_map(w, j, ie, ir, inn, iv):
        return (ie[w], 0, 0)

    grid_spec = pltpu.PrefetchScalarGridSpec(
        num_scalar_prefetch=4, grid=(item_e.shape[0], j1 + 1),
        in_specs=[
            pl.BlockSpec(memory_space=pl.ANY),
            pl.BlockSpec((None, d, tw), w1_map),
            pl.BlockSpec((None, 1, tw), w1_map),
            pl.BlockSpec((None, tf, d), w2_map),
            pl.BlockSpec((None, 1, d), e_map),
        ],
        out_specs=pl.BlockSpec(memory_space=pl.ANY),
        scratch_shapes=[
            pltpu.VMEM((r_max, d), BF16),
            pltpu.VMEM((j1, r_max, tf), BF16),
            pltpu.VMEM((2, rb // SUBLANES, nt, SUBLANES, LANES), F32),
            pltpu.VMEM((d // LANES, tf, LANES), F32),
            pltpu.VMEM((f_dim, d), BF16),
            pltpu.VMEM((2, rb, tw), F32),
            pltpu.SemaphoreType.DMA((2,)),
        ],
    )
    return pl.pallas_call(
        kern, grid_spec=grid_spec,
        out_shape=jax.ShapeDtypeStruct(xs.shape, F32),
        input_output_aliases={4: 0},
        compiler_params=pltpu.CompilerParams(dimension_semantics=("arbitrary", "arbitrary"),
                                             vmem_limit_bytes=VMEM_LIMIT_BYTES),
        name="experts",
    )(item_e, item_row, item_n, item_valid, xs, w1, b1.reshape(n_exp, 1, f2), w2, b2.reshape(n_exp, 1, d))


def _combine_kernel(dest_ref, tokmeta_ref, h1_ref, fg_ref, ys_ref, o_ref, buf, sem, *, tc, nt):
    def body(r, c):
        for k in range(TOP_K):
            pltpu.make_async_copy(_row_of(ys_ref, dest_ref[r * TOP_K + k]), _row_of(buf.at[k], r), sem).start()
        return c

    lax.fori_loop(0, tc, body, 0, unroll=8)
    for k in range(TOP_K):
        pltpu.make_async_copy(ys_ref.at[pl.ds(0, tc // SUBLANES)], buf.at[k], sem).wait()
    tm = tokmeta_ref[...]
    cols = [h1_ref[:, c * LANES:(c + 1) * LANES] for c in range(nt)]
    for k in range(TOP_K):
        gate = tm[:, 2 * TOP_K + k:2 * TOP_K + k + 1]
        cols = [a + gate * buf[k, :, c].reshape(tc, LANES) for c, a in enumerate(cols)]
    o_ref[...] = _rms(jnp.concatenate(cols, axis=1), fg_ref[...])


def _combine(dest_flat, tokmeta, h1, fg, ys, *, tc):
    t, d = h1.shape
    nt = ys.shape[1]
    kern = functools.partial(_combine_kernel, tc=tc, nt=nt)
    row = lambda i: (i, 0)
    return pl.pallas_call(
        kern, grid=(t // tc,),
        in_specs=[
            pl.BlockSpec((tc * TOP_K,), lambda i: (i,), memory_space=pltpu.SMEM),
            pl.BlockSpec((tc, LANES), row), pl.BlockSpec((tc, d), row), _const_spec(fg.shape),
            pl.BlockSpec(memory_space=pl.ANY),
        ],
        out_specs=pl.BlockSpec((tc, d), row),
        out_shape=jax.ShapeDtypeStruct((t, d), F32),
        scratch_shapes=[pltpu.VMEM((TOP_K, tc // SUBLANES, nt, SUBLANES, LANES), F32), pltpu.SemaphoreType.DMA],
        compiler_params=pltpu.CompilerParams(dimension_semantics=("arbitrary",),
                                             vmem_limit_bytes=VMEM_LIMIT_BYTES),
        name="combine",
    )(dest_flat, tokmeta, h1, fg, ys)


def _rot_cols(w):
    h = QK_ROPE // 2
    return jnp.concatenate([-w[..., h:], w[..., :h]], axis=-1)


def _tile_rows(n, cap):
    t = min(n, cap)
    assert n % t == 0, (n, cap)
    return t


def kernel(x, meta_tokens, attn_norm_g, w_in, q_norm_g, w_uq, kv_norm_g, w_ukv, pool_w, pool_scale, w_o,
           ffn_norm_g, w_router, b_router, w1, b1, w2, b2, final_norm_g):
    nb, s_len, d = x.shape
    n_meta = meta_tokens.shape[0]
    assert w_in.shape[0] == 1, "one layer"
    assert n_meta == HALO and max(POOL_WINDOWS) - 1 <= HALO
    pw = pool_scale.shape[1]
    q_lora = q_norm_g.shape[1]
    kv_lora = kv_norm_g.shape[1]
    n_heads = w_uq.shape[2] // (QK_NOPE + QK_ROPE)
    n_exp = w_router.shape[2]
    f_dim = w2.shape[2]
    t = nb * s_len
    assert s_len % CHUNK == 0 and n_exp <= LANES and pw // len(POOL_WINDOWS) % LANES == 0

    win = w_in[0]
    o = pw + q_lora + kv_lora
    w_kr = win[:, o:o + QK_ROPE]
    zc = jnp.zeros((d, LANES - QK_ROPE), F32)
    win_b = jnp.concatenate([win[:, :o], w_kr, zc, _rot_cols(w_kr), zc], axis=1).astype(BF16)
    wq3 = w_uq[0].reshape(q_lora, n_heads, QK_NOPE + QK_ROPE)
    zq = jnp.zeros((q_lora, n_heads, LANES - QK_ROPE), F32)
    wq_b = jnp.concatenate([wq3, zq], axis=2).reshape(q_lora, n_heads * HEAD_PAD).astype(BF16)
    wqr_b = jnp.concatenate([_rot_cols(wq3[:, :, QK_NOPE:]), zq], axis=2).reshape(
        q_lora, n_heads * LANES).astype(BF16)
    wkv3 = w_ukv[0].reshape(kv_lora, n_heads, QK_NOPE + V_DIM)
    wk_b = wkv3[:, :, :QK_NOPE].reshape(kv_lora, n_heads * QK_NOPE).astype(BF16)
    wv_b = wkv3[:, :, QK_NOPE:].reshape(kv_lora, n_heads * V_DIM).astype(BF16)
    front_w = (attn_norm_g, win_b, pool_w[0].astype(BF16), pool_scale, q_norm_g, wq_b, wqr_b, kv_norm_g,
               wk_b, wv_b)
    woa = w_o[0, :pw].astype(BF16)
    wob = w_o[0, pw:].astype(BF16)
    wr_b = jnp.pad(w_router[0], ((0, 0), (0, LANES - n_exp))).astype(BF16)
    br = jnp.pad(b_router, ((0, 0), (0, LANES - n_exp)), constant_values=NEG_BIG)

    pos = jnp.arange(n_meta + s_len, dtype=F32)
    inv_freq = 1.0 / (ROPE_BASE ** (jnp.arange(0, QK_ROPE, 2, dtype=F32) / QK_ROPE))
    ang = pos[:, None] * inv_freq[None, :]
    ones = jnp.ones((n_meta + s_len, LANES - QK_ROPE), F32)
    cs = jnp.concatenate([jnp.cos(ang), jnp.cos(ang), ones], axis=1)
    sn = jnp.concatenate([jnp.sin(ang), jnp.sin(ang), 0.0 * ones], axis=1)

    zero_halo = jnp.zeros((HALO, pw), F32)
    _, _, k_meta, v_meta, p_meta = _front(meta_tokens[None], zero_halo, cs[:n_meta], sn[:n_meta], front_w,
                                          ts=n_meta)
    ts = _tile_rows(s_len, 256)
    y_pool, q, k, v, _ = _front(x, p_meta[0, 0], cs[n_meta:], sn[n_meta:], front_w, ts=ts)

    km = jnp.pad(k_meta[0], ((0, LANES - n_meta), (0, 0)))
    vm = jnp.pad(v_meta[0], ((0, LANES - n_meta), (0, 0)))
    y_mla = _attn(q, k, v, km, vm, tq=_tile_rows(s_len, 256), n_meta=n_meta)

    tm = _tile_rows(t, 256)
    h1, xn4, tokmeta, counts = _mid(y_pool.reshape(t, pw), y_mla.reshape(t, -1), x.reshape(t, d), woa, wob,
                                    ffn_norm_g, wr_b, br, tm=tm)

    r_max = 10 * EXPERT_BLOCK
    cnt = counts[0, :n_exp].astype(I32)
    padded = (cnt + EXPERT_BLOCK - 1) // EXPERT_BLOCK * EXPERT_BLOCK
    pad_end = jnp.cumsum(padded)
    pad_start = pad_end - padded
    n_assign = t * TOP_K
    p_rows = -(-(n_assign + n_exp * (EXPERT_BLOCK - 1)) // EXPERT_BLOCK) * EXPERT_BLOCK
    idx = tokmeta[:, 0:TOP_K].astype(I32)
    rank = tokmeta[:, TOP_K:2 * TOP_K].astype(I32)
    start_of = jnp.sum(jnp.where(idx[:, :, None] == jnp.arange(n_exp, dtype=I32), pad_start, 0), axis=-1)
    dest = (start_of + rank).reshape(-1)

    n_items = n_exp + p_rows // r_max
    per_e = (padded + r_max - 1) // r_max
    item_end = jnp.cumsum(per_e)
    total = item_end[-1]
    wi = jnp.arange(n_items, dtype=I32)
    valid = wi < total
    wc = jnp.minimum(wi, total - 1)
    ie = jnp.minimum(jnp.sum((item_end[None, :] <= wc[:, None]).astype(I32), axis=1), n_exp - 1)
    local = wc - (item_end[ie] - per_e[ie])
    item_row = jnp.where(valid, pad_start[ie] + local * r_max, 0).astype(I32)
    item_n = jnp.where(valid, jnp.clip(padded[ie] - local * r_max, 0, r_max), 0).astype(I32)

    zstart = (pad_start + cnt).astype(I32)
    zlen = (padded - cnt).astype(I32)
    tail = jnp.stack([pad_end[-1] // SUBLANES, (p_rows - pad_end[-1]) // (SUBLANES * ZERO_GROUPS)]).astype(I32)
    xs = _dispatch(zstart, zlen, tail, dest, xn4, p_rows, td=_tile_rows(t, 256))
    ys = _experts(ie, item_row, item_n, valid.astype(I32), xs, w1[0], b1[0], w2[0], b2[0],
                  r_max=r_max, tw=min(512, 2 * f_dim))
    out = _combine(dest, tokmeta, h1, final_norm_g.reshape(1, d), ys, tc=_tile_rows(t, 256))
    return out.reshape(nb, s_len, d)
```

```python
import functools

import jax
import jax.numpy as jnp
from jax import lax
from jax.experimental import pallas as pl
from jax.experimental.pallas import tpu as pltpu

F32 = jnp.float32
BF16 = jnp.bfloat16
I32 = jnp.int32

CHUNK = 64
POOL_WINDOWS = (2, 4, 8, 16)
V_DIM = 128
QK_NOPE = 128
QK_ROPE = 64
ROPE_BASE = 10000.0
TOP_K = 4
SWIGLU_LIMIT = 7.0
SWIGLU_ALPHA = 1.702
EPS = 1e-5
EXPERT_BLOCK = 256

LANES = 128
SUBLANES = 8
HEAD_PAD = 2 * LANES
VMEM_LIMIT_BYTES = 56 * 1024 * 1024

HALO = 16
NEG_BIG = -1e30


def _rms(x, g):
    ms = jnp.mean(x * x, axis=-1, keepdims=True)
    return x * lax.rsqrt(ms + EPS) * g


def _dot(a, b):
    return jnp.dot(a, b, preferred_element_type=F32)


def _dot_nt(a, b):
    return lax.dot_general(a, b, (((1,), (1,)), ((), ())), preferred_element_type=F32)


def _const_spec(shape):
    nd = len(shape)
    return pl.BlockSpec(shape, lambda *_: (0,) * nd)


def _row_of(ref, r):
    if isinstance(r, int):
        return ref.at[r // SUBLANES, :, r % SUBLANES, :]
    return ref.at[lax.shift_right_logical(r, 3), :, r & (SUBLANES - 1), :]


def _front_kernel(x_ref, mpool_ref, cs_ref, sn_ref, ag_ref, win_ref, pw_ref, ps_ref, qg_ref, wq_ref,
                  wqr_ref, kg_ref, wk_ref, wv_ref,
                  ypool_ref, q_ref, k_ref, v_ref, ptail_ref, ext_ref, *, ts, pool_w, q_lora, kv_lora,
                  n_heads):
    st = pl.program_id(1)
    hn = _rms(x_ref[0], ag_ref[...]).astype(BF16)
    proj = _dot(hn, win_ref[...])
    pool_in = proj[:, :pool_w]

    @pl.when(st == 0)
    def _():
        ext_ref[0:HALO, :] = mpool_ref[...]

    ext_ref[HALO:HALO + ts, :] = pool_in
    gw = pool_w // len(POOL_WINDOWS)
    for g, w in enumerate(POOL_WINDOWS):
        c0 = g * gw
        u = pool_in[:, c0:c0 + gw]
        s = u
        for k in range(1, w):
            s = s + ext_ref[HALO - k:HALO - k + ts, c0:c0 + gw]
        d = (s * (1.0 / w) - u).astype(BF16)
        y = _dot(d, pw_ref[g]) * ps_ref[:, c0:c0 + gw]
        ypool_ref[0, :, c0:c0 + gw] = y.astype(BF16)
    tail = pool_in[ts - HALO:ts, :]
    ext_ref[0:HALO, :] = tail
    ptail_ref[0, 0] = tail

    o = pool_w
    q_c = proj[:, o:o + q_lora]
    o += q_lora
    kv_c = proj[:, o:o + kv_lora]
    o += kv_lora
    kr = proj[:, o:o + LANES]
    kr_rot = proj[:, o + LANES:o + 2 * LANES]
    cs = cs_ref[...]
    sn = sn_ref[...]
    krope = (kr * cs + kr_rot * sn).astype(BF16)
    qn = _rms(q_c, qg_ref[...]).astype(BF16)
    qm = _dot(qn, wq_ref[...])
    qr = _dot(qn, wqr_ref[...])
    kvn = _rms(kv_c, kg_ref[...]).astype(BF16)
    kn = _dot(kvn, wk_ref[...])
    v_ref[0] = _dot(kvn, wv_ref[...]).astype(BF16)
    for h in range(n_heads):
        a = h * HEAD_PAD
        b = h * LANES
        q_ref[0, :, a:a + LANES] = qm[:, a:a + LANES].astype(BF16)
        q_ref[0, :, a + LANES:a + HEAD_PAD] = (
            qm[:, a + LANES:a + HEAD_PAD] * cs + qr[:, b:b + LANES] * sn).astype(BF16)
        k_ref[0, :, a:a + LANES] = kn[:, b:b + LANES].astype(BF16)
        k_ref[0, :, a + LANES:a + HEAD_PAD] = krope


def _front(x3, mpool, cs, sn, wts, *, ts):
    nb, s_len, d = x3.shape
    (ag, win, pw, ps, qg, wq, wqr, kg, wk, wv) = wts
    pool_w = ps.shape[1]
    q_lora = qg.shape[1]
    kv_lora = kg.shape[1]
    n_heads = wk.shape[1] // LANES
    n_st = s_len // ts
    kern = functools.partial(_front_kernel, ts=ts, pool_w=pool_w, q_lora=q_lora, kv_lora=kv_lora,
                             n_heads=n_heads)
    row = lambda b, s: (b, s, 0)
    in_specs = [
        pl.BlockSpec((1, ts, d), row),
        _const_spec(mpool.shape),
        pl.BlockSpec((ts, LANES), lambda b, s: (s, 0)),
        pl.BlockSpec((ts, LANES), lambda b, s: (s, 0)),
    ] + [_const_spec(w.shape) for w in wts]
    out_shape = (
        jax.ShapeDtypeStruct((nb, s_len, pool_w), BF16),
        jax.ShapeDtypeStruct((nb, s_len, n_heads * HEAD_PAD), BF16),
        jax.ShapeDtypeStruct((nb, s_len, n_heads * HEAD_PAD), BF16),
        jax.ShapeDtypeStruct((nb, s_len, n_heads * V_DIM), BF16),
        jax.ShapeDtypeStruct((nb, n_st, HALO, pool_w), F32),
    )
    out_specs = (
        pl.BlockSpec((1, ts, pool_w), row),
        pl.BlockSpec((1, ts, n_heads * HEAD_PAD), row),
        pl.BlockSpec((1, ts, n_heads * HEAD_PAD), row),
        pl.BlockSpec((1, ts, n_heads * V_DIM), row),
        pl.BlockSpec((1, 1, HALO, pool_w), lambda b, s: (b, s, 0, 0)),
    )
    return pl.pallas_call(
        kern, grid=(nb, n_st), in_specs=in_specs, out_specs=out_specs, out_shape=out_shape,
        scratch_shapes=[pltpu.VMEM((HALO + ts, pool_w), F32)],
        compiler_params=pltpu.CompilerParams(dimension_semantics=("arbitrary", "arbitrary"),
                                             vmem_limit_bytes=VMEM_LIMIT_BYTES),
        name="front",
    )(x3, mpool, cs, sn, *wts)


def _attn_kernel(q_ref, k_ref, v_ref, km_ref, vm_ref, o_ref, m_ref, l_ref, acc_ref, *, tq, n_meta, n_heads,
                 scale):
    qi = pl.program_id(1)
    c2 = scale * 1.4426950408889634

    def update(h, s, vb, first):
        s_max = jnp.max(s, axis=1, keepdims=True)
        if first:
            m_new = jnp.broadcast_to(s_max, (tq, LANES))
        else:
            m_old = m_ref[h]
            m_new = jnp.maximum(m_old, s_max)
            alpha = jnp.exp2((m_old - m_new) * c2)
        p = jnp.exp2((s - jnp.concatenate([m_new] * (s.shape[1] // LANES), axis=1)) * c2)
        v1 = jnp.concatenate([vb, jnp.ones(vb.shape, BF16)], axis=1)
        pv = _dot(p.astype(BF16), v1)
        if first:
            l_ref[h] = pv[:, V_DIM:]
            acc_ref[h] = pv[:, :V_DIM]
        else:
            l_ref[h] = alpha * l_ref[h] + pv[:, V_DIM:]
            acc_ref[h] = alpha * acc_ref[h] + pv[:, :V_DIM]
        m_ref[h] = m_new

    def q_of(h):
        return q_ref[0, :, h * HEAD_PAD:(h + 1) * HEAD_PAD]

    r_diag = pl.multiple_of(qi * tq, tq)
    rc = lax.broadcasted_iota(I32, (tq, tq), 0) // CHUNK
    cc = lax.broadcasted_iota(I32, (tq, tq), 1) // CHUNK
    vis = jnp.concatenate([cc <= rc, lax.broadcasted_iota(I32, (tq, LANES), 1) < n_meta], axis=1)
    for h in range(n_heads):
        kd = jnp.concatenate([k_ref[0, pl.ds(r_diag, tq), h * HEAD_PAD:(h + 1) * HEAD_PAD],
                              km_ref[:, h * HEAD_PAD:(h + 1) * HEAD_PAD]], axis=0)
        vd = jnp.concatenate([v_ref[0, pl.ds(r_diag, tq), h * V_DIM:(h + 1) * V_DIM],
                              vm_ref[:, h * V_DIM:(h + 1) * V_DIM]], axis=0)
        s = jnp.where(vis, _dot_nt(q_of(h), kd), -jnp.inf)
        update(h, s, vd, True)

    def body(j, c):
        r0 = pl.multiple_of(j * tq, tq)
        for h in range(n_heads):
            s = _dot_nt(q_of(h), k_ref[0, pl.ds(r0, tq), h * HEAD_PAD:(h + 1) * HEAD_PAD])
            update(h, s, v_ref[0, pl.ds(r0, tq), h * V_DIM:(h + 1) * V_DIM], False)
        return c

    lax.fori_loop(0, qi, body, 0)
    for h in range(n_heads):
        o_ref[0, :, h * V_DIM:(h + 1) * V_DIM] = (acc_ref[h] / l_ref[h]).astype(BF16)


def _attn(q, k, v, km, vm, *, tq, n_meta):
    nb, s_len, hw = q.shape
    n_heads = hw // HEAD_PAD
    kern = functools.partial(_attn_kernel, tq=tq, n_meta=n_meta, n_heads=n_heads,
                             scale=float((QK_NOPE + QK_ROPE) ** -0.5))
    return pl.pallas_call(
        kern, grid=(nb, s_len // tq),
        in_specs=[
            pl.BlockSpec((1, tq, hw), lambda b, i: (b, i, 0)),
            pl.BlockSpec((1, s_len, hw), lambda b, i: (b, 0, 0)),
            pl.BlockSpec((1, s_len, n_heads * V_DIM), lambda b, i: (b, 0, 0)),
            _const_spec(km.shape),
            _const_spec(vm.shape),
        ],
        out_specs=pl.BlockSpec((1, tq, n_heads * V_DIM), lambda b, i: (b, i, 0)),
        out_shape=jax.ShapeDtypeStruct((nb, s_len, n_heads * V_DIM), BF16),
        scratch_shapes=[pltpu.VMEM((n_heads, tq, LANES), F32)] * 3,
        compiler_params=pltpu.CompilerParams(dimension_semantics=("arbitrary", "arbitrary"),
                                             vmem_limit_bytes=VMEM_LIMIT_BYTES),
        name="attn",
    )(q, k, v, km, vm)


def _mid_kernel(yp_ref, ym_ref, x_ref, woa_ref, wob_ref, fg_ref, wr_ref, br_ref,
                h1_ref, xn4_ref, tokmeta_ref, counts_ref, run_ref, *, tm):
    i = pl.program_id(0)

    @pl.when(i == 0)
    def _():
        run_ref[...] = jnp.zeros_like(run_ref)

    h1 = x_ref[...] + _dot(yp_ref[...], woa_ref[...]) + _dot(ym_ref[...], wob_ref[...])
    h1_ref[...] = h1
    xn = _rms(h1, fg_ref[...])
    for c in range(xn4_ref.shape[1]):
        xn4_ref[:, c] = xn[:, c * LANES:(c + 1) * LANES].reshape(tm // SUBLANES, SUBLANES, LANES)
    xb = xn.astype(BF16)

    logits = _dot(xb, wr_ref[...]) + br_ref[...]
    lane = lax.broadcasted_iota(I32, logits.shape, 1).astype(F32)
    work = logits
    idxs, vals = [], []
    for _ in range(TOP_K):
        mx = jnp.max(work, axis=1, keepdims=True)
        ix = jnp.min(jnp.where(work == mx, lane, float(LANES)), axis=1, keepdims=True)
        idxs.append(ix)
        vals.append(mx)
        work = jnp.where(lane == ix, -jnp.inf, work)
    es = [jnp.exp(vv - vals[0]) for vv in vals]
    den = es[0]
    for e in es[1:]:
        den = den + e
    hot = [jnp.where(lane == ix, 1.0, 0.0) for ix in idxs]
    cnt = hot[0]
    for hh in hot[1:]:
        cnt = cnt + hh
    rr = lax.broadcasted_iota(I32, (tm, tm), 0)
    cc = lax.broadcasted_iota(I32, (tm, tm), 1)
    ltri = jnp.where(rr > cc, 1.0, 0.0).astype(BF16)
    base = run_ref[0:1, :] + _dot(ltri, cnt.astype(BF16))
    out = jnp.zeros(logits.shape, F32)
    for k in range(TOP_K):
        rank = jnp.sum(hot[k] * base, axis=1, keepdims=True)
        out = jnp.where(lane == float(k), idxs[k], out)
        out = jnp.where(lane == float(TOP_K + k), rank, out)
        out = jnp.where(lane == float(2 * TOP_K + k), es[k] / den, out)
    tokmeta_ref[...] = out
    run = run_ref[...] + jnp.sum(cnt, axis=0, keepdims=True)
    run_ref[...] = run
    counts_ref[...] = run


def _mid(yp, ym, x2, woa, wob, fg, wr, br, *, tm):
    t, d = x2.shape
    pw = yp.shape[1]
    mw = ym.shape[1]
    nt = d // LANES
    kern = functools.partial(_mid_kernel, tm=tm)
    row = lambda i: (i, 0)
    return pl.pallas_call(
        kern, grid=(t // tm,),
        in_specs=[
            pl.BlockSpec((tm, pw), row), pl.BlockSpec((tm, mw), row), pl.BlockSpec((tm, d), row),
            _const_spec(woa.shape), _const_spec(wob.shape), _const_spec(fg.shape),
            _const_spec(wr.shape), _const_spec(br.shape),
        ],
        out_specs=(
            pl.BlockSpec((tm, d), row), pl.BlockSpec((tm // SUBLANES, nt, SUBLANES, LANES), lambda i: (i, 0, 0, 0)),
            pl.BlockSpec((tm, LANES), row),
            pl.BlockSpec((8, LANES), lambda i: (0, 0)),
        ),
        out_shape=(
            jax.ShapeDtypeStruct((t, d), F32), jax.ShapeDtypeStruct((t // SUBLANES, nt, SUBLANES, LANES), F32),
            jax.ShapeDtypeStruct((t, LANES), F32), jax.ShapeDtypeStruct((8, LANES), F32),
        ),
        scratch_shapes=[pltpu.VMEM((8, LANES), F32)],
        compiler_params=pltpu.CompilerParams(dimension_semantics=("arbitrary",),
                                             vmem_limit_bytes=VMEM_LIMIT_BYTES),
        name="mid",
    )(yp, ym, x2, woa, wob, fg, wr, br)


ZERO_RECORDS = 128


def _dispatch_kernel(zstart_ref, zlen_ref, tail_ref, dest_ref, xn4_ref, xs_ref, zbuf, sem, sem_z, *, td, n_exp):
    def body(r, c):
        for k in range(TOP_K):
            pltpu.make_async_copy(_row_of(xn4_ref, r), xs_ref.at[dest_ref[r * TOP_K + k]], sem).start()
        return c

    lax.fori_loop(0, td, body, 0, unroll=8)
    for _ in range(TOP_K):
        pltpu.make_async_copy(xn4_ref, xn4_ref, sem).wait()

    @pl.when(pl.program_id(0) == pl.num_programs(0) - 1)
    def _zero_fill():
        zbuf[...] = jnp.zeros_like(zbuf)
        tail0 = tail_ref[0]
        n_tail = tail_ref[1]

        def pad_copies(e, wait):
            zs = zstart_ref[e]
            zl = zlen_ref[e]
            v = ZERO_RECORDS
            while v >= 1:
                @pl.when((zl & v) != 0)
                def _(v=v):
                    off = zs + (zl & (-2 * v))
                    cp = pltpu.make_async_copy(zbuf.at[pl.ds(0, v)], xs_ref.at[pl.ds(off, v)], sem_z)
                    cp.wait() if wait else cp.start()
                v //= 2

        def tail_copy(i, wait):
            cp = pltpu.make_async_copy(zbuf, xs_ref.at[pl.ds(tail0 + i * ZERO_RECORDS, ZERO_RECORDS)], sem_z)
            cp.wait() if wait else cp.start()

        for wait in (False, True):
            lax.fori_loop(0, n_exp, lambda e, c, wait=wait: (pad_copies(e, wait), c)[1], 0)
            lax.fori_loop(0, n_tail, lambda i, c, wait=wait: (tail_copy(i, wait), c)[1], 0)


def _dispatch(zstart, zlen, tail, dest_flat, xn4, p_rows, *, td):
    tg, nt, _, _ = xn4.shape
    n_exp = zstart.shape[0]
    kern = functools.partial(_dispatch_kernel, td=td, n_exp=n_exp)
    grid_spec = pltpu.PrefetchScalarGridSpec(
        num_scalar_prefetch=3, grid=(tg * SUBLANES // td,),
        in_specs=[
            pl.BlockSpec((td * TOP_K,), lambda i, *_: (i,), memory_space=pltpu.SMEM),
            pl.BlockSpec((td // SUBLANES, nt, SUBLANES, LANES), lambda i, *_: (i, 0, 0, 0)),
        ],
        out_specs=pl.BlockSpec(memory_space=pl.ANY),
        scratch_shapes=[pltpu.VMEM((ZERO_RECORDS, nt, LANES), F32), pltpu.SemaphoreType.DMA,
                        pltpu.SemaphoreType.DMA],
    )
    return pl.pallas_call(
        kern, grid_spec=grid_spec,
        out_shape=jax.ShapeDtypeStruct((p_rows, nt, LANES), F32),
        compiler_params=pltpu.CompilerParams(dimension_semantics=("arbitrary",),
                                             vmem_limit_bytes=VMEM_LIMIT_BYTES),
        name="dispatch",
    )(zstart, zlen, tail, dest_flat, xn4)


def _expert_kernel(item_e_ref, item_row_ref, item_n_ref, item_valid_ref,
                   xs_ref, w1_ref, b1_ref, w2_ref, b2_ref, ys_ref,
                   xbuf, abuf, stage, wperm, w2b, hcbuf, sem_s, *, rb, j1, tw, nt):
    del item_e_ref, item_valid_ref, xs_ref
    w = pl.program_id(0)
    j = pl.program_id(1)
    nrows = item_n_ref[w]
    row0 = pl.multiple_of(item_row_ref[w], rb)
    nrb = nrows // rb
    d = w2_ref.shape[1]
    tf = w2_ref.shape[0]

    def rows(i):
        return pl.ds(pl.multiple_of(i * rb, rb), rb)

    def stage_copies(i, slot, to_hbm):
        cps = []
        for c in range(nt):
            hbm = ys_ref.at[pl.ds(row0 + i * rb, rb), c, :]
            vmem = stage.at[slot, :, c * LANES:(c + 1) * LANES]
            cps.append(pltpu.make_async_copy(vmem, hbm, sem_s.at[slot]) if to_hbm
                       else pltpu.make_async_copy(hbm, vmem, sem_s.at[slot]))
        return cps

    def start_all(cps):
        for cp in cps:
            cp.start()

    def wait_all(cps):
        for cp in cps:
            cp.wait()

    def slot_of(jj, i):
        return (jj * nrb + i) & 1

    def dot1(i, x):
        hcbuf[slot_of(j, i)] = _dot(x, w1_ref[...].astype(BF16)) + b1_ref[...]

    lane = lax.broadcasted_iota(I32, (rb, LANES), 1)
    even = (lane & 1) == 0

    def tail1(jj, i):
        hc = hcbuf[slot_of(jj, i)]
        outs = []
        for q in range(tw // (2 * LANES)):
            c0 = hc[:, 2 * q * LANES:(2 * q + 1) * LANES]
            c1 = hc[:, (2 * q + 1) * LANES:(2 * q + 2) * LANES]
            glu = jnp.where(even, c0, pltpu.roll(c1, 1, 1))
            lin = jnp.where(even, pltpu.roll(c0, LANES - 1, 1), c1)
            glu = jnp.minimum(glu, SWIGLU_LIMIT)
            lin = jnp.clip(lin, -SWIGLU_LIMIT, SWIGLU_LIMIT)
            act = glu * (1.0 / (1.0 + jnp.exp(-SWIGLU_ALPHA * glu))) * (lin + 1.0)
            outs.append(act.astype(BF16))
        abuf[jj, rows(i), :] = jnp.concatenate(outs, axis=1)

    def permute_w2_tile():
        for s in range(d // LANES):
            for g in range(tf // LANES):
                top = w2_ref[g * LANES:g * LANES + LANES // 2, s * LANES:(s + 1) * LANES]
                bot = w2_ref[g * LANES + LANES // 2:(g + 1) * LANES, s * LANES:(s + 1) * LANES]
                wperm.at[s][pl.ds(g * LANES, LANES // 2, stride=2), :] = top
                wperm.at[s][pl.ds(g * LANES + 1, LANES // 2, stride=2), :] = bot
        k0 = pl.multiple_of(j * tf, tf)
        for s in range(d // LANES):
            w2b[pl.ds(k0, tf), s * LANES:(s + 1) * LANES] = wperm[s].astype(BF16)

    @pl.when((j == 0) & (nrows > 0))
    def _first_step():
        def fetch(i):
            wait_all(stage_copies(i, i & 1, False))

            @pl.when(i + 1 < nrb)
            def _():
                start_all(stage_copies(i + 1, (i + 1) & 1, False))

        def load_and_dot(i):
            x = stage[i & 1].astype(BF16)
            xbuf[rows(i), :] = x
            dot1(i, x)

        start_all(stage_copies(0, 0, False))
        fetch(0)
        load_and_dot(0)
        permute_w2_tile()

        def body(i, c):
            fetch(i)
            tail1(0, i - 1)
            load_and_dot(i)
            return c

        lax.fori_loop(1, nrb, body, 0)

    @pl.when((j > 0) & (j < j1) & (nrows > 0))
    def _next_steps():
        tail1(j - 1, nrb - 1)
        dot1(0, xbuf[rows(0), :])
        permute_w2_tile()

        def body(i, c):
            tail1(j, i - 1)
            dot1(i, xbuf[rows(i), :])
            return c

        lax.fori_loop(1, nrb, body, 0)

    @pl.when((j == j1) & (nrows > 0))
    def _last_step():
        tail1(j1 - 1, nrb - 1)

        def body(i, c):
            @pl.when(i >= 2)
            def _():
                wait_all(stage_copies(i - 2, i & 1, True))

            @pl.when(i >= 1)
            def _():
                start_all(stage_copies(i - 1, (i - 1) & 1, True))

            a = jnp.concatenate([abuf[jj, rows(i), :] for jj in range(j1)], axis=1)
            stage[i & 1] = _dot(a, w2b[...]) + b2_ref[...]
            return c

        lax.fori_loop(0, nrb, body, 0)

        @pl.when(nrb >= 2)
        def _():
            wait_all(stage_copies(nrb - 2, nrb & 1, True))

        last = stage_copies(nrb - 1, (nrb - 1) & 1, True)
        start_all(last)
        wait_all(last)


def _experts(item_e, item_row, item_n, item_valid, xs, w1, b1, w2, b2, *, r_max, tw):
    n_exp, d, f2 = w1.shape
    f_dim = w2.shape[1]
    nt = xs.shape[1]
    j1 = f2 // tw
    tf = f_dim // j1
    assert tf == tw // 2 and tf % LANES == 0 and nt * LANES == d
    rb = EXPERT_BLOCK
    kern = functools.partial(_expert_kernel, rb=rb, j1=j1, tw=tw, nt=nt)

    def w_step(w, j, iv):
        return jnp.minimum(jnp.where(iv[w] == 1, j, j1), j1 - 1)

    def w1_map(w, j, ie, ir, inn, iv):
        return (ie[w], 0, w_step(w, j, iv))

    def w2_map(w, j, ie, ir, inn, iv):
        return (ie[w], w_step(w, j, iv), 0)

    def e_map(w, j, ie, ir, inn, iv):
        return (ie[w], 0, 0)

    grid_spec = pltpu.PrefetchScalarGridSpec(
        num_scalar_prefetch=4, grid=(item_e.shape[0], j1 + 1),
        in_specs=[
            pl.BlockSpec(memory_space=pl.ANY),
            pl.BlockSpec((None, d, tw), w1_map),
            pl.BlockSpec((None, 1, tw), w1_map),
            pl.BlockSpec((None, tf, d), w2_map),
            pl.BlockSpec((None, 1, d), e_map),
        ],
        out_specs=pl.BlockSpec(memory_space=pl.ANY),
        scratch_shapes=[
            pltpu.VMEM((r_max, d), BF16),
            pltpu.VMEM((j1, r_max, tf), BF16),
            pltpu.VMEM((2, rb, d), F32),
            pltpu.VMEM((d // LANES, tf, LANES), F32),
            pltpu.VMEM((f_dim, d), BF16),
            pltpu.VMEM((2, rb, tw), F32),
            pltpu.SemaphoreType.DMA((2,)),
        ],
    )
    return pl.pallas_call(
        kern, grid_spec=grid_spec,
        out_shape=jax.ShapeDtypeStruct(xs.shape, F32),
        input_output_aliases={4: 0},
        compiler_params=pltpu.CompilerParams(dimension_semantics=("arbitrary", "arbitrary"),
                                             vmem_limit_bytes=VMEM_LIMIT_BYTES),
        name="experts",
    )(item_e, item_row, item_n, item_valid, xs, w1, b1.reshape(n_exp, 1, f2), w2, b2.reshape(n_exp, 1, d))


def _combine_kernel(dest_ref, tokmeta_ref, h1_ref, fg_ref, ys_ref, o_ref, buf, sem, *, tc, nt):
    def body(r, c):
        for k in range(TOP_K):
            pltpu.make_async_copy(ys_ref.at[dest_ref[r * TOP_K + k]], _row_of(buf.at[k], r), sem).start()
        return c

    lax.fori_loop(0, tc, body, 0, unroll=8)
    for k in range(TOP_K):
        pltpu.make_async_copy(buf.at[k], buf.at[k], sem).wait()
    tm = tokmeta_ref[...]
    cols = [h1_ref[:, c * LANES:(c + 1) * LANES] for c in range(nt)]
    for k in range(TOP_K):
        gate = tm[:, 2 * TOP_K + k:2 * TOP_K + k + 1]
        cols = [a + gate * buf[k, :, c].reshape(tc, LANES) for c, a in enumerate(cols)]
    o_ref[...] = _rms(jnp.concatenate(cols, axis=1), fg_ref[...])


def _combine(dest_flat, tokmeta, h1, fg, ys, *, tc):
    t, d = h1.shape
    nt = ys.shape[1]
    kern = functools.partial(_combine_kernel, tc=tc, nt=nt)
    row = lambda i: (i, 0)
    return pl.pallas_call(
        kern, grid=(t // tc,),
        in_specs=[
            pl.BlockSpec((tc * TOP_K,), lambda i: (i,), memory_space=pltpu.SMEM),
            pl.BlockSpec((tc, LANES), row), pl.BlockSpec((tc, d), row), _const_spec(fg.shape),
            pl.BlockSpec(memory_space=pl.ANY),
        ],
        out_specs=pl.BlockSpec((tc, d), row),
        out_shape=jax.ShapeDtypeStruct((t, d), F32),
        scratch_shapes=[pltpu.VMEM((TOP_K, tc // SUBLANES, nt, SUBLANES, LANES), F32), pltpu.SemaphoreType.DMA],
        compiler_params=pltpu.CompilerParams(dimension_semantics=("arbitrary",),
                                             vmem_limit_bytes=VMEM_LIMIT_BYTES),
        name="combine",
    )(dest_flat, tokmeta, h1, fg, ys)


def _rot_cols(w):
    h = QK_ROPE // 2
    return jnp.concatenate([-w[..., h:], w[..., :h]], axis=-1)


def _tile_rows(n, cap):
    t = min(n, cap)
    assert n % t == 0, (n, cap)
    return t


def kernel(x, meta_tokens, attn_norm_g, w_in, q_norm_g, w_uq, kv_norm_g, w_ukv, pool_w, pool_scale, w_o,
           ffn_norm_g, w_router, b_router, w1, b1, w2, b2, final_norm_g):
    nb, s_len, d = x.shape
    n_meta = meta_tokens.shape[0]
    assert w_in.shape[0] == 1, "one layer"
    assert n_meta == HALO and max(POOL_WINDOWS) - 1 <= HALO
    pw = pool_scale.shape[1]
    q_lora = q_norm_g.shape[1]
    kv_lora = kv_norm_g.shape[1]
    n_heads = w_uq.shape[2] // (QK_NOPE + QK_ROPE)
    n_exp = w_router.shape[2]
    f_dim = w2.shape[2]
    t = nb * s_len
    assert s_len % CHUNK == 0 and n_exp <= LANES and pw // len(POOL_WINDOWS) % LANES == 0

    win = w_in[0]
    o = pw + q_lora + kv_lora
    w_kr = win[:, o:o + QK_ROPE]
    zc = jnp.zeros((d, LANES - QK_ROPE), F32)
    win_b = jnp.concatenate([win[:, :o], w_kr, zc, _rot_cols(w_kr), zc], axis=1).astype(BF16)
    wq3 = w_uq[0].reshape(q_lora, n_heads, QK_NOPE + QK_ROPE)
    zq = jnp.zeros((q_lora, n_heads, LANES - QK_ROPE), F32)
    wq_b = jnp.concatenate([wq3, zq], axis=2).reshape(q_lora, n_heads * HEAD_PAD).astype(BF16)
    wqr_b = jnp.concatenate([_rot_cols(wq3[:, :, QK_NOPE:]), zq], axis=2).reshape(
        q_lora, n_heads * LANES).astype(BF16)
    wkv3 = w_ukv[0].reshape(kv_lora, n_heads, QK_NOPE + V_DIM)
    wk_b = wkv3[:, :, :QK_NOPE].reshape(kv_lora, n_heads * QK_NOPE).astype(BF16)
    wv_b = wkv3[:, :, QK_NOPE:].reshape(kv_lora, n_heads * V_DIM).astype(BF16)
    front_w = (attn_norm_g, win_b, pool_w[0].astype(BF16), pool_scale, q_norm_g, wq_b, wqr_b, kv_norm_g,
               wk_b, wv_b)
    woa = w_o[0, :pw].astype(BF16)
    wob = w_o[0, pw:].astype(BF16)
    wr_b = jnp.pad(w_router[0], ((0, 0), (0, LANES - n_exp))).astype(BF16)
    br = jnp.pad(b_router, ((0, 0), (0, LANES - n_exp)), constant_values=NEG_BIG)

    pos = jnp.arange(n_meta + s_len, dtype=F32)
    inv_freq = 1.0 / (ROPE_BASE ** (jnp.arange(0, QK_ROPE, 2, dtype=F32) / QK_ROPE))
    ang = pos[:, None] * inv_freq[None, :]
    ones = jnp.ones((n_meta + s_len, LANES - QK_ROPE), F32)
    cs = jnp.concatenate([jnp.cos(ang), jnp.cos(ang), ones], axis=1)
    sn = jnp.concatenate([jnp.sin(ang), jnp.sin(ang), 0.0 * ones], axis=1)

    zero_halo = jnp.zeros((HALO, pw), F32)
    _, _, k_meta, v_meta, p_meta = _front(meta_tokens[None], zero_halo, cs[:n_meta], sn[:n_meta], front_w,
                                          ts=n_meta)
    ts = _tile_rows(s_len, 256)
    y_pool, q, k, v, _ = _front(x, p_meta[0, 0], cs[n_meta:], sn[n_meta:], front_w, ts=ts)

    km = jnp.pad(k_meta[0], ((0, LANES - n_meta), (0, 0)))
    vm = jnp.pad(v_meta[0], ((0, LANES - n_meta), (0, 0)))
    y_mla = _attn(q, k, v, km, vm, tq=_tile_rows(s_len, 256), n_meta=n_meta)

    tm = _tile_rows(t, 256)
    h1, xn4, tokmeta, counts = _mid(y_pool.reshape(t, pw), y_mla.reshape(t, -1), x.reshape(t, d), woa, wob,
                                    ffn_norm_g, wr_b, br, tm=tm)

    r_max = 10 * EXPERT_BLOCK
    cnt = counts[0, :n_exp].astype(I32)
    padded = (cnt + EXPERT_BLOCK - 1) // EXPERT_BLOCK * EXPERT_BLOCK
    pad_end = jnp.cumsum(padded)
    pad_start = pad_end - padded
    n_assign = t * TOP_K
    p_rows = -(-(n_assign + n_exp * (EXPERT_BLOCK - 1)) // EXPERT_BLOCK) * EXPERT_BLOCK
    idx = tokmeta[:, 0:TOP_K].astype(I32)
    rank = tokmeta[:, TOP_K:2 * TOP_K].astype(I32)
    start_of = jnp.sum(jnp.where(idx[:, :, None] == jnp.arange(n_exp, dtype=I32), pad_start, 0), axis=-1)
    dest = (start_of + rank).reshape(-1)

    n_items = n_exp + p_rows // r_max
    per_e = (padded + r_max - 1) // r_max
    item_end = jnp.cumsum(per_e)
    total = item_end[-1]
    wi = jnp.arange(n_items, dtype=I32)
    valid = wi < total
    wc = jnp.minimum(wi, total - 1)
    ie = jnp.minimum(jnp.sum((item_end[None, :] <= wc[:, None]).astype(I32), axis=1), n_exp - 1)
    local = wc - (item_end[ie] - per_e[ie])
    item_row = jnp.where(valid, pad_start[ie] + local * r_max, 0).astype(I32)
    item_n = jnp.where(valid, jnp.clip(padded[ie] - local * r_max, 0, r_max), 0).astype(I32)

    zstart = (pad_start + cnt).astype(I32)
    zlen = (padded - cnt).astype(I32)
    tail = jnp.stack([pad_end[-1], (p_rows - pad_end[-1]) // ZERO_RECORDS]).astype(I32)
    xs = _dispatch(zstart, zlen, tail, dest, xn4, p_rows, td=_tile_rows(t, 256))
    ys = _experts(ie, item_row, item_n, valid.astype(I32), xs, w1[0], b1[0], w2[0], b2[0],
                  r_max=r_max, tw=min(512, 2 * f_dim))
    out = _combine(dest, tokmeta, h1, final_norm_g.reshape(1, d), ys, tc=_tile_rows(t, 256))
    return out.reshape(nb, s_len, d)
```

```python
import functools

import jax
import jax.numpy as jnp
from jax import lax
from jax.experimental import pallas as pl
from jax.experimental.pallas import tpu as pltpu

F32 = jnp.float32
BF16 = jnp.bfloat16
I32 = jnp.int32

CHUNK = 64
POOL_WINDOWS = (2, 4, 8, 16)
V_DIM = 128
QK_NOPE = 128
QK_ROPE = 64
ROPE_BASE = 10000.0
TOP_K = 4
SWIGLU_LIMIT = 7.0
SWIGLU_ALPHA = 1.702
EPS = 1e-5
EXPERT_BLOCK = 256

LANES = 128
SUBLANES = 8
HEAD_PAD = 2 * LANES
VMEM_LIMIT_BYTES = 56 * 1024 * 1024

HALO = 16
NEG_BIG = -1e30


def _rms(x, g):
    ms = jnp.mean(x * x, axis=-1, keepdims=True)
    return x * lax.rsqrt(ms + EPS) * g


def _dot(a, b):
    return jnp.dot(a, b, preferred_element_type=F32)


def _dot_nt(a, b):
    return lax.dot_general(a, b, (((1,), (1,)), ((), ())), preferred_element_type=F32)


def _const_spec(shape):
    nd = len(shape)
    return pl.BlockSpec(shape, lambda *_: (0,) * nd)


def _row_of(ref, r):
    if isinstance(r, int):
        return ref.at[r // SUBLANES, :, r % SUBLANES, :]
    return ref.at[lax.shift_right_logical(r, 3), :, r & (SUBLANES - 1), :]


def _front_kernel(x_ref, mpool_ref, cs_ref, sn_ref, ag_ref, win_ref, pw_ref, ps_ref, qg_ref, wq_ref,
                  wqr_ref, kg_ref, wk_ref, wv_ref,
                  ypool_ref, q_ref, k_ref, v_ref, ptail_ref, ext_ref, *, ts, pool_w, q_lora, kv_lora,
                  n_heads):
    st = pl.program_id(1)
    hn = _rms(x_ref[0], ag_ref[...]).astype(BF16)
    proj = _dot(hn, win_ref[...])
    pool_in = proj[:, :pool_w]

    @pl.when(st == 0)
    def _():
        ext_ref[0:HALO, :] = mpool_ref[...]

    ext_ref[HALO:HALO + ts, :] = pool_in
    gw = pool_w // len(POOL_WINDOWS)
    for g, w in enumerate(POOL_WINDOWS):
        c0 = g * gw
        u = pool_in[:, c0:c0 + gw]
        s = u
        for k in range(1, w):
            s = s + ext_ref[HALO - k:HALO - k + ts, c0:c0 + gw]
        d = (s * (1.0 / w) - u).astype(BF16)
        y = _dot(d, pw_ref[g]) * ps_ref[:, c0:c0 + gw]
        ypool_ref[0, :, c0:c0 + gw] = y.astype(BF16)
    tail = pool_in[ts - HALO:ts, :]
    ext_ref[0:HALO, :] = tail
    ptail_ref[0, 0] = tail

    o = pool_w
    q_c = proj[:, o:o + q_lora]
    o += q_lora
    kv_c = proj[:, o:o + kv_lora]
    o += kv_lora
    kr = proj[:, o:o + LANES]
    kr_rot = proj[:, o + LANES:o + 2 * LANES]
    cs = cs_ref[...]
    sn = sn_ref[...]
    krope = (kr * cs + kr_rot * sn).astype(BF16)
    qn = _rms(q_c, qg_ref[...]).astype(BF16)
    qm = _dot(qn, wq_ref[...])
    qr = _dot(qn, wqr_ref[...])
    kvn = _rms(kv_c, kg_ref[...]).astype(BF16)
    kn = _dot(kvn, wk_ref[...])
    v_ref[0] = _dot(kvn, wv_ref[...]).astype(BF16)
    for h in range(n_heads):
        a = h * HEAD_PAD
        b = h * LANES
        q_ref[0, :, a:a + LANES] = qm[:, a:a + LANES].astype(BF16)
        q_ref[0, :, a + LANES:a + HEAD_PAD] = (
            qm[:, a + LANES:a + HEAD_PAD] * cs + qr[:, b:b + LANES] * sn).astype(BF16)
        k_ref[0, :, a:a + LANES] = kn[:, b:b + LANES].astype(BF16)
        k_ref[0, :, a + LANES:a + HEAD_PAD] = krope


def _front(x3, mpool, cs, sn, wts, *, ts):
    nb, s_len, d = x3.shape
    (ag, win, pw, ps, qg, wq, wqr, kg, wk, wv) = wts
    pool_w = ps.shape[1]
    q_lora = qg.shape[1]
    kv_lora = kg.shape[1]
    n_heads = wk.shape[1] // LANES
    n_st = s_len // ts
    kern = functools.partial(_front_kernel, ts=ts, pool_w=pool_w, q_lora=q_lora, kv_lora=kv_lora,
                             n_heads=n_heads)
    row = lambda b, s: (b, s, 0)
    in_specs = [
        pl.BlockSpec((1, ts, d), row),
        _const_spec(mpool.shape),
        pl.BlockSpec((ts, LANES), lambda b, s: (s, 0)),
        pl.BlockSpec((ts, LANES), lambda b, s: (s, 0)),
    ] + [_const_spec(w.shape) for w in wts]
    out_shape = (
        jax.ShapeDtypeStruct((nb, s_len, pool_w), BF16),
        jax.ShapeDtypeStruct((nb, s_len, n_heads * HEAD_PAD), BF16),
        jax.ShapeDtypeStruct((nb, s_len, n_heads * HEAD_PAD), BF16),
        jax.ShapeDtypeStruct((nb, s_len, n_heads * V_DIM), BF16),
        jax.ShapeDtypeStruct((nb, n_st, HALO, pool_w), F32),
    )
    out_specs = (
        pl.BlockSpec((1, ts, pool_w), row),
        pl.BlockSpec((1, ts, n_heads * HEAD_PAD), row),
        pl.BlockSpec((1, ts, n_heads * HEAD_PAD), row),
        pl.BlockSpec((1, ts, n_heads * V_DIM), row),
        pl.BlockSpec((1, 1, HALO, pool_w), lambda b, s: (b, s, 0, 0)),
    )
    return pl.pallas_call(
        kern, grid=(nb, n_st), in_specs=in_specs, out_specs=out_specs, out_shape=out_shape,
        scratch_shapes=[pltpu.VMEM((HALO + ts, pool_w), F32)],
        compiler_params=pltpu.CompilerParams(dimension_semantics=("arbitrary", "arbitrary"),
                                             vmem_limit_bytes=VMEM_LIMIT_BYTES),
        name="front",
    )(x3, mpool, cs, sn, *wts)


def _attn_kernel(q_ref, k_ref, v_ref, km_ref, vm_ref, o_ref, m_ref, l_ref, acc_ref, *, tq, n_meta, n_heads,
                 scale):
    qi = pl.program_id(1)
    c2 = scale * 1.4426950408889634

    def update(h, s, vb, first):
        s_max = jnp.max(s, axis=1, keepdims=True)
        if first:
            m_new = jnp.broadcast_to(s_max, (tq, LANES))
        else:
            m_old = m_ref[h]
            m_new = jnp.maximum(m_old, s_max)
            alpha = jnp.exp2((m_old - m_new) * c2)
        p = jnp.exp2((s - jnp.concatenate([m_new] * (s.shape[1] // LANES), axis=1)) * c2)
        v1 = jnp.concatenate([vb, jnp.ones(vb.shape, BF16)], axis=1)
        pv = _dot(p.astype(BF16), v1)
        if first:
            l_ref[h] = pv[:, V_DIM:]
            acc_ref[h] = pv[:, :V_DIM]
        else:
            l_ref[h] = alpha * l_ref[h] + pv[:, V_DIM:]
            acc_ref[h] = alpha * acc_ref[h] + pv[:, :V_DIM]
        m_ref[h] = m_new

    def q_of(h):
        return q_ref[0, :, h * HEAD_PAD:(h + 1) * HEAD_PAD]

    r_diag = pl.multiple_of(qi * tq, tq)
    rc = lax.broadcasted_iota(I32, (tq, tq), 0) // CHUNK
    cc = lax.broadcasted_iota(I32, (tq, tq), 1) // CHUNK
    vis = jnp.concatenate([cc <= rc, lax.broadcasted_iota(I32, (tq, LANES), 1) < n_meta], axis=1)
    for h in range(n_heads):
        kd = jnp.concatenate([k_ref[0, pl.ds(r_diag, tq), h * HEAD_PAD:(h + 1) * HEAD_PAD],
                              km_ref[:, h * HEAD_PAD:(h + 1) * HEAD_PAD]], axis=0)
        vd = jnp.concatenate([v_ref[0, pl.ds(r_diag, tq), h * V_DIM:(h + 1) * V_DIM],
                              vm_ref[:, h * V_DIM:(h + 1) * V_DIM]], axis=0)
        s = jnp.where(vis, _dot_nt(q_of(h), kd), -jnp.inf)
        update(h, s, vd, True)

    def body(j, c):
        r0 = pl.multiple_of(j * tq, tq)
        for h in range(n_heads):
            s = _dot_nt(q_of(h), k_ref[0, pl.ds(r0, tq), h * HEAD_PAD:(h + 1) * HEAD_PAD])
            update(h, s, v_ref[0, pl.ds(r0, tq), h * V_DIM:(h + 1) * V_DIM], False)
        return c

    lax.fori_loop(0, qi, body, 0)
    for h in range(n_heads):
        o_ref[0, :, h * V_DIM:(h + 1) * V_DIM] = (acc_ref[h] / l_ref[h]).astype(BF16)


def _attn(q, k, v, km, vm, *, tq, n_meta):
    nb, s_len, hw = q.shape
    n_heads = hw // HEAD_PAD
    kern = functools.partial(_attn_kernel, tq=tq, n_meta=n_meta, n_heads=n_heads,
                             scale=float((QK_NOPE + QK_ROPE) ** -0.5))
    return pl.pallas_call(
        kern, grid=(nb, s_len // tq),
        in_specs=[
            pl.BlockSpec((1, tq, hw), lambda b, i: (b, i, 0)),
            pl.BlockSpec((1, s_len, hw), lambda b, i: (b, 0, 0)),
            pl.BlockSpec((1, s_len, n_heads * V_DIM), lambda b, i: (b, 0, 0)),
            _const_spec(km.shape),
            _const_spec(vm.shape),
        ],
        out_specs=pl.BlockSpec((1, tq, n_heads * V_DIM), lambda b, i: (b, i, 0)),
        out_shape=jax.ShapeDtypeStruct((nb, s_len, n_heads * V_DIM), BF16),
        scratch_shapes=[pltpu.VMEM((n_heads, tq, LANES), F32)] * 3,
        compiler_params=pltpu.CompilerParams(dimension_semantics=("arbitrary", "arbitrary"),
                                             vmem_limit_bytes=VMEM_LIMIT_BYTES),
        name="attn",
    )(q, k, v, km, vm)


def _mid_kernel(yp_ref, ym_ref, x_ref, woa_ref, wob_ref, fg_ref, wr_ref, br_ref,
                h1_ref, xn4_ref, tokmeta_ref, counts_ref, run_ref, *, tm):
    i = pl.program_id(0)

    @pl.when(i == 0)
    def _():
        run_ref[...] = jnp.zeros_like(run_ref)

    h1 = x_ref[...] + _dot(yp_ref[...], woa_ref[...]) + _dot(ym_ref[...], wob_ref[...])
    h1_ref[...] = h1
    xn = _rms(h1, fg_ref[...])
    for c in range(xn4_ref.shape[1]):
        xn4_ref[:, c] = xn[:, c * LANES:(c + 1) * LANES].reshape(tm // SUBLANES, SUBLANES, LANES)
    xb = xn.astype(BF16)

    logits = _dot(xb, wr_ref[...]) + br_ref[...]
    lane = lax.broadcasted_iota(I32, logits.shape, 1).astype(F32)
    work = logits
    idxs, vals = [], []
    for _ in range(TOP_K):
        mx = jnp.max(work, axis=1, keepdims=True)
        ix = jnp.min(jnp.where(work == mx, lane, float(LANES)), axis=1, keepdims=True)
        idxs.append(ix)
        vals.append(mx)
        work = jnp.where(lane == ix, -jnp.inf, work)
    es = [jnp.exp(vv - vals[0]) for vv in vals]
    den = es[0]
    for e in es[1:]:
        den = den + e
    hot = [jnp.where(lane == ix, 1.0, 0.0) for ix in idxs]
    cnt = hot[0]
    for hh in hot[1:]:
        cnt = cnt + hh
    rr = lax.broadcasted_iota(I32, (tm, tm), 0)
    cc = lax.broadcasted_iota(I32, (tm, tm), 1)
    ltri = jnp.where(rr > cc, 1.0, 0.0).astype(BF16)
    base = run_ref[0:1, :] + _dot(ltri, cnt.astype(BF16))
    out = jnp.zeros(logits.shape, F32)
    for k in range(TOP_K):
        rank = jnp.sum(hot[k] * base, axis=1, keepdims=True)
        out = jnp.where(lane == float(k), idxs[k], out)
        out = jnp.where(lane == float(TOP_K + k), rank, out)
        out = jnp.where(lane == float(2 * TOP_K + k), es[k] / den, out)
    tokmeta_ref[...] = out
    run = run_ref[...] + jnp.sum(cnt, axis=0, keepdims=True)
    run_ref[...] = run
    counts_ref[...] = run


def _mid(yp, ym, x2, woa, wob, fg, wr, br, *, tm):
    t, d = x2.shape
    pw = yp.shape[1]
    mw = ym.shape[1]
    nt = d // LANES
    kern = functools.partial(_mid_kernel, tm=tm)
    row = lambda i: (i, 0)
    return pl.pallas_call(
        kern, grid=(t // tm,),
        in_specs=[
            pl.BlockSpec((tm, pw), row), pl.BlockSpec((tm, mw), row), pl.BlockSpec((tm, d), row),
            _const_spec(woa.shape), _const_spec(wob.shape), _const_spec(fg.shape),
            _const_spec(wr.shape), _const_spec(br.shape),
        ],
        out_specs=(
            pl.BlockSpec((tm, d), row), pl.BlockSpec((tm // SUBLANES, nt, SUBLANES, LANES), lambda i: (i, 0, 0, 0)),
            pl.BlockSpec((tm, LANES), row),
            pl.BlockSpec((8, LANES), lambda i: (0, 0)),
        ),
        out_shape=(
            jax.ShapeDtypeStruct((t, d), F32), jax.ShapeDtypeStruct((t // SUBLANES, nt, SUBLANES, LANES), F32),
            jax.ShapeDtypeStruct((t, LANES), F32), jax.ShapeDtypeStruct((8, LANES), F32),
        ),
        scratch_shapes=[pltpu.VMEM((8, LANES), F32)],
        compiler_params=pltpu.CompilerParams(dimension_semantics=("arbitrary",),
                                             vmem_limit_bytes=VMEM_LIMIT_BYTES),
        name="mid",
    )(yp, ym, x2, woa, wob, fg, wr, br)


ZERO_GROUPS = 16


def _dispatch_kernel(zstart_ref, zlen_ref, tail_ref, dest_ref, xn4_ref, xs_ref, zbuf, sem, sem_z, *, td, n_exp):
    def body(r, c):
        for k in range(TOP_K):
            pltpu.make_async_copy(_row_of(xn4_ref, r), _row_of(xs_ref, dest_ref[r * TOP_K + k]), sem).start(
                priority=k % 2)
        return c

    lax.fori_loop(0, td, body, 0, unroll=8)
    for _ in range(TOP_K):
        pltpu.make_async_copy(xn4_ref, xn4_ref, sem).wait()

    @pl.when(pl.program_id(0) == pl.num_programs(0) - 1)
    def _zero_fill():
        zbuf[...] = jnp.zeros_like(zbuf)
        tail0 = tail_ref[0]
        n_tail = tail_ref[1]

        def pad_copies(e, wait):
            zs = zstart_ref[e]
            zl = zlen_ref[e]
            head = jnp.minimum((-zs) & (SUBLANES - 1), zl)
            for h in range(SUBLANES - 1):
                @pl.when(h < head)
                def _(h=h):
                    cp = pltpu.make_async_copy(_row_of(zbuf, 0), _row_of(xs_ref, zs + h), sem_z)
                    cp.wait() if wait else cp.start()
            g0 = lax.shift_right_logical(zs + head, 3)
            ng = lax.shift_right_logical(zl - head, 3)
            v = ZERO_GROUPS
            while v >= 1:
                @pl.when((ng & v) != 0)
                def _(v=v):
                    off = g0 + (ng & (-2 * v))
                    cp = pltpu.make_async_copy(zbuf.at[pl.ds(0, v)], xs_ref.at[pl.ds(off, v)], sem_z)
                    cp.wait() if wait else cp.start()
                v //= 2

        def tail_copy(i, wait):
            cp = pltpu.make_async_copy(zbuf, xs_ref.at[pl.ds(tail0 + i * ZERO_GROUPS, ZERO_GROUPS)], sem_z)
            cp.wait() if wait else cp.start()

        for wait in (False, True):
            lax.fori_loop(0, n_exp, lambda e, c, wait=wait: (pad_copies(e, wait), c)[1], 0)
            lax.fori_loop(0, n_tail, lambda i, c, wait=wait: (tail_copy(i, wait), c)[1], 0)


def _dispatch(zstart, zlen, tail, dest_flat, xn4, p_rows, *, td):
    tg, nt, _, _ = xn4.shape
    n_exp = zstart.shape[0]
    kern = functools.partial(_dispatch_kernel, td=td, n_exp=n_exp)
    grid_spec = pltpu.PrefetchScalarGridSpec(
        num_scalar_prefetch=3, grid=(tg * SUBLANES // td,),
        in_specs=[
            pl.BlockSpec((td * TOP_K,), lambda i, *_: (i,), memory_space=pltpu.SMEM),
            pl.BlockSpec((td // SUBLANES, nt, SUBLANES, LANES), lambda i, *_: (i, 0, 0, 0)),
        ],
        out_specs=pl.BlockSpec(memory_space=pl.ANY),
        scratch_shapes=[pltpu.VMEM((ZERO_GROUPS, nt, SUBLANES, LANES), F32), pltpu.SemaphoreType.DMA,
                        pltpu.SemaphoreType.DMA],
    )
    return pl.pallas_call(
        kern, grid_spec=grid_spec,
        out_shape=jax.ShapeDtypeStruct((p_rows // SUBLANES, nt, SUBLANES, LANES), F32),
        compiler_params=pltpu.CompilerParams(dimension_semantics=("arbitrary",),
                                             vmem_limit_bytes=VMEM_LIMIT_BYTES),
        name="dispatch",
    )(zstart, zlen, tail, dest_flat, xn4)


def _expert_kernel(item_e_ref, item_row_ref, item_n_ref, item_valid_ref,
                   xs_ref, w1_ref, b1_ref, w2_ref, b2_ref, ys_ref,
                   xbuf, abuf, stage, wperm, w2b, hcbuf, sem_s, *, rb, j1, tw, nt):
    del item_e_ref, item_valid_ref, xs_ref
    w = pl.program_id(0)
    j = pl.program_id(1)
    nrows = item_n_ref[w]
    row0 = pl.multiple_of(item_row_ref[w], rb)
    nrb = nrows // rb
    d = w2_ref.shape[1]
    tf = w2_ref.shape[0]

    def rows(i):
        return pl.ds(pl.multiple_of(i * rb, rb), rb)

    g0 = lax.shift_right_logical(row0, 3)
    gb = rb // SUBLANES

    def stage_copy(i, slot, to_hbm):
        hbm = ys_ref.at[pl.ds(g0 + i * gb, gb)]
        vmem = stage.at[slot]
        return pltpu.make_async_copy(vmem, hbm, sem_s.at[slot]) if to_hbm else pltpu.make_async_copy(
            hbm, vmem, sem_s.at[slot])

    def slot_of(jj, i):
        return (jj * nrb + i) & 1

    def dot1(i, x):
        hcbuf[slot_of(j, i)] = _dot(x, w1_ref[...].astype(BF16)) + b1_ref[...]

    lane = lax.broadcasted_iota(I32, (rb, LANES), 1)
    even = (lane & 1) == 0

    def tail1(jj, i):
        hc = hcbuf[slot_of(jj, i)]
        outs = []
        for q in range(tw // (2 * LANES)):
            c0 = hc[:, 2 * q * LANES:(2 * q + 1) * LANES]
            c1 = hc[:, (2 * q + 1) * LANES:(2 * q + 2) * LANES]
            glu = jnp.where(even, c0, pltpu.roll(c1, 1, 1))
            lin = jnp.where(even, pltpu.roll(c0, LANES - 1, 1), c1)
            glu = jnp.minimum(glu, SWIGLU_LIMIT)
            lin = jnp.clip(lin, -SWIGLU_LIMIT, SWIGLU_LIMIT)
            act = glu * (1.0 / (1.0 + jnp.exp(-SWIGLU_ALPHA * glu))) * (lin + 1.0)
            outs.append(act.astype(BF16))
        abuf[jj, rows(i), :] = jnp.concatenate(outs, axis=1)

    def permute_w2_tile():
        for s in range(d // LANES):
            for g in range(tf // LANES):
                top = w2_ref[g * LANES:g * LANES + LANES // 2, s * LANES:(s + 1) * LANES]
                bot = w2_ref[g * LANES + LANES // 2:(g + 1) * LANES, s * LANES:(s + 1) * LANES]
                wperm.at[s][pl.ds(g * LANES, LANES // 2, stride=2), :] = top
                wperm.at[s][pl.ds(g * LANES + 1, LANES // 2, stride=2), :] = bot
        k0 = pl.multiple_of(j * tf, tf)
        for s in range(d // LANES):
            w2b[pl.ds(k0, tf), s * LANES:(s + 1) * LANES] = wperm[s].astype(BF16)

    @pl.when((j == 0) & (nrows > 0))
    def _first_step():
        def fetch(i):
            stage_copy(i, i & 1, False).wait()

            @pl.when(i + 1 < nrb)
            def _():
                stage_copy(i + 1, (i + 1) & 1, False).start()

        def load_and_dot(i):
            x = jnp.concatenate([stage[i & 1, :, c].reshape(rb, LANES) for c in range(nt)], axis=1).astype(BF16)
            xbuf[rows(i), :] = x
            dot1(i, x)

        stage_copy(0, 0, False).start()
        fetch(0)
        load_and_dot(0)
        permute_w2_tile()

        def body(i, c):
            fetch(i)
            tail1(0, i - 1)
            load_and_dot(i)
            return c

        lax.fori_loop(1, nrb, body, 0)

    @pl.when((j > 0) & (j < j1) & (nrows > 0))
    def _next_steps():
        tail1(j - 1, nrb - 1)
        dot1(0, xbuf[rows(0), :])
        permute_w2_tile()

        def body(i, c):
            tail1(j, i - 1)
            dot1(i, xbuf[rows(i), :])
            return c

        lax.fori_loop(1, nrb, body, 0)

    @pl.when((j == j1) & (nrows > 0))
    def _last_step():
        tail1(j1 - 1, nrb - 1)

        def body(i, c):
            @pl.when(i >= 2)
            def _():
                stage_copy(i - 2, i & 1, True).wait()

            @pl.when(i >= 1)
            def _():
                stage_copy(i - 1, (i - 1) & 1, True).start()

            a = jnp.concatenate([abuf[jj, rows(i), :] for jj in range(j1)], axis=1)
            y = _dot(a, w2b[...]) + b2_ref[...]
            for c in range(nt):
                stage[i & 1, :, c] = y[:, c * LANES:(c + 1) * LANES].reshape(gb, SUBLANES, LANES)
            return c

        lax.fori_loop(0, nrb, body, 0)

        @pl.when(nrb >= 2)
        def _():
            stage_copy(nrb - 2, nrb & 1, True).wait()

        last = stage_copy(nrb - 1, (nrb - 1) & 1, True)
        last.start()
        last.wait()


def _experts(item_e, item_row, item_n, item_valid, xs, w1, b1, w2, b2, *, r_max, tw):
    n_exp, d, f2 = w1.shape
    f_dim = w2.shape[1]
    nt = xs.shape[1]
    j1 = f2 // tw
    tf = f_dim // j1
    assert tf == tw // 2 and tf % LANES == 0 and nt * LANES == d
    rb = EXPERT_BLOCK
    kern = functools.partial(_expert_kernel, rb=rb, j1=j1, tw=tw, nt=nt)

    def w_step(w, j, iv):
        return jnp.minimum(jnp.where(iv[w] == 1, j, j1), j1 - 1)

    def w1_map(w, j, ie, ir, inn, iv):
        return (ie[w], 0, w_step(w, j, iv))

    def w2_map(w, j, ie, ir, inn, iv):
        return (ie[w], w_step(w, j, iv), 0)

    def e_map(w, j, ie, ir, inn, iv):
        return (ie[w], 0, 0)

    grid_spec = pltpu.PrefetchScalarGridSpec(
        num_scalar_prefetch=4, grid=(item_e.shape[0], j1 + 1),
        in_specs=[
            pl.BlockSpec(memory_space=pl.ANY),
            pl.BlockSpec((None, d, tw), w1_map),
            pl.BlockSpec((None, 1, tw), w1_map),
            pl.BlockSpec((None, tf, d), w2_map),
            pl.BlockSpec((None, 1, d), e_map),
        ],
        out_specs=pl.BlockSpec(memory_space=pl.ANY),
        scratch_shapes=[
            pltpu.VMEM((r_max, d), BF16),
            pltpu.VMEM((j1, r_max, tf), BF16),
            pltpu.VMEM((2, rb // SUBLANES, nt, SUBLANES, LANES), F32),
            pltpu.VMEM((d // LANES, tf, LANES), F32),
            pltpu.VMEM((f_dim, d), BF16),
            pltpu.VMEM((2, rb, tw), F32),
            pltpu.SemaphoreType.DMA((2,)),
        ],
    )
    return pl.pallas_call(
        kern, grid_spec=grid_spec,
        out_shape=jax.ShapeDtypeStruct(xs.shape, F32),
        input_output_aliases={4: 0},
        compiler_params=pltpu.CompilerParams(dimension_semantics=("arbitrary", "arbitrary"),
                                             vmem_limit_bytes=VMEM_LIMIT_BYTES),
        name="experts",
    )(item_e, item_row, item_n, item_valid, xs, w1, b1.reshape(n_exp, 1, f2), w2, b2.reshape(n_exp, 1, d))


def _combine_kernel(dest_ref, tokmeta_ref, h1_ref, fg_ref, ys_ref, o_ref, buf, sem, *, tc, nt):
    def body(r, c):
        for k in range(TOP_K):
            pltpu.make_async_copy(_row_of(ys_ref, dest_ref[r * TOP_K + k]), _row_of(buf.at[k], r), sem).start(
                priority=k % 2)
        return c

    lax.fori_loop(0, tc, body, 0, unroll=8)
    for k in range(TOP_K):
        pltpu.make_async_copy(buf.at[k], buf.at[k], sem).wait()
    tm = tokmeta_ref[...]
    cols = [h1_ref[:, c * LANES:(c + 1) * LANES] for c in range(nt)]
    for k in range(TOP_K):
        gate = tm[:, 2 * TOP_K + k:2 * TOP_K + k + 1]
        cols = [a + gate * buf[k, :, c].reshape(tc, LANES) for c, a in enumerate(cols)]
    o_ref[...] = _rms(jnp.concatenate(cols, axis=1), fg_ref[...])


def _combine(dest_flat, tokmeta, h1, fg, ys, *, tc):
    t, d = h1.shape
    nt = ys.shape[1]
    kern = functools.partial(_combine_kernel, tc=tc, nt=nt)
    row = lambda i: (i, 0)
    return pl.pallas_call(
        kern, grid=(t // tc,),
        in_specs=[
            pl.BlockSpec((tc * TOP_K,), lambda i: (i,), memory_space=pltpu.SMEM),
            pl.BlockSpec((tc, LANES), row), pl.BlockSpec((tc, d), row), _const_spec(fg.shape),
            pl.BlockSpec(memory_space=pl.ANY),
        ],
        out_specs=pl.BlockSpec((tc, d), row),
        out_shape=jax.ShapeDtypeStruct((t, d), F32),
        scratch_shapes=[pltpu.VMEM((TOP_K, tc // SUBLANES, nt, SUBLANES, LANES), F32), pltpu.SemaphoreType.DMA],
        compiler_params=pltpu.CompilerParams(dimension_semantics=("arbitrary",),
                                             vmem_limit_bytes=VMEM_LIMIT_BYTES),
        name="combine",
    )(dest_flat, tokmeta, h1, fg, ys)


def _rot_cols(w):
    h = QK_ROPE // 2
    return jnp.concatenate([-w[..., h:], w[..., :h]], axis=-1)


def _tile_rows(n, cap):
    t = min(n, cap)
    assert n % t == 0, (n, cap)
    return t


def kernel(x, meta_tokens, attn_norm_g, w_in, q_norm_g, w_uq, kv_norm_g, w_ukv, pool_w, pool_scale, w_o,
           ffn_norm_g, w_router, b_router, w1, b1, w2, b2, final_norm_g):
    nb, s_len, d = x.shape
    n_meta = meta_tokens.shape[0]
    assert w_in.shape[0] == 1, "one layer"
    assert n_meta == HALO and max(POOL_WINDOWS) - 1 <= HALO
    pw = pool_scale.shape[1]
    q_lora = q_norm_g.shape[1]
    kv_lora = kv_norm_g.shape[1]
    n_heads = w_uq.shape[2] // (QK_NOPE + QK_ROPE)
    n_exp = w_router.shape[2]
    f_dim = w2.shape[2]
    t = nb * s_len
    assert s_len % CHUNK == 0 and n_exp <= LANES and pw // len(POOL_WINDOWS) % LANES == 0

    win = w_in[0]
    o = pw + q_lora + kv_lora
    w_kr = win[:, o:o + QK_ROPE]
    zc = jnp.zeros((d, LANES - QK_ROPE), F32)
    win_b = jnp.concatenate([win[:, :o], w_kr, zc, _rot_cols(w_kr), zc], axis=1).astype(BF16)
    wq3 = w_uq[0].reshape(q_lora, n_heads, QK_NOPE + QK_ROPE)
    zq = jnp.zeros((q_lora, n_heads, LANES - QK_ROPE), F32)
    wq_b = jnp.concatenate([wq3, zq], axis=2).reshape(q_lora, n_heads * HEAD_PAD).astype(BF16)
    wqr_b = jnp.concatenate([_rot_cols(wq3[:, :, QK_NOPE:]), zq], axis=2).reshape(
        q_lora, n_heads * LANES).astype(BF16)
    wkv3 = w_ukv[0].reshape(kv_lora, n_heads, QK_NOPE + V_DIM)
    wk_b = wkv3[:, :, :QK_NOPE].reshape(kv_lora, n_heads * QK_NOPE).astype(BF16)
    wv_b = wkv3[:, :, QK_NOPE:].reshape(kv_lora, n_heads * V_DIM).astype(BF16)
    front_w = (attn_norm_g, win_b, pool_w[0].astype(BF16), pool_scale, q_norm_g, wq_b, wqr_b, kv_norm_g,
               wk_b, wv_b)
    woa = w_o[0, :pw].astype(BF16)
    wob = w_o[0, pw:].astype(BF16)
    wr_b = jnp.pad(w_router[0], ((0, 0), (0, LANES - n_exp))).astype(BF16)
    br = jnp.pad(b_router, ((0, 0), (0, LANES - n_exp)), constant_values=NEG_BIG)

    pos = jnp.arange(n_meta + s_len, dtype=F32)
    inv_freq = 1.0 / (ROPE_BASE ** (jnp.arange(0, QK_ROPE, 2, dtype=F32) / QK_ROPE))
    ang = pos[:, None] * inv_freq[None, :]
    ones = jnp.ones((n_meta + s_len, LANES - QK_ROPE), F32)
    cs = jnp.concatenate([jnp.cos(ang), jnp.cos(ang), ones], axis=1)
    sn = jnp.concatenate([jnp.sin(ang), jnp.sin(ang), 0.0 * ones], axis=1)

    zero_halo = jnp.zeros((HALO, pw), F32)
    _, _, k_meta, v_meta, p_meta = _front(meta_tokens[None], zero_halo, cs[:n_meta], sn[:n_meta], front_w,
                                          ts=n_meta)
    ts = _tile_rows(s_len, 256)
    y_pool, q, k, v, _ = _front(x, p_meta[0, 0], cs[n_meta:], sn[n_meta:], front_w, ts=ts)

    km = jnp.pad(k_meta[0], ((0, LANES - n_meta), (0, 0)))
    vm = jnp.pad(v_meta[0], ((0, LANES - n_meta), (0, 0)))
    y_mla = _attn(q, k, v, km, vm, tq=_tile_rows(s_len, 512), n_meta=n_meta)

    tm = _tile_rows(t, 256)
    h1, xn4, tokmeta, counts = _mid(y_pool.reshape(t, pw), y_mla.reshape(t, -1), x.reshape(t, d), woa, wob,
                                    ffn_norm_g, wr_b, br, tm=tm)

    r_max = 10 * EXPERT_BLOCK
    cnt = counts[0, :n_exp].astype(I32)
    padded = (cnt + EXPERT_BLOCK - 1) // EXPERT_BLOCK * EXPERT_BLOCK
    pad_end = jnp.cumsum(padded)
    pad_start = pad_end - padded
    n_assign = t * TOP_K
    p_rows = -(-(n_assign + n_exp * (EXPERT_BLOCK - 1)) // EXPERT_BLOCK) * EXPERT_BLOCK
    idx = tokmeta[:, 0:TOP_K].astype(I32)
    rank = tokmeta[:, TOP_K:2 * TOP_K].astype(I32)
    start_of = jnp.sum(jnp.where(idx[:, :, None] == jnp.arange(n_exp, dtype=I32), pad_start, 0), axis=-1)
    dest = (start_of + rank).reshape(-1)

    n_items = n_exp + p_rows // r_max
    per_e = (padded + r_max - 1) // r_max
    item_end = jnp.cumsum(per_e)
    total = item_end[-1]
    wi = jnp.arange(n_items, dtype=I32)
    valid = wi < total
    wc = jnp.minimum(wi, total - 1)
    ie = jnp.minimum(jnp.sum((item_end[None, :] <= wc[:, None]).astype(I32), axis=1), n_exp - 1)
    local = wc - (item_end[ie] - per_e[ie])
    item_row = jnp.where(valid, pad_start[ie] + local * r_max, 0).astype(I32)
    item_n = jnp.where(valid, jnp.clip(padded[ie] - local * r_max, 0, r_max), 0).astype(I32)

    zstart = (pad_start + cnt).astype(I32)
    zlen = (padded - cnt).astype(I32)
    tail = jnp.stack([pad_end[-1] // SUBLANES, (p_rows - pad_end[-1]) // (SUBLANES * ZERO_GROUPS)]).astype(I32)
    xs = _dispatch(zstart, zlen, tail, dest, xn4, p_rows, td=_tile_rows(t, 256))
    ys = _experts(ie, item_row, item_n, valid.astype(I32), xs, w1[0], b1[0], w2[0], b2[0],
                  r_max=r_max, tw=min(512, 2 * f_dim))
    out = _combine(dest, tokmeta, h1, final_norm_g.reshape(1, d), ys, tc=_tile_rows(t, 256))
    return out.reshape(nb, s_len, d)
```

```python
import functools

import jax
import jax.numpy as jnp
from jax import lax
from jax.experimental import pallas as pl
from jax.experimental.pallas import tpu as pltpu

F32 = jnp.float32
BF16 = jnp.bfloat16
I32 = jnp.int32

CHUNK = 64
POOL_WINDOWS = (2, 4, 8, 16)
V_DIM = 128
QK_NOPE = 128
QK_ROPE = 64
ROPE_BASE = 10000.0
TOP_K = 4
SWIGLU_LIMIT = 7.0
SWIGLU_ALPHA = 1.702
EPS = 1e-5
EXPERT_BLOCK = 512

LANES = 128
SUBLANES = 8
HEAD_PAD = 2 * LANES
VMEM_LIMIT_BYTES = 56 * 1024 * 1024

HALO = 16
NEG_BIG = -1e30


def _rms(x, g):
    ms = jnp.mean(x * x, axis=-1, keepdims=True)
    return x * lax.rsqrt(ms + EPS) * g


def _dot(a, b):
    return jnp.dot(a, b, preferred_element_type=F32)


def _dot_nt(a, b):
    return lax.dot_general(a, b, (((1,), (1,)), ((), ())), preferred_element_type=F32)


def _const_spec(shape):
    nd = len(shape)
    return pl.BlockSpec(shape, lambda *_: (0,) * nd)


def _row_of(ref, r):
    if isinstance(r, int):
        return ref.at[r // SUBLANES, :, r % SUBLANES, :]
    return ref.at[lax.shift_right_logical(r, 3), :, r & (SUBLANES - 1), :]


def _front_kernel(x_ref, mpool_ref, cs_ref, sn_ref, ag_ref, win_ref, pw_ref, ps_ref, qg_ref, wq_ref,
                  wqr_ref, kg_ref, wk_ref, wv_ref,
                  ypool_ref, q_ref, k_ref, v_ref, ptail_ref, ext_ref, *, ts, pool_w, q_lora, kv_lora,
                  n_heads):
    st = pl.program_id(1)
    hn = _rms(x_ref[0], ag_ref[...]).astype(BF16)
    proj = _dot(hn, win_ref[...])
    pool_in = proj[:, :pool_w]

    @pl.when(st == 0)
    def _():
        ext_ref[0:HALO, :] = mpool_ref[...]

    ext_ref[HALO:HALO + ts, :] = pool_in
    gw = pool_w // len(POOL_WINDOWS)
    for g, w in enumerate(POOL_WINDOWS):
        c0 = g * gw
        u = pool_in[:, c0:c0 + gw]
        s = u
        for k in range(1, w):
            s = s + ext_ref[HALO - k:HALO - k + ts, c0:c0 + gw]
        d = (s * (1.0 / w) - u).astype(BF16)
        y = _dot(d, pw_ref[g]) * ps_ref[:, c0:c0 + gw]
        ypool_ref[0, :, c0:c0 + gw] = y.astype(BF16)
    tail = pool_in[ts - HALO:ts, :]
    ext_ref[0:HALO, :] = tail
    ptail_ref[0, 0] = tail

    o = pool_w
    q_c = proj[:, o:o + q_lora]
    o += q_lora
    kv_c = proj[:, o:o + kv_lora]
    o += kv_lora
    kr = proj[:, o:o + LANES]
    kr_rot = proj[:, o + LANES:o + 2 * LANES]
    cs = cs_ref[...]
    sn = sn_ref[...]
    krope = (kr * cs + kr_rot * sn).astype(BF16)
    qn = _rms(q_c, qg_ref[...]).astype(BF16)
    qm = _dot(qn, wq_ref[...])
    qr = _dot(qn, wqr_ref[...])
    kvn = _rms(kv_c, kg_ref[...]).astype(BF16)
    kn = _dot(kvn, wk_ref[...])
    v_ref[0] = _dot(kvn, wv_ref[...]).astype(BF16)
    for h in range(n_heads):
        a = h * HEAD_PAD
        b = h * LANES
        q_ref[0, :, a:a + LANES] = qm[:, a:a + LANES].astype(BF16)
        q_ref[0, :, a + LANES:a + HEAD_PAD] = (
            qm[:, a + LANES:a + HEAD_PAD] * cs + qr[:, b:b + LANES] * sn).astype(BF16)
        k_ref[0, :, a:a + LANES] = kn[:, b:b + LANES].astype(BF16)
        k_ref[0, :, a + LANES:a + HEAD_PAD] = krope


def _front(x3, mpool, cs, sn, wts, *, ts):
    nb, s_len, d = x3.shape
    (ag, win, pw, ps, qg, wq, wqr, kg, wk, wv) = wts
    pool_w = ps.shape[1]
    q_lora = qg.shape[1]
    kv_lora = kg.shape[1]
    n_heads = wk.shape[1] // LANES
    n_st = s_len // ts
    kern = functools.partial(_front_kernel, ts=ts, pool_w=pool_w, q_lora=q_lora, kv_lora=kv_lora,
                             n_heads=n_heads)
    row = lambda b, s: (b, s, 0)
    in_specs = [
        pl.BlockSpec((1, ts, d), row),
        _const_spec(mpool.shape),
        pl.BlockSpec((ts, LANES), lambda b, s: (s, 0)),
        pl.BlockSpec((ts, LANES), lambda b, s: (s, 0)),
    ] + [_const_spec(w.shape) for w in wts]
    out_shape = (
        jax.ShapeDtypeStruct((nb, s_len, pool_w), BF16),
        jax.ShapeDtypeStruct((nb, s_len, n_heads * HEAD_PAD), BF16),
        jax.ShapeDtypeStruct((nb, s_len, n_heads * HEAD_PAD), BF16),
        jax.ShapeDtypeStruct((nb, s_len, n_heads * V_DIM), BF16),
        jax.ShapeDtypeStruct((nb, n_st, HALO, pool_w), F32),
    )
    out_specs = (
        pl.BlockSpec((1, ts, pool_w), row),
        pl.BlockSpec((1, ts, n_heads * HEAD_PAD), row),
        pl.BlockSpec((1, ts, n_heads * HEAD_PAD), row),
        pl.BlockSpec((1, ts, n_heads * V_DIM), row),
        pl.BlockSpec((1, 1, HALO, pool_w), lambda b, s: (b, s, 0, 0)),
    )
    return pl.pallas_call(
        kern, grid=(nb, n_st), in_specs=in_specs, out_specs=out_specs, out_shape=out_shape,
        scratch_shapes=[pltpu.VMEM((HALO + ts, pool_w), F32)],
        compiler_params=pltpu.CompilerParams(dimension_semantics=("arbitrary", "arbitrary"),
                                             vmem_limit_bytes=VMEM_LIMIT_BYTES),
        name="front",
    )(x3, mpool, cs, sn, *wts)


def _attn_kernel(q_ref, k_ref, v_ref, km_ref, vm_ref, o_ref, m_ref, l_ref, acc_ref, *, tq, n_meta, n_heads,
                 scale):
    qi = pl.program_id(1)
    c2 = scale * 1.4426950408889634

    def update(h, s, vb, first):
        s_max = jnp.max(s, axis=1, keepdims=True)
        if first:
            m_new = jnp.broadcast_to(s_max, (tq, LANES))
        else:
            m_old = m_ref[h]
            m_new = jnp.maximum(m_old, s_max)
            alpha = jnp.exp2((m_old - m_new) * c2)
        p = jnp.exp2((s - jnp.concatenate([m_new] * (s.shape[1] // LANES), axis=1)) * c2)
        v1 = jnp.concatenate([vb, jnp.ones(vb.shape, BF16)], axis=1)
        pv = _dot(p.astype(BF16), v1)
        if first:
            l_ref[h] = pv[:, V_DIM:]
            acc_ref[h] = pv[:, :V_DIM]
        else:
            l_ref[h] = alpha * l_ref[h] + pv[:, V_DIM:]
            acc_ref[h] = alpha * acc_ref[h] + pv[:, :V_DIM]
        m_ref[h] = m_new

    def q_of(h):
        return q_ref[0, :, h * HEAD_PAD:(h + 1) * HEAD_PAD]

    r_diag = pl.multiple_of(qi * tq, tq)
    rc = lax.broadcasted_iota(I32, (tq, tq), 0) // CHUNK
    cc = lax.broadcasted_iota(I32, (tq, tq), 1) // CHUNK
    vis = jnp.concatenate([cc <= rc, lax.broadcasted_iota(I32, (tq, LANES), 1) < n_meta], axis=1)
    for h in range(n_heads):
        kd = jnp.concatenate([k_ref[0, pl.ds(r_diag, tq), h * HEAD_PAD:(h + 1) * HEAD_PAD],
                              km_ref[:, h * HEAD_PAD:(h + 1) * HEAD_PAD]], axis=0)
        vd = jnp.concatenate([v_ref[0, pl.ds(r_diag, tq), h * V_DIM:(h + 1) * V_DIM],
                              vm_ref[:, h * V_DIM:(h + 1) * V_DIM]], axis=0)
        s = jnp.where(vis, _dot_nt(q_of(h), kd), -jnp.inf)
        update(h, s, vd, True)

    def body(j, c):
        r0 = pl.multiple_of(j * tq, tq)
        for h in range(n_heads):
            s = _dot_nt(q_of(h), k_ref[0, pl.ds(r0, tq), h * HEAD_PAD:(h + 1) * HEAD_PAD])
            update(h, s, v_ref[0, pl.ds(r0, tq), h * V_DIM:(h + 1) * V_DIM], False)
        return c

    lax.fori_loop(0, qi, body, 0)
    for h in range(n_heads):
        o_ref[0, :, h * V_DIM:(h + 1) * V_DIM] = (acc_ref[h] / l_ref[h]).astype(BF16)


def _attn(q, k, v, km, vm, *, tq, n_meta):
    nb, s_len, hw = q.shape
    n_heads = hw // HEAD_PAD
    kern = functools.partial(_attn_kernel, tq=tq, n_meta=n_meta, n_heads=n_heads,
                             scale=float((QK_NOPE + QK_ROPE) ** -0.5))
    return pl.pallas_call(
        kern, grid=(nb, s_len // tq),
        in_specs=[
            pl.BlockSpec((1, tq, hw), lambda b, i: (b, i, 0)),
            pl.BlockSpec((1, s_len, hw), lambda b, i: (b, 0, 0)),
            pl.BlockSpec((1, s_len, n_heads * V_DIM), lambda b, i: (b, 0, 0)),
            _const_spec(km.shape),
            _const_spec(vm.shape),
        ],
        out_specs=pl.BlockSpec((1, tq, n_heads * V_DIM), lambda b, i: (b, i, 0)),
        out_shape=jax.ShapeDtypeStruct((nb, s_len, n_heads * V_DIM), BF16),
        scratch_shapes=[pltpu.VMEM((n_heads, tq, LANES), F32)] * 3,
        compiler_params=pltpu.CompilerParams(dimension_semantics=("arbitrary", "arbitrary"),
                                             vmem_limit_bytes=VMEM_LIMIT_BYTES),
        name="attn",
    )(q, k, v, km, vm)


def _mid_kernel(yp_ref, ym_ref, x_ref, woa_ref, wob_ref, fg_ref, wr_ref, br_ref,
                h1_ref, xn4_ref, tokmeta_ref, counts_ref, run_ref, *, tm):
    i = pl.program_id(0)

    @pl.when(i == 0)
    def _():
        run_ref[...] = jnp.zeros_like(run_ref)

    h1 = x_ref[...] + _dot(yp_ref[...], woa_ref[...]) + _dot(ym_ref[...], wob_ref[...])
    h1_ref[...] = h1
    xn = _rms(h1, fg_ref[...])
    for c in range(xn4_ref.shape[1]):
        xn4_ref[:, c] = xn[:, c * LANES:(c + 1) * LANES].reshape(tm // SUBLANES, SUBLANES, LANES)
    xb = xn.astype(BF16)

    logits = _dot(xb, wr_ref[...]) + br_ref[...]
    lane = lax.broadcasted_iota(I32, logits.shape, 1).astype(F32)
    work = logits
    idxs, vals = [], []
    for _ in range(TOP_K):
        mx = jnp.max(work, axis=1, keepdims=True)
        ix = jnp.min(jnp.where(work == mx, lane, float(LANES)), axis=1, keepdims=True)
        idxs.append(ix)
        vals.append(mx)
        work = jnp.where(lane == ix, -jnp.inf, work)
    es = [jnp.exp(vv - vals[0]) for vv in vals]
    den = es[0]
    for e in es[1:]:
        den = den + e
    hot = [jnp.where(lane == ix, 1.0, 0.0) for ix in idxs]
    cnt = hot[0]
    for hh in hot[1:]:
        cnt = cnt + hh
    rr = lax.broadcasted_iota(I32, (tm, tm), 0)
    cc = lax.broadcasted_iota(I32, (tm, tm), 1)
    ltri = jnp.where(rr > cc, 1.0, 0.0).astype(BF16)
    base = run_ref[0:1, :] + _dot(ltri, cnt.astype(BF16))
    out = jnp.zeros(logits.shape, F32)
    for k in range(TOP_K):
        rank = jnp.sum(hot[k] * base, axis=1, keepdims=True)
        out = jnp.where(lane == float(k), idxs[k], out)
        out = jnp.where(lane == float(TOP_K + k), rank, out)
        out = jnp.where(lane == float(2 * TOP_K + k), es[k] / den, out)
    tokmeta_ref[...] = out
    run = run_ref[...] + jnp.sum(cnt, axis=0, keepdims=True)
    run_ref[...] = run
    counts_ref[...] = run


def _mid(yp, ym, x2, woa, wob, fg, wr, br, *, tm):
    t, d = x2.shape
    pw = yp.shape[1]
    mw = ym.shape[1]
    nt = d // LANES
    kern = functools.partial(_mid_kernel, tm=tm)
    row = lambda i: (i, 0)
    return pl.pallas_call(
        kern, grid=(t // tm,),
        in_specs=[
            pl.BlockSpec((tm, pw), row), pl.BlockSpec((tm, mw), row), pl.BlockSpec((tm, d), row),
            _const_spec(woa.shape), _const_spec(wob.shape), _const_spec(fg.shape),
            _const_spec(wr.shape), _const_spec(br.shape),
        ],
        out_specs=(
            pl.BlockSpec((tm, d), row), pl.BlockSpec((tm // SUBLANES, nt, SUBLANES, LANES), lambda i: (i, 0, 0, 0)),
            pl.BlockSpec((tm, LANES), row),
            pl.BlockSpec((8, LANES), lambda i: (0, 0)),
        ),
        out_shape=(
            jax.ShapeDtypeStruct((t, d), F32), jax.ShapeDtypeStruct((t // SUBLANES, nt, SUBLANES, LANES), F32),
            jax.ShapeDtypeStruct((t, LANES), F32), jax.ShapeDtypeStruct((8, LANES), F32),
        ),
        scratch_shapes=[pltpu.VMEM((8, LANES), F32)],
        compiler_params=pltpu.CompilerParams(dimension_semantics=("arbitrary",),
                                             vmem_limit_bytes=VMEM_LIMIT_BYTES),
        name="mid",
    )(yp, ym, x2, woa, wob, fg, wr, br)


ZERO_GROUPS = EXPERT_BLOCK // (2 * SUBLANES)


def _dispatch_kernel(zstart_ref, zlen_ref, tail_ref, dest_ref, xn4_ref, xs_ref, zbuf, sem, sem_z, *, td, n_exp):
    def body(r, c):
        for k in range(TOP_K):
            pltpu.make_async_copy(_row_of(xn4_ref, r), _row_of(xs_ref, dest_ref[r * TOP_K + k]), sem).start(
                priority=k % 2)
        return c

    lax.fori_loop(0, td, body, 0, unroll=8)
    for _ in range(TOP_K):
        pltpu.make_async_copy(xn4_ref, xn4_ref, sem).wait()

    @pl.when(pl.program_id(0) == pl.num_programs(0) - 1)
    def _zero_fill():
        zbuf[...] = jnp.zeros_like(zbuf)
        tail0 = tail_ref[0]
        n_tail = tail_ref[1]

        def pad_copies(e, wait):
            zs = zstart_ref[e]
            zl = zlen_ref[e]
            head = jnp.minimum((-zs) & (SUBLANES - 1), zl)
            for h in range(SUBLANES - 1):
                @pl.when(h < head)
                def _(h=h):
                    cp = pltpu.make_async_copy(_row_of(zbuf, 0), _row_of(xs_ref, zs + h), sem_z)
                    cp.wait() if wait else cp.start()
            g0 = lax.shift_right_logical(zs + head, 3)
            ng = lax.shift_right_logical(zl - head, 3)
            v = ZERO_GROUPS
            while v >= 1:
                @pl.when((ng & v) != 0)
                def _(v=v):
                    off = g0 + (ng & (-2 * v))
                    cp = pltpu.make_async_copy(zbuf.at[pl.ds(0, v)], xs_ref.at[pl.ds(off, v)], sem_z)
                    cp.wait() if wait else cp.start()
                v //= 2

        def tail_copy(i, wait):
            cp = pltpu.make_async_copy(zbuf, xs_ref.at[pl.ds(tail0 + i * ZERO_GROUPS, ZERO_GROUPS)], sem_z)
            cp.wait() if wait else cp.start()

        for wait in (False, True):
            lax.fori_loop(0, n_exp, lambda e, c, wait=wait: (pad_copies(e, wait), c)[1], 0)
            lax.fori_loop(0, n_tail, lambda i, c, wait=wait: (tail_copy(i, wait), c)[1], 0)


def _dispatch(zstart, zlen, tail, dest_flat, xn4, p_rows, *, td):
    tg, nt, _, _ = xn4.shape
    n_exp = zstart.shape[0]
    kern = functools.partial(_dispatch_kernel, td=td, n_exp=n_exp)
    grid_spec = pltpu.PrefetchScalarGridSpec(
        num_scalar_prefetch=3, grid=(tg * SUBLANES // td,),
        in_specs=[
            pl.BlockSpec((td * TOP_K,), lambda i, *_: (i,), memory_space=pltpu.SMEM),
            pl.BlockSpec((td // SUBLANES, nt, SUBLANES, LANES), lambda i, *_: (i, 0, 0, 0)),
        ],
        out_specs=pl.BlockSpec(memory_space=pl.ANY),
        scratch_shapes=[pltpu.VMEM((ZERO_GROUPS, nt, SUBLANES, LANES), F32), pltpu.SemaphoreType.DMA,
                        pltpu.SemaphoreType.DMA],
    )
    return pl.pallas_call(
        kern, grid_spec=grid_spec,
        out_shape=jax.ShapeDtypeStruct((p_rows // SUBLANES, nt, SUBLANES, LANES), F32),
        compiler_params=pltpu.CompilerParams(dimension_semantics=("arbitrary",),
                                             vmem_limit_bytes=VMEM_LIMIT_BYTES),
        name="dispatch",
    )(zstart, zlen, tail, dest_flat, xn4)


def _expert_kernel(item_e_ref, item_row_ref, item_n_ref, item_valid_ref,
                   xs_ref, w1_ref, b1_ref, w2_ref, b2_ref, ys_ref,
                   xbuf, abuf, stage, wperm, w2b, hcbuf, sem_s, *, rb, j1, tw, nt):
    del item_e_ref, item_valid_ref, xs_ref
    w = pl.program_id(0)
    j = pl.program_id(1)
    nrows = item_n_ref[w]
    row0 = pl.multiple_of(item_row_ref[w], rb)
    nrb = nrows // rb
    d = w2_ref.shape[1]
    tf = w2_ref.shape[0]

    def rows(i):
        return pl.ds(pl.multiple_of(i * rb, rb), rb)

    g0 = lax.shift_right_logical(row0, 3)
    gb = rb // SUBLANES

    def stage_copy(i, slot, to_hbm):
        hbm = ys_ref.at[pl.ds(g0 + i * gb, gb)]
        vmem = stage.at[slot]
        return pltpu.make_async_copy(vmem, hbm, sem_s.at[slot]) if to_hbm else pltpu.make_async_copy(
            hbm, vmem, sem_s.at[slot])

    def slot_of(jj, i):
        return (jj * nrb + i) & 1

    def dot1(i, x):
        hcbuf[slot_of(j, i)] = _dot(x, w1_ref[...].astype(BF16)) + b1_ref[...]

    lane = lax.broadcasted_iota(I32, (rb, LANES), 1)
    even = (lane & 1) == 0

    def tail1(jj, i):
        hc = hcbuf[slot_of(jj, i)]
        outs = []
        for q in range(tw // (2 * LANES)):
            c0 = hc[:, 2 * q * LANES:(2 * q + 1) * LANES]
            c1 = hc[:, (2 * q + 1) * LANES:(2 * q + 2) * LANES]
            glu = jnp.where(even, c0, pltpu.roll(c1, 1, 1))
            lin = jnp.where(even, pltpu.roll(c0, LANES - 1, 1), c1)
            glu = jnp.minimum(glu, SWIGLU_LIMIT)
            lin = jnp.clip(lin, -SWIGLU_LIMIT, SWIGLU_LIMIT)
            act = glu * (1.0 / (1.0 + jnp.exp(-SWIGLU_ALPHA * glu))) * (lin + 1.0)
            outs.append(act.astype(BF16))
        abuf[jj, rows(i), :] = jnp.concatenate(outs, axis=1)

    def permute_w2_tile():
        for s in range(d // LANES):
            for g in range(tf // LANES):
                top = w2_ref[g * LANES:g * LANES + LANES // 2, s * LANES:(s + 1) * LANES]
                bot = w2_ref[g * LANES + LANES // 2:(g + 1) * LANES, s * LANES:(s + 1) * LANES]
                wperm.at[s][pl.ds(g * LANES, LANES // 2, stride=2), :] = top
                wperm.at[s][pl.ds(g * LANES + 1, LANES // 2, stride=2), :] = bot
        k0 = pl.multiple_of(j * tf, tf)
        for s in range(d // LANES):
            w2b[pl.ds(k0, tf), s * LANES:(s + 1) * LANES] = wperm[s].astype(BF16)

    @pl.when((j == 0) & (nrows > 0))
    def _first_step():
        def fetch(i):
            stage_copy(i, i & 1, False).wait()

            @pl.when(i + 1 < nrb)
            def _():
                stage_copy(i + 1, (i + 1) & 1, False).start()

        def load_and_dot(i):
            x = jnp.concatenate([stage[i & 1, :, c].reshape(rb, LANES) for c in range(nt)], axis=1).astype(BF16)
            xbuf[rows(i), :] = x
            dot1(i, x)

        stage_copy(0, 0, False).start()
        fetch(0)
        load_and_dot(0)
        permute_w2_tile()

        def body(i, c):
            fetch(i)
            tail1(0, i - 1)
            load_and_dot(i)
            return c

        lax.fori_loop(1, nrb, body, 0)

    @pl.when((j > 0) & (j < j1) & (nrows > 0))
    def _next_steps():
        tail1(j - 1, nrb - 1)
        dot1(0, xbuf[rows(0), :])
        permute_w2_tile()

        def body(i, c):
            tail1(j, i - 1)
            dot1(i, xbuf[rows(i), :])
            return c

        lax.fori_loop(1, nrb, body, 0)

    @pl.when((j == j1) & (nrows > 0))
    def _last_step():
        tail1(j1 - 1, nrb - 1)

        def body(i, c):
            @pl.when(i >= 2)
            def _():
                stage_copy(i - 2, i & 1, True).wait()

            @pl.when(i >= 1)
            def _():
                stage_copy(i - 1, (i - 1) & 1, True).start()

            a = jnp.concatenate([abuf[jj, rows(i), :] for jj in range(j1)], axis=1)
            y = _dot(a, w2b[...]) + b2_ref[...]
            for c in range(nt):
                stage[i & 1, :, c] = y[:, c * LANES:(c + 1) * LANES].reshape(gb, SUBLANES, LANES)
            return c

        lax.fori_loop(0, nrb, body, 0)

        @pl.when(nrb >= 2)
        def _():
            stage_copy(nrb - 2, nrb & 1, True).wait()

        last = stage_copy(nrb - 1, (nrb - 1) & 1, True)
        last.start()
        last.wait()


def _experts(item_e, item_row, item_n, item_valid, xs, w1, b1, w2, b2, *, r_max, tw):
    n_exp, d, f2 = w1.shape
    f_dim = w2.shape[1]
    nt = xs.shape[1]
    j1 = f2 // tw
    tf = f_dim // j1
    assert tf == tw // 2 and tf % LANES == 0 and nt * LANES == d
    rb = EXPERT_BLOCK
    kern = functools.partial(_expert_kernel, rb=rb, j1=j1, tw=tw, nt=nt)

    def w_step(w, j, iv):
        return jnp.minimum(jnp.where(iv[w] == 1, j, j1), j1 - 1)

    def w1_map(w, j, ie, ir, inn, iv):
        return (ie[w], 0, w_step(w, j, iv))

    def w2_map(w, j, ie, ir, inn, iv):
        return (ie[w], w_step(w, j, iv), 0)

    def e_map(w, j, ie, ir, inn, iv):
        return (ie[w], 0, 0)

    grid_spec = pltpu.PrefetchScalarGridSpec(
        num_scalar_prefetch=4, grid=(item_e.shape[0], j1 + 1),
        in_specs=[
            pl.BlockSpec(memory_space=pl.ANY),
            pl.BlockSpec((None, d, tw), w1_map),
            pl.BlockSpec((None, 1, tw), w1_map),
            pl.BlockSpec((None, tf, d), w2_map),
            pl.BlockSpec((None, 1, d), e_map),
        ],
        out_specs=pl.BlockSpec(memory_space=pl.ANY),
        scratch_shapes=[
            pltpu.VMEM((r_max, d), BF16),
            pltpu.VMEM((j1, r_max, tf), BF16),
            pltpu.VMEM((2, rb // SUBLANES, nt, SUBLANES, LANES), F32),
            pltpu.VMEM((d // LANES, tf, LANES), F32),
            pltpu.VMEM((f_dim, d), BF16),
            pltpu.VMEM((2, rb, tw), F32),
            pltpu.SemaphoreType.DMA((2,)),
        ],
    )
    return pl.pallas_call(
        kern, grid_spec=grid_spec,
        out_shape=jax.ShapeDtypeStruct(xs.shape, F32),
        input_output_aliases={4: 0},
        compiler_params=pltpu.CompilerParams(dimension_semantics=("arbitrary", "arbitrary"),
                                             vmem_limit_bytes=VMEM_LIMIT_BYTES),
        name="experts",
    )(item_e, item_row, item_n, item_valid, xs, w1, b1.reshape(n_exp, 1, f2), w2, b2.reshape(n_exp, 1, d))


def _combine_kernel(dest_ref, tokmeta_ref, h1_ref, fg_ref, ys_ref, o_ref, buf, sem, *, tc, nt):
    def body(r, c):
        for k in range(TOP_K):
            pltpu.make_async_copy(_row_of(ys_ref, dest_ref[r * TOP_K + k]), _row_of(buf.at[k], r), sem).start(
                priority=k % 2)
        return c

    lax.fori_loop(0, tc, body, 0, unroll=8)
    for k in range(TOP_K):
        pltpu.make_async_copy(buf.at[k], buf.at[k], sem).wait()
    tm = tokmeta_ref[...]
    cols = [h1_ref[:, c * LANES:(c + 1) * LANES] for c in range(nt)]
    for k in range(TOP_K):
        gate = tm[:, 2 * TOP_K + k:2 * TOP_K + k + 1]
        cols = [a + gate * buf[k, :, c].reshape(tc, LANES) for c, a in enumerate(cols)]
    o_ref[...] = _rms(jnp.concatenate(cols, axis=1), fg_ref[...])


def _combine(dest_flat, tokmeta, h1, fg, ys, *, tc):
    t, d = h1.shape
    nt = ys.shape[1]
    kern = functools.partial(_combine_kernel, tc=tc, nt=nt)
    row = lambda i: (i, 0)
    return pl.pallas_call(
        kern, grid=(t // tc,),
        in_specs=[
            pl.BlockSpec((tc * TOP_K,), lambda i: (i,), memory_space=pltpu.SMEM),
            pl.BlockSpec((tc, LANES), row), pl.BlockSpec((tc, d), row), _const_spec(fg.shape),
            pl.BlockSpec(memory_space=pl.ANY),
        ],
        out_specs=pl.BlockSpec((tc, d), row),
        out_shape=jax.ShapeDtypeStruct((t, d), F32),
        scratch_shapes=[pltpu.VMEM((TOP_K, tc // SUBLANES, nt, SUBLANES, LANES), F32), pltpu.SemaphoreType.DMA],
        compiler_params=pltpu.CompilerParams(dimension_semantics=("arbitrary",),
                                             vmem_limit_bytes=VMEM_LIMIT_BYTES),
        name="combine",
    )(dest_flat, tokmeta, h1, fg, ys)


def _rot_cols(w):
    h = QK_ROPE // 2
    return jnp.concatenate([-w[..., h:], w[..., :h]], axis=-1)


def _tile_rows(n, cap):
    t = min(n, cap)
    assert n % t == 0, (n, cap)
    return t


def kernel(x, meta_tokens, attn_norm_g, w_in, q_norm_g, w_uq, kv_norm_g, w_ukv, pool_w, pool_scale, w_o,
           ffn_norm_g, w_router, b_router, w1, b1, w2, b2, final_norm_g):
    nb, s_len, d = x.shape
    n_meta = meta_tokens.shape[0]
    assert w_in.shape[0] == 1, "one layer"
    assert n_meta == HALO and max(POOL_WINDOWS) - 1 <= HALO
    pw = pool_scale.shape[1]
    q_lora = q_norm_g.shape[1]
    kv_lora = kv_norm_g.shape[1]
    n_heads = w_uq.shape[2] // (QK_NOPE + QK_ROPE)
    n_exp = w_router.shape[2]
    f_dim = w2.shape[2]
    t = nb * s_len
    assert s_len % CHUNK == 0 and n_exp <= LANES and pw // len(POOL_WINDOWS) % LANES == 0

    win = w_in[0]
    o = pw + q_lora + kv_lora
    w_kr = win[:, o:o + QK_ROPE]
    zc = jnp.zeros((d, LANES - QK_ROPE), F32)
    win_b = jnp.concatenate([win[:, :o], w_kr, zc, _rot_cols(w_kr), zc], axis=1).astype(BF16)
    wq3 = w_uq[0].reshape(q_lora, n_heads, QK_NOPE + QK_ROPE)
    zq = jnp.zeros((q_lora, n_heads, LANES - QK_ROPE), F32)
    wq_b = jnp.concatenate([wq3, zq], axis=2).reshape(q_lora, n_heads * HEAD_PAD).astype(BF16)
    wqr_b = jnp.concatenate([_rot_cols(wq3[:, :, QK_NOPE:]), zq], axis=2).reshape(
        q_lora, n_heads * LANES).astype(BF16)
    wkv3 = w_ukv[0].reshape(kv_lora, n_heads, QK_NOPE + V_DIM)
    wk_b = wkv3[:, :, :QK_NOPE].reshape(kv_lora, n_heads * QK_NOPE).astype(BF16)
    wv_b = wkv3[:, :, QK_NOPE:].reshape(kv_lora, n_heads * V_DIM).astype(BF16)
    front_w = (attn_norm_g, win_b, pool_w[0].astype(BF16), pool_scale, q_norm_g, wq_b, wqr_b, kv_norm_g,
               wk_b, wv_b)
    woa = w_o[0, :pw].astype(BF16)
    wob = w_o[0, pw:].astype(BF16)
    wr_b = jnp.pad(w_router[0], ((0, 0), (0, LANES - n_exp))).astype(BF16)
    br = jnp.pad(b_router, ((0, 0), (0, LANES - n_exp)), constant_values=NEG_BIG)

    pos = jnp.arange(n_meta + s_len, dtype=F32)
    inv_freq = 1.0 / (ROPE_BASE ** (jnp.arange(0, QK_ROPE, 2, dtype=F32) / QK_ROPE))
    ang = pos[:, None] * inv_freq[None, :]
    ones = jnp.ones((n_meta + s_len, LANES - QK_ROPE), F32)
    cs = jnp.concatenate([jnp.cos(ang), jnp.cos(ang), ones], axis=1)
    sn = jnp.concatenate([jnp.sin(ang), jnp.sin(ang), 0.0 * ones], axis=1)

    zero_halo = jnp.zeros((HALO, pw), F32)
    _, _, k_meta, v_meta, p_meta = _front(meta_tokens[None], zero_halo, cs[:n_meta], sn[:n_meta], front_w,
                                          ts=n_meta)
    ts = _tile_rows(s_len, 512)
    y_pool, q, k, v, _ = _front(x, p_meta[0, 0], cs[n_meta:], sn[n_meta:], front_w, ts=ts)

    km = jnp.pad(k_meta[0], ((0, LANES - n_meta), (0, 0)))
    vm = jnp.pad(v_meta[0], ((0, LANES - n_meta), (0, 0)))
    y_mla = _attn(q, k, v, km, vm, tq=_tile_rows(s_len, 512), n_meta=n_meta)

    tm = _tile_rows(t, 512)
    h1, xn4, tokmeta, counts = _mid(y_pool.reshape(t, pw), y_mla.reshape(t, -1), x.reshape(t, d), woa, wob,
                                    ffn_norm_g, wr_b, br, tm=tm)

    r_max = 5 * EXPERT_BLOCK
    cnt = counts[0, :n_exp].astype(I32)
    padded = (cnt + EXPERT_BLOCK - 1) // EXPERT_BLOCK * EXPERT_BLOCK
    pad_end = jnp.cumsum(padded)
    pad_start = pad_end - padded
    n_assign = t * TOP_K
    p_rows = -(-(n_assign + n_exp * (EXPERT_BLOCK - 1)) // EXPERT_BLOCK) * EXPERT_BLOCK
    idx = tokmeta[:, 0:TOP_K].astype(I32)
    rank = tokmeta[:, TOP_K:2 * TOP_K].astype(I32)
    start_of = jnp.sum(jnp.where(idx[:, :, None] == jnp.arange(n_exp, dtype=I32), pad_start, 0), axis=-1)
    dest = (start_of + rank).reshape(-1)

    n_items = n_exp + p_rows // r_max
    per_e = (padded + r_max - 1) // r_max
    item_end = jnp.cumsum(per_e)
    total = item_end[-1]
    wi = jnp.arange(n_items, dtype=I32)
    valid = wi < total
    wc = jnp.minimum(wi, total - 1)
    ie = jnp.minimum(jnp.sum((item_end[None, :] <= wc[:, None]).astype(I32), axis=1), n_exp - 1)
    local = wc - (item_end[ie] - per_e[ie])
    item_row = jnp.where(valid, pad_start[ie] + local * r_max, 0).astype(I32)
    item_n = jnp.where(valid, jnp.clip(padded[ie] - local * r_max, 0, r_max), 0).astype(I32)

    zstart = (pad_start + cnt).astype(I32)
    zlen = (padded - cnt).astype(I32)
    tail = jnp.stack([pad_end[-1] // SUBLANES, (p_rows - pad_end[-1]) // (SUBLANES * ZERO_GROUPS)]).astype(I32)
    xs = _dispatch(zstart, zlen, tail, dest, xn4, p_rows, td=_tile_rows(t, 256))
    ys = _experts(ie, item_row, item_n, valid.astype(I32), xs, w1[0], b1[0], w2[0], b2[0],
                  r_max=r_max, tw=min(512, 2 * f_dim))
    out = _combine(dest, tokmeta, h1, final_norm_g.reshape(1, d), ys, tc=_tile_rows(t, 256))
    return out.reshape(nb, s_len, d)
```

```python
import functools

import jax
import jax.numpy as jnp
from jax import lax
from jax.experimental import pallas as pl
from jax.experimental.pallas import tpu as pltpu

F32 = jnp.float32
BF16 = jnp.bfloat16
I32 = jnp.int32

CHUNK = 64
POOL_WINDOWS = (2, 4, 8, 16)
V_DIM = 128
QK_NOPE = 128
QK_ROPE = 64
ROPE_BASE = 10000.0
TOP_K = 4
SWIGLU_LIMIT = 7.0
SWIGLU_ALPHA = 1.702
EPS = 1e-5
EXPERT_BLOCK = 512

LANES = 128
SUBLANES = 8
HEAD_PAD = 2 * LANES
VMEM_LIMIT_BYTES = 56 * 1024 * 1024

HALO = 16
NEG_BIG = -1e30


def _rms(x, g):
    ms = jnp.mean(x * x, axis=-1, keepdims=True)
    return x * lax.rsqrt(ms + EPS) * g


def _dot(a, b):
    return jnp.dot(a, b, preferred_element_type=F32)


def _dot_nt(a, b):
    return lax.dot_general(a, b, (((1,), (1,)), ((), ())), preferred_element_type=F32)


def _const_spec(shape):
    nd = len(shape)
    return pl.BlockSpec(shape, lambda *_: (0,) * nd)


def _row_of(ref, r):
    if isinstance(r, int):
        return ref.at[r // SUBLANES, :, r % SUBLANES, :]
    return ref.at[lax.shift_right_logical(r, 3), :, r & (SUBLANES - 1), :]


def _front_kernel(x_ref, mpool_ref, cs_ref, sn_ref, ag_ref, win_ref, pw_ref, ps_ref, qg_ref, wq_ref,
                  wqr_ref, kg_ref, wk_ref, wv_ref,
                  ypool_ref, q_ref, k_ref, v_ref, ptail_ref, ext_ref, *, ts, pool_w, q_lora, kv_lora,
                  n_heads):
    st = pl.program_id(1)
    hn = _rms(x_ref[0], ag_ref[...]).astype(BF16)
    proj = _dot(hn, win_ref[...])
    pool_in = proj[:, :pool_w]

    @pl.when(st == 0)
    def _():
        ext_ref[0:HALO, :] = mpool_ref[...]

    ext_ref[HALO:HALO + ts, :] = pool_in
    gw = pool_w // len(POOL_WINDOWS)
    for g, w in enumerate(POOL_WINDOWS):
        c0 = g * gw
        u = pool_in[:, c0:c0 + gw]
        s = u
        for k in range(1, w):
            s = s + ext_ref[HALO - k:HALO - k + ts, c0:c0 + gw]
        d = (s * (1.0 / w) - u).astype(BF16)
        y = _dot(d, pw_ref[g]) * ps_ref[:, c0:c0 + gw]
        ypool_ref[0, :, c0:c0 + gw] = y.astype(BF16)
    tail = pool_in[ts - HALO:ts, :]
    ext_ref[0:HALO, :] = tail
    ptail_ref[0, 0] = tail

    o = pool_w
    q_c = proj[:, o:o + q_lora]
    o += q_lora
    kv_c = proj[:, o:o + kv_lora]
    o += kv_lora
    kr = proj[:, o:o + LANES]
    kr_rot = proj[:, o + LANES:o + 2 * LANES]
    cs = cs_ref[...]
    sn = sn_ref[...]
    krope = (kr * cs + kr_rot * sn).astype(BF16)
    qn = _rms(q_c, qg_ref[...]).astype(BF16)
    qm = _dot(qn, wq_ref[...])
    qr = _dot(qn, wqr_ref[...])
    kvn = _rms(kv_c, kg_ref[...]).astype(BF16)
    kn = _dot(kvn, wk_ref[...])
    v_ref[0] = _dot(kvn, wv_ref[...]).astype(BF16)
    for h in range(n_heads):
        a = h * HEAD_PAD
        b = h * LANES
        q_ref[0, :, a:a + LANES] = qm[:, a:a + LANES].astype(BF16)
        q_ref[0, :, a + LANES:a + HEAD_PAD] = (
            qm[:, a + LANES:a + HEAD_PAD] * cs + qr[:, b:b + LANES] * sn).astype(BF16)
        k_ref[0, :, a:a + LANES] = kn[:, b:b + LANES].astype(BF16)
        k_ref[0, :, a + LANES:a + HEAD_PAD] = krope


def _front(x3, mpool, cs, sn, wts, *, ts):
    nb, s_len, d = x3.shape
    (ag, win, pw, ps, qg, wq, wqr, kg, wk, wv) = wts
    pool_w = ps.shape[1]
    q_lora = qg.shape[1]
    kv_lora = kg.shape[1]
    n_heads = wk.shape[1] // LANES
    n_st = s_len // ts
    kern = functools.partial(_front_kernel, ts=ts, pool_w=pool_w, q_lora=q_lora, kv_lora=kv_lora,
                             n_heads=n_heads)
    row = lambda b, s: (b, s, 0)
    in_specs = [
        pl.BlockSpec((1, ts, d), row),
        _const_spec(mpool.shape),
        pl.BlockSpec((ts, LANES), lambda b, s: (s, 0)),
        pl.BlockSpec((ts, LANES), lambda b, s: (s, 0)),
    ] + [_const_spec(w.shape) for w in wts]
    out_shape = (
        jax.ShapeDtypeStruct((nb, s_len, pool_w), BF16),
        jax.ShapeDtypeStruct((nb, s_len, n_heads * HEAD_PAD), BF16),
        jax.ShapeDtypeStruct((nb, s_len, n_heads * HEAD_PAD), BF16),
        jax.ShapeDtypeStruct((nb, s_len, n_heads * V_DIM), BF16),
        jax.ShapeDtypeStruct((nb, n_st, HALO, pool_w), F32),
    )
    out_specs = (
        pl.BlockSpec((1, ts, pool_w), row),
        pl.BlockSpec((1, ts, n_heads * HEAD_PAD), row),
        pl.BlockSpec((1, ts, n_heads * HEAD_PAD), row),
        pl.BlockSpec((1, ts, n_heads * V_DIM), row),
        pl.BlockSpec((1, 1, HALO, pool_w), lambda b, s: (b, s, 0, 0)),
    )
    return pl.pallas_call(
        kern, grid=(nb, n_st), in_specs=in_specs, out_specs=out_specs, out_shape=out_shape,
        scratch_shapes=[pltpu.VMEM((HALO + ts, pool_w), F32)],
        compiler_params=pltpu.CompilerParams(dimension_semantics=("arbitrary", "arbitrary"),
                                             vmem_limit_bytes=VMEM_LIMIT_BYTES),
        name="front",
    )(x3, mpool, cs, sn, *wts)


def _attn_kernel(q_ref, k_ref, v_ref, km_ref, vm_ref, o_ref, m_ref, l_ref, acc_ref, *, tq, n_meta, n_heads,
                 scale):
    qi = pl.program_id(1)
    c2 = scale * 1.4426950408889634

    def update(h, s, vb, first):
        s_max = jnp.max(s, axis=1, keepdims=True)
        if first:
            m_new = jnp.broadcast_to(s_max, (tq, LANES))
        else:
            m_old = m_ref[h]
            m_new = jnp.maximum(m_old, s_max)
            alpha = jnp.exp2((m_old - m_new) * c2)
        p = jnp.exp2((s - jnp.concatenate([m_new] * (s.shape[1] // LANES), axis=1)) * c2)
        v1 = jnp.concatenate([vb, jnp.ones(vb.shape, BF16)], axis=1)
        pv = _dot(p.astype(BF16), v1)
        if first:
            l_ref[h] = pv[:, V_DIM:]
            acc_ref[h] = pv[:, :V_DIM]
        else:
            l_ref[h] = alpha * l_ref[h] + pv[:, V_DIM:]
            acc_ref[h] = alpha * acc_ref[h] + pv[:, :V_DIM]
        m_ref[h] = m_new

    def q_of(h):
        return q_ref[0, :, h * HEAD_PAD:(h + 1) * HEAD_PAD]

    r_diag = pl.multiple_of(qi * tq, tq)
    rc = lax.broadcasted_iota(I32, (tq, tq), 0) // CHUNK
    cc = lax.broadcasted_iota(I32, (tq, tq), 1) // CHUNK
    vis = jnp.concatenate([cc <= rc, lax.broadcasted_iota(I32, (tq, LANES), 1) < n_meta], axis=1)
    for h in range(n_heads):
        kd = jnp.concatenate([k_ref[0, pl.ds(r_diag, tq), h * HEAD_PAD:(h + 1) * HEAD_PAD],
                              km_ref[:, h * HEAD_PAD:(h + 1) * HEAD_PAD]], axis=0)
        vd = jnp.concatenate([v_ref[0, pl.ds(r_diag, tq), h * V_DIM:(h + 1) * V_DIM],
                              vm_ref[:, h * V_DIM:(h + 1) * V_DIM]], axis=0)
        s = jnp.where(vis, _dot_nt(q_of(h), kd), -jnp.inf)
        update(h, s, vd, True)

    def body(j, c):
        r0 = pl.multiple_of(j * tq, tq)
        for h in range(n_heads):
            s = _dot_nt(q_of(h), k_ref[0, pl.ds(r0, tq), h * HEAD_PAD:(h + 1) * HEAD_PAD])
            update(h, s, v_ref[0, pl.ds(r0, tq), h * V_DIM:(h + 1) * V_DIM], False)
        return c

    lax.fori_loop(0, qi, body, 0)
    for h in range(n_heads):
        o_ref[0, :, h * V_DIM:(h + 1) * V_DIM] = (acc_ref[h] / l_ref[h]).astype(BF16)


def _attn(q, k, v, km, vm, *, tq, n_meta):
    nb, s_len, hw = q.shape
    n_heads = hw // HEAD_PAD
    kern = functools.partial(_attn_kernel, tq=tq, n_meta=n_meta, n_heads=n_heads,
                             scale=float((QK_NOPE + QK_ROPE) ** -0.5))
    return pl.pallas_call(
        kern, grid=(nb, s_len // tq),
        in_specs=[
            pl.BlockSpec((1, tq, hw), lambda b, i: (b, i, 0)),
            pl.BlockSpec((1, s_len, hw), lambda b, i: (b, 0, 0)),
            pl.BlockSpec((1, s_len, n_heads * V_DIM), lambda b, i: (b, 0, 0)),
            _const_spec(km.shape),
            _const_spec(vm.shape),
        ],
        out_specs=pl.BlockSpec((1, tq, n_heads * V_DIM), lambda b, i: (b, i, 0)),
        out_shape=jax.ShapeDtypeStruct((nb, s_len, n_heads * V_DIM), BF16),
        scratch_shapes=[pltpu.VMEM((n_heads, tq, LANES), F32)] * 3,
        compiler_params=pltpu.CompilerParams(dimension_semantics=("arbitrary", "arbitrary"),
                                             vmem_limit_bytes=VMEM_LIMIT_BYTES),
        name="attn",
    )(q, k, v, km, vm)


def _mid_kernel(yp_ref, ym_ref, x_ref, woa_ref, wob_ref, fg_ref, wr_ref, br_ref,
                h1_ref, xn4_ref, tokmeta_ref, counts_ref, run_ref, *, tm):
    i = pl.program_id(0)

    @pl.when(i == 0)
    def _():
        run_ref[...] = jnp.zeros_like(run_ref)

    h1 = x_ref[...] + _dot(yp_ref[...], woa_ref[...]) + _dot(ym_ref[...], wob_ref[...])
    h1_ref[...] = h1
    xn = _rms(h1, fg_ref[...])
    for c in range(xn4_ref.shape[1]):
        xn4_ref[:, c] = xn[:, c * LANES:(c + 1) * LANES].reshape(tm // SUBLANES, SUBLANES, LANES)
    xb = xn.astype(BF16)

    kh = xb.shape[1] // 2
    logits = _dot(xb[:, :kh], wr_ref[0:kh, :]) + _dot(xb[:, kh:], wr_ref[kh:2 * kh, :]) + br_ref[...]
    lane = lax.broadcasted_iota(I32, logits.shape, 1).astype(F32)
    work = logits
    idxs, vals = [], []
    for _ in range(TOP_K):
        mx = jnp.max(work, axis=1, keepdims=True)
        ix = jnp.min(jnp.where(work == mx, lane, float(LANES)), axis=1, keepdims=True)
        idxs.append(ix)
        vals.append(mx)
        work = jnp.where(lane == ix, -jnp.inf, work)
    es = [jnp.exp(vv - vals[0]) for vv in vals]
    den = es[0]
    for e in es[1:]:
        den = den + e
    hot = [jnp.where(lane == ix, 1.0, 0.0) for ix in idxs]
    cnt = hot[0]
    for hh in hot[1:]:
        cnt = cnt + hh
    rr = lax.broadcasted_iota(I32, (tm, tm), 0)
    cc = lax.broadcasted_iota(I32, (tm, tm), 1)
    ltri = jnp.where(rr > cc, 1.0, 0.0).astype(BF16)
    base = run_ref[0:1, :] + _dot(ltri, cnt.astype(BF16))
    out = jnp.zeros(logits.shape, F32)
    for k in range(TOP_K):
        rank = jnp.sum(hot[k] * base, axis=1, keepdims=True)
        out = jnp.where(lane == float(k), idxs[k], out)
        out = jnp.where(lane == float(TOP_K + k), rank, out)
        out = jnp.where(lane == float(2 * TOP_K + k), es[k] / den, out)
    tokmeta_ref[...] = out
    run = run_ref[...] + jnp.sum(cnt, axis=0, keepdims=True)
    run_ref[...] = run
    counts_ref[...] = run


def _mid(yp, ym, x2, woa, wob, fg, wr, br, *, tm):
    t, d = x2.shape
    pw = yp.shape[1]
    mw = ym.shape[1]
    nt = d // LANES
    kern = functools.partial(_mid_kernel, tm=tm)
    row = lambda i: (i, 0)
    return pl.pallas_call(
        kern, grid=(t // tm,),
        in_specs=[
            pl.BlockSpec((tm, pw), row), pl.BlockSpec((tm, mw), row), pl.BlockSpec((tm, d), row),
            _const_spec(woa.shape), _const_spec(wob.shape), _const_spec(fg.shape),
            _const_spec(wr.shape), _const_spec(br.shape),
        ],
        out_specs=(
            pl.BlockSpec((tm, d), row), pl.BlockSpec((tm // SUBLANES, nt, SUBLANES, LANES), lambda i: (i, 0, 0, 0)),
            pl.BlockSpec((tm, LANES), row),
            pl.BlockSpec((8, LANES), lambda i: (0, 0)),
        ),
        out_shape=(
            jax.ShapeDtypeStruct((t, d), F32), jax.ShapeDtypeStruct((t // SUBLANES, nt, SUBLANES, LANES), F32),
            jax.ShapeDtypeStruct((t, LANES), F32), jax.ShapeDtypeStruct((8, LANES), F32),
        ),
        scratch_shapes=[pltpu.VMEM((8, LANES), F32)],
        compiler_params=pltpu.CompilerParams(dimension_semantics=("arbitrary",),
                                             vmem_limit_bytes=VMEM_LIMIT_BYTES),
        name="mid",
    )(yp, ym, x2, woa, wob, fg, wr, br)


ZERO_GROUPS = EXPERT_BLOCK // (2 * SUBLANES)


def _dispatch_kernel(zstart_ref, zlen_ref, tail_ref, dest_ref, xn4_ref, xs_ref, zbuf, sem, sem_z, *, td, n_exp):
    def body(r, c):
        for k in range(TOP_K):
            pltpu.make_async_copy(_row_of(xn4_ref, r), _row_of(xs_ref, dest_ref[r * TOP_K + k]), sem).start(
                priority=k % 2)
        return c

    lax.fori_loop(0, td, body, 0, unroll=8)
    for _ in range(TOP_K):
        pltpu.make_async_copy(xn4_ref, xn4_ref, sem).wait()

    @pl.when(pl.program_id(0) == pl.num_programs(0) - 1)
    def _zero_fill():
        zbuf[...] = jnp.zeros_like(zbuf)
        tail0 = tail_ref[0]
        n_tail = tail_ref[1]

        def pad_copies(e, wait):
            zs = zstart_ref[e]
            zl = zlen_ref[e]
            head = jnp.minimum((-zs) & (SUBLANES - 1), zl)
            for h in range(SUBLANES - 1):
                @pl.when(h < head)
                def _(h=h):
                    cp = pltpu.make_async_copy(_row_of(zbuf, 0), _row_of(xs_ref, zs + h), sem_z)
                    cp.wait() if wait else cp.start()
            g0 = lax.shift_right_logical(zs + head, 3)
            ng = lax.shift_right_logical(zl - head, 3)
            v = ZERO_GROUPS
            while v >= 1:
                @pl.when((ng & v) != 0)
                def _(v=v):
                    off = g0 + (ng & (-2 * v))
                    cp = pltpu.make_async_copy(zbuf.at[pl.ds(0, v)], xs_ref.at[pl.ds(off, v)], sem_z)
                    cp.wait() if wait else cp.start()
                v //= 2

        def tail_copy(i, wait):
            cp = pltpu.make_async_copy(zbuf, xs_ref.at[pl.ds(tail0 + i * ZERO_GROUPS, ZERO_GROUPS)], sem_z)
            cp.wait() if wait else cp.start()

        for wait in (False, True):
            lax.fori_loop(0, n_exp, lambda e, c, wait=wait: (pad_copies(e, wait), c)[1], 0)
            lax.fori_loop(0, n_tail, lambda i, c, wait=wait: (tail_copy(i, wait), c)[1], 0)


def _dispatch(zstart, zlen, tail, dest_flat, xn4, p_rows, *, td):
    tg, nt, _, _ = xn4.shape
    n_exp = zstart.shape[0]
    kern = functools.partial(_dispatch_kernel, td=td, n_exp=n_exp)
    grid_spec = pltpu.PrefetchScalarGridSpec(
        num_scalar_prefetch=3, grid=(tg * SUBLANES // td,),
        in_specs=[
            pl.BlockSpec((td * TOP_K,), lambda i, *_: (i,), memory_space=pltpu.SMEM),
            pl.BlockSpec((td // SUBLANES, nt, SUBLANES, LANES), lambda i, *_: (i, 0, 0, 0)),
        ],
        out_specs=pl.BlockSpec(memory_space=pl.ANY),
        scratch_shapes=[pltpu.VMEM((ZERO_GROUPS, nt, SUBLANES, LANES), F32), pltpu.SemaphoreType.DMA,
                        pltpu.SemaphoreType.DMA],
    )
    return pl.pallas_call(
        kern, grid_spec=grid_spec,
        out_shape=jax.ShapeDtypeStruct((p_rows // SUBLANES, nt, SUBLANES, LANES), F32),
        compiler_params=pltpu.CompilerParams(dimension_semantics=("arbitrary",),
                                             vmem_limit_bytes=VMEM_LIMIT_BYTES),
        name="dispatch",
    )(zstart, zlen, tail, dest_flat, xn4)


def _expert_kernel(item_e_ref, item_row_ref, item_n_ref, item_valid_ref,
                   xs_ref, w1_ref, b1_ref, w2_ref, b2_ref, ys_ref,
                   xbuf, abuf, stage, wperm, w2b, hcbuf, sem_s, *, rb, j1, tw, nt):
    del item_e_ref, item_valid_ref, xs_ref
    w = pl.program_id(0)
    j = pl.program_id(1)
    nrows = item_n_ref[w]
    row0 = pl.multiple_of(item_row_ref[w], rb)
    nrb = nrows // rb
    d = w2_ref.shape[1]
    tf = w2_ref.shape[0]

    def rows(i):
        return pl.ds(pl.multiple_of(i * rb, rb), rb)

    g0 = lax.shift_right_logical(row0, 3)
    gb = rb // SUBLANES

    def stage_copy(i, slot, to_hbm):
        hbm = ys_ref.at[pl.ds(g0 + i * gb, gb)]
        vmem = stage.at[slot]
        return pltpu.make_async_copy(vmem, hbm, sem_s.at[slot]) if to_hbm else pltpu.make_async_copy(
            hbm, vmem, sem_s.at[slot])

    def slot_of(jj, i):
        return (jj * nrb + i) & 1

    def dot1(i, x):
        hcbuf[slot_of(j, i)] = _dot(x, w1_ref[...].astype(BF16)) + b1_ref[...]

    lane = lax.broadcasted_iota(I32, (rb, LANES), 1)
    even = (lane & 1) == 0

    def tail1(jj, i):
        hc = hcbuf[slot_of(jj, i)]
        outs = []
        for q in range(tw // (2 * LANES)):
            c0 = hc[:, 2 * q * LANES:(2 * q + 1) * LANES]
            c1 = hc[:, (2 * q + 1) * LANES:(2 * q + 2) * LANES]
            glu = jnp.where(even, c0, pltpu.roll(c1, 1, 1))
            lin = jnp.where(even, pltpu.roll(c0, LANES - 1, 1), c1)
            glu = jnp.minimum(glu, SWIGLU_LIMIT)
            lin = jnp.clip(lin, -SWIGLU_LIMIT, SWIGLU_LIMIT)
            act = glu * (1.0 / (1.0 + jnp.exp(-SWIGLU_ALPHA * glu))) * (lin + 1.0)
            outs.append(act.astype(BF16))
        abuf[jj, rows(i), :] = jnp.concatenate(outs, axis=1)

    def permute_w2_tile():
        for s in range(d // LANES):
            for g in range(tf // LANES):
                top = w2_ref[g * LANES:g * LANES + LANES // 2, s * LANES:(s + 1) * LANES]
                bot = w2_ref[g * LANES + LANES // 2:(g + 1) * LANES, s * LANES:(s + 1) * LANES]
                wperm.at[s][pl.ds(g * LANES, LANES // 2, stride=2), :] = top
                wperm.at[s][pl.ds(g * LANES + 1, LANES // 2, stride=2), :] = bot
        k0 = pl.multiple_of(j * tf, tf)
        for s in range(d // LANES):
            w2b[pl.ds(k0, tf), s * LANES:(s + 1) * LANES] = wperm[s].astype(BF16)

    @pl.when((j == 0) & (nrows > 0))
    def _first_step():
        def fetch(i):
            stage_copy(i, i & 1, False).wait()

            @pl.when(i + 1 < nrb)
            def _():
                stage_copy(i + 1, (i + 1) & 1, False).start()

        def load_and_dot(i):
            x = jnp.concatenate([stage[i & 1, :, c].reshape(rb, LANES) for c in range(nt)], axis=1).astype(BF16)
            xbuf[rows(i), :] = x
            dot1(i, x)

        stage_copy(0, 0, False).start()
        fetch(0)
        load_and_dot(0)
        permute_w2_tile()

        def body(i, c):
            fetch(i)
            tail1(0, i - 1)
            load_and_dot(i)
            return c

        lax.fori_loop(1, nrb, body, 0)

    @pl.when((j > 0) & (j < j1) & (nrows > 0))
    def _next_steps():
        tail1(j - 1, nrb - 1)
        dot1(0, xbuf[rows(0), :])
        permute_w2_tile()

        def body(i, c):
            tail1(j, i - 1)
            dot1(i, xbuf[rows(i), :])
            return c

        lax.fori_loop(1, nrb, body, 0)

    @pl.when((j == j1) & (nrows > 0))
    def _last_step():
        tail1(j1 - 1, nrb - 1)

        def body(i, c):
            @pl.when(i >= 2)
            def _():
                stage_copy(i - 2, i & 1, True).wait()

            @pl.when(i >= 1)
            def _():
                stage_copy(i - 1, (i - 1) & 1, True).start()

            a = jnp.concatenate([abuf[jj, rows(i), :] for jj in range(j1)], axis=1)
            y = _dot(a, w2b[...]) + b2_ref[...]
            for c in range(nt):
                stage[i & 1, :, c] = y[:, c * LANES:(c + 1) * LANES].reshape(gb, SUBLANES, LANES)
            return c

        lax.fori_loop(0, nrb, body, 0)

        @pl.when(nrb >= 2)
        def _():
            stage_copy(nrb - 2, nrb & 1, True).wait()

        last = stage_copy(nrb - 1, (nrb - 1) & 1, True)
        last.start()
        last.wait()


def _experts(item_e, item_row, item_n, item_valid, xs, w1, b1, w2, b2, *, r_max, tw):
    n_exp, d, f2 = w1.shape
    f_dim = w2.shape[1]
    nt = xs.shape[1]
    j1 = f2 // tw
    tf = f_dim // j1
    assert tf == tw // 2 and tf % LANES == 0 and nt * LANES == d
    rb = EXPERT_BLOCK
    kern = functools.partial(_expert_kernel, rb=rb, j1=j1, tw=tw, nt=nt)

    def w_step(w, j, iv):
        return jnp.minimum(jnp.where(iv[w] == 1, j, j1), j1 - 1)

    def w1_map(w, j, ie, ir, inn, iv):
        return (ie[w], 0, w_step(w, j, iv))

    def w2_map(w, j, ie, ir, inn, iv):
        return (ie[w], w_step(w, j, iv), 0)

    def e_map(w, j, ie, ir, inn, iv):
        return (ie[w], 0, 0)

    grid_spec = pltpu.PrefetchScalarGridSpec(
        num_scalar_prefetch=4, grid=(item_e.shape[0], j1 + 1),
        in_specs=[
            pl.BlockSpec(memory_space=pl.ANY),
            pl.BlockSpec((None, d, tw), w1_map),
            pl.BlockSpec((None, 1, tw), w1_map),
            pl.BlockSpec((None, tf, d), w2_map),
            pl.BlockSpec((None, 1, d), e_map),
        ],
        out_specs=pl.BlockSpec(memory_space=pl.ANY),
        scratch_shapes=[
            pltpu.VMEM((r_max, d), BF16),
            pltpu.VMEM((j1, r_max, tf), BF16),
            pltpu.VMEM((2, rb // SUBLANES, nt, SUBLANES, LANES), F32),
            pltpu.VMEM((d // LANES, tf, LANES), F32),
            pltpu.VMEM((f_dim, d), BF16),
            pltpu.VMEM((2, rb, tw), F32),
            pltpu.SemaphoreType.DMA((2,)),
        ],
    )
    return pl.pallas_call(
        kern, grid_spec=grid_spec,
        out_shape=jax.ShapeDtypeStruct(xs.shape, F32),
        input_output_aliases={4: 0},
        compiler_params=pltpu.CompilerParams(dimension_semantics=("arbitrary", "arbitrary"),
                                             vmem_limit_bytes=VMEM_LIMIT_BYTES),
        name="experts",
    )(item_e, item_row, item_n, item_valid, xs, w1, b1.reshape(n_exp, 1, f2), w2, b2.reshape(n_exp, 1, d))


def _combine_kernel(dest_ref, tokmeta_ref, h1_ref, fg_ref, ys_ref, o_ref, buf, ssq_ref, sem, *, tc, nt):
    i = pl.program_id(0)
    last = pl.num_programs(0) - 1
    slot = i & 1
    d = nt * LANES

    per_row = nt // SUBLANES

    def loop(do_sum, do_issue):
        def body(g, c):
            r8 = pl.ds(pl.multiple_of(g * SUBLANES, SUBLANES), SUBLANES)
            if do_sum:
                tm = tokmeta_ref[r8, :]
                gates = [tm[:, 2 * TOP_K + k:2 * TOP_K + k + 1] for k in range(TOP_K)]
                ssq = jnp.zeros((SUBLANES, LANES), F32)
            for s in range(SUBLANES):
                if do_issue:
                    for k in range(TOP_K):
                        src = _row_of(ys_ref, dest_ref[(g * SUBLANES + s) * TOP_K + k])
                        pltpu.make_async_copy(src, buf.at[slot, k, g, :, s, :], sem.at[slot]).start(
                            priority=k % 2)
                if do_sum:
                    for cc in range(s * per_row, (s + 1) * per_row):
                        a = h1_ref[r8, cc * LANES:(cc + 1) * LANES]
                        for k in range(TOP_K):
                            a = a + gates[k] * buf[1 - slot, k, g, cc]
                        o_ref[r8, cc * LANES:(cc + 1) * LANES] = a
                        ssq = ssq + a * a
            if do_sum:
                ssq_ref[r8, :] = ssq
            return c

        lax.fori_loop(0, tc // SUBLANES, body, 0)
        if do_sum:
            inv = lax.rsqrt(jnp.sum(ssq_ref[...], axis=1, keepdims=True) * (1.0 / d) + EPS)
            o_ref[...] = o_ref[...] * inv * fg_ref[...]

    @pl.when(i > 0)
    def _():
        for k in range(TOP_K):
            pltpu.make_async_copy(buf.at[1 - slot, k], buf.at[1 - slot, k], sem.at[1 - slot]).wait()

    @pl.when(i == 0)
    def _():
        loop(False, True)

    @pl.when((i > 0) & (i < last))
    def _():
        loop(True, True)

    @pl.when(i == last)
    def _():
        loop(True, False)


def _combine(dest_flat, tokmeta, h1, fg, ys, *, tc):
    t, d = h1.shape
    nt = ys.shape[1]
    nb = t // tc
    kern = functools.partial(_combine_kernel, tc=tc, nt=nt)
    prev = lambda i: (jnp.maximum(i - 1, 0), 0)
    return pl.pallas_call(
        kern, grid=(nb + 1,),
        in_specs=[
            pl.BlockSpec((tc * TOP_K,), lambda i: (jnp.minimum(i, nb - 1),), memory_space=pltpu.SMEM),
            pl.BlockSpec((tc, LANES), prev), pl.BlockSpec((tc, d), prev), _const_spec(fg.shape),
            pl.BlockSpec(memory_space=pl.ANY),
        ],
        out_specs=pl.BlockSpec((tc, d), prev),
        out_shape=jax.ShapeDtypeStruct((t, d), F32),
        scratch_shapes=[pltpu.VMEM((2, TOP_K, tc // SUBLANES, nt, SUBLANES, LANES), F32),
                        pltpu.VMEM((tc, LANES), F32), pltpu.SemaphoreType.DMA((2,))],
        compiler_params=pltpu.CompilerParams(dimension_semantics=("arbitrary",),
                                             vmem_limit_bytes=VMEM_LIMIT_BYTES),
        name="combine",
    )(dest_flat, tokmeta, h1, fg, ys)


def _rot_cols(w):
    h = QK_ROPE // 2
    return jnp.concatenate([-w[..., h:], w[..., :h]], axis=-1)


def _tile_rows(n, cap):
    t = min(n, cap)
    assert n % t == 0, (n, cap)
    return t


def kernel(x, meta_tokens, attn_norm_g, w_in, q_norm_g, w_uq, kv_norm_g, w_ukv, pool_w, pool_scale, w_o,
           ffn_norm_g, w_router, b_router, w1, b1, w2, b2, final_norm_g):
    nb, s_len, d = x.shape
    n_meta = meta_tokens.shape[0]
    assert w_in.shape[0] == 1, "one layer"
    assert n_meta == HALO and max(POOL_WINDOWS) - 1 <= HALO
    pw = pool_scale.shape[1]
    q_lora = q_norm_g.shape[1]
    kv_lora = kv_norm_g.shape[1]
    n_heads = w_uq.shape[2] // (QK_NOPE + QK_ROPE)
    n_exp = w_router.shape[2]
    f_dim = w2.shape[2]
    t = nb * s_len
    assert s_len % CHUNK == 0 and n_exp <= LANES and pw // len(POOL_WINDOWS) % LANES == 0

    win = w_in[0]
    o = pw + q_lora + kv_lora
    w_kr = win[:, o:o + QK_ROPE]
    zc = jnp.zeros((d, LANES - QK_ROPE), F32)
    win_b = jnp.concatenate([win[:, :o], w_kr, zc, _rot_cols(w_kr), zc], axis=1).astype(BF16)
    wq3 = w_uq[0].reshape(q_lora, n_heads, QK_NOPE + QK_ROPE)
    zq = jnp.zeros((q_lora, n_heads, LANES - QK_ROPE), F32)
    wq_b = jnp.concatenate([wq3, zq], axis=2).reshape(q_lora, n_heads * HEAD_PAD).astype(BF16)
    wqr_b = jnp.concatenate([_rot_cols(wq3[:, :, QK_NOPE:]), zq], axis=2).reshape(
        q_lora, n_heads * LANES).astype(BF16)
    wkv3 = w_ukv[0].reshape(kv_lora, n_heads, QK_NOPE + V_DIM)
    wk_b = wkv3[:, :, :QK_NOPE].reshape(kv_lora, n_heads * QK_NOPE).astype(BF16)
    wv_b = wkv3[:, :, QK_NOPE:].reshape(kv_lora, n_heads * V_DIM).astype(BF16)
    front_w = (attn_norm_g, win_b, pool_w[0].astype(BF16), pool_scale, q_norm_g, wq_b, wqr_b, kv_norm_g,
               wk_b, wv_b)
    woa = w_o[0, :pw].astype(BF16)
    wob = w_o[0, pw:].astype(BF16)
    wr_b = jnp.pad(w_router[0], ((0, 0), (0, LANES - n_exp))).astype(BF16)
    br = jnp.pad(b_router, ((0, 0), (0, LANES - n_exp)), constant_values=NEG_BIG)

    pos = jnp.arange(n_meta + s_len, dtype=F32)
    inv_freq = 1.0 / (ROPE_BASE ** (jnp.arange(0, QK_ROPE, 2, dtype=F32) / QK_ROPE))
    ang = pos[:, None] * inv_freq[None, :]
    ones = jnp.ones((n_meta + s_len, LANES - QK_ROPE), F32)
    cs = jnp.concatenate([jnp.cos(ang), jnp.cos(ang), ones], axis=1)
    sn = jnp.concatenate([jnp.sin(ang), jnp.sin(ang), 0.0 * ones], axis=1)

    zero_halo = jnp.zeros((HALO, pw), F32)
    _, _, k_meta, v_meta, p_meta = _front(meta_tokens[None], zero_halo, cs[:n_meta], sn[:n_meta], front_w,
                                          ts=n_meta)
    ts = _tile_rows(s_len, 512)
    y_pool, q, k, v, _ = _front(x, p_meta[0, 0], cs[n_meta:], sn[n_meta:], front_w, ts=ts)

    km = jnp.pad(k_meta[0], ((0, LANES - n_meta), (0, 0)))
    vm = jnp.pad(v_meta[0], ((0, LANES - n_meta), (0, 0)))
    y_mla = _attn(q, k, v, km, vm, tq=_tile_rows(s_len, 512), n_meta=n_meta)

    tm = _tile_rows(t, 512)
    h1, xn4, tokmeta, counts = _mid(y_pool.reshape(t, pw), y_mla.reshape(t, -1), x.reshape(t, d), woa, wob,
                                    ffn_norm_g, wr_b, br, tm=tm)

    r_max = 5 * EXPERT_BLOCK
    cnt = counts[0, :n_exp].astype(I32)
    padded = (cnt + EXPERT_BLOCK - 1) // EXPERT_BLOCK * EXPERT_BLOCK
    pad_end = jnp.cumsum(padded)
    pad_start = pad_end - padded
    n_assign = t * TOP_K
    p_rows = -(-(n_assign + n_exp * (EXPERT_BLOCK - 1)) // EXPERT_BLOCK) * EXPERT_BLOCK
    idx = tokmeta[:, 0:TOP_K].astype(I32)
    rank = tokmeta[:, TOP_K:2 * TOP_K].astype(I32)
    start_of = jnp.sum(jnp.where(idx[:, :, None] == jnp.arange(n_exp, dtype=I32), pad_start, 0), axis=-1)
    dest = (start_of + rank).reshape(-1)

    n_items = n_exp + p_rows // r_max
    per_e = (padded + r_max - 1) // r_max
    item_end = jnp.cumsum(per_e)
    total = item_end[-1]
    wi = jnp.arange(n_items, dtype=I32)
    valid = wi < total
    wc = jnp.minimum(wi, total - 1)
    ie = jnp.minimum(jnp.sum((item_end[None, :] <= wc[:, None]).astype(I32), axis=1), n_exp - 1)
    local = wc - (item_end[ie] - per_e[ie])
    item_row = jnp.where(valid, pad_start[ie] + local * r_max, 0).astype(I32)
    item_n = jnp.where(valid, jnp.clip(padded[ie] - local * r_max, 0, r_max), 0).astype(I32)

    zstart = (pad_start + cnt).astype(I32)
    zlen = (padded - cnt).astype(I32)
    tail = jnp.stack([pad_end[-1] // SUBLANES, (p_rows - pad_end[-1]) // (SUBLANES * ZERO_GROUPS)]).astype(I32)
    xs = _dispatch(zstart, zlen, tail, dest, xn4, p_rows, td=_tile_rows(t, 512))
    ys = _experts(ie, item_row, item_n, valid.astype(I32), xs, w1[0], b1[0], w2[0], b2[0],
                  r_max=r_max, tw=min(512, 2 * f_dim))
    out = _combine(dest, tokmeta, h1, final_norm_g.reshape(1, d), ys, tc=_tile_rows(t, 256))
    return out.reshape(nb, s_len, d)
```

```python
import functools

import jax
import jax.numpy as jnp
from jax import lax
from jax.experimental import pallas as pl
from jax.experimental.pallas import tpu as pltpu

F32 = jnp.float32
BF16 = jnp.bfloat16
I32 = jnp.int32

CHUNK = 64
POOL_WINDOWS = (2, 4, 8, 16)
V_DIM = 128
QK_NOPE = 128
QK_ROPE = 64
ROPE_BASE = 10000.0
TOP_K = 4
SWIGLU_LIMIT = 7.0
SWIGLU_ALPHA = 1.702
EPS = 1e-5
EXPERT_BLOCK = 512

LANES = 128
SUBLANES = 8
HEAD_PAD = 2 * LANES
VMEM_LIMIT_BYTES = 56 * 1024 * 1024

HALO = 16
NEG_BIG = -1e30


def _rms(x, g):
    ms = jnp.mean(x * x, axis=-1, keepdims=True)
    return x * lax.rsqrt(ms + EPS) * g


def _dot(a, b):
    return jnp.dot(a, b, preferred_element_type=F32)


def _dot_nt(a, b):
    return lax.dot_general(a, b, (((1,), (1,)), ((), ())), preferred_element_type=F32)


def _const_spec(shape):
    nd = len(shape)
    return pl.BlockSpec(shape, lambda *_: (0,) * nd)


def _row_of(ref, r):
    if isinstance(r, int):
        return ref.at[r // SUBLANES, :, r % SUBLANES, :]
    return ref.at[lax.shift_right_logical(r, 3), :, r & (SUBLANES - 1), :]


def _front_kernel(x_ref, mpool_ref, cs_ref, sn_ref, ag_ref, win_ref, pw_ref, ps_ref, qg_ref, wq_ref,
                  wqr_ref, kg_ref, wk_ref, wv_ref,
                  ypool_ref, q_ref, k_ref, v_ref, ptail_ref, ext_ref, lv_ref, *, ts, pool_w, q_lora, kv_lora,
                  n_heads):
    st = pl.program_id(1)
    hn = _rms(x_ref[0], ag_ref[...]).astype(BF16)
    proj = _dot(hn, win_ref[...])
    pool_in = proj[:, :pool_w]

    lo = SUBLANES
    hi = SUBLANES + HALO + ts

    @pl.when(st == 0)
    def _():
        ext_ref[0:lo, :] = jnp.zeros((lo, pool_w), F32)
        ext_ref[lo:lo + HALO, :] = mpool_ref[...]
        lv_ref[:, 0:lo, :] = jnp.zeros((2, lo, lv_ref.shape[2]), F32)

    ext_ref[lo + HALO:hi, :] = pool_in
    gw = pool_w // len(POOL_WINDOWS)
    for g, w in enumerate(POOL_WINDOWS):
        c0 = g * gw
        u = pool_in[:, c0:c0 + gw]
        cur = ext_ref[lo:hi, c0:c0 + gw]
        src, m = None, 1
        while m < w:
            prev = ext_ref[lo - m:hi - m, c0:c0 + gw] if src is None else lv_ref[src, lo - m:hi - m, :]
            cur = cur + prev
            m *= 2
            if m < w:
                src = 0 if src != 0 else 1
                lv_ref[src, lo:hi, :] = cur
        s = cur[HALO:, :]
        d = (s * (1.0 / w) - u).astype(BF16)
        y = _dot(d, pw_ref[g]) * ps_ref[:, c0:c0 + gw]
        ypool_ref[0, :, c0:c0 + gw] = y.astype(BF16)
    tail = pool_in[ts - HALO:ts, :]
    ext_ref[lo:lo + HALO, :] = tail
    ptail_ref[0, 0] = tail

    o = pool_w
    q_c = proj[:, o:o + q_lora]
    o += q_lora
    kv_c = proj[:, o:o + kv_lora]
    o += kv_lora
    kr = proj[:, o:o + LANES]
    kr_rot = proj[:, o + LANES:o + 2 * LANES]
    cs = cs_ref[...]
    sn = sn_ref[...]
    krope = (kr * cs + kr_rot * sn).astype(BF16)
    qn = _rms(q_c, qg_ref[...]).astype(BF16)
    qm = _dot(qn, wq_ref[...])
    qr = _dot(qn, wqr_ref[...])
    kvn = _rms(kv_c, kg_ref[...]).astype(BF16)
    kn = _dot(kvn, wk_ref[...])
    v_ref[0] = _dot(kvn, wv_ref[...]).astype(BF16)
    for h in range(n_heads):
        a = h * HEAD_PAD
        b = h * LANES
        q_ref[0, :, a:a + LANES] = qm[:, a:a + LANES].astype(BF16)
        q_ref[0, :, a + LANES:a + HEAD_PAD] = (
            qm[:, a + LANES:a + HEAD_PAD] * cs + qr[:, b:b + LANES] * sn).astype(BF16)
        k_ref[0, :, a:a + LANES] = kn[:, b:b + LANES].astype(BF16)
        k_ref[0, :, a + LANES:a + HEAD_PAD] = krope


def _front(x3, mpool, cs, sn, wts, *, ts):
    nb, s_len, d = x3.shape
    (ag, win, pw, ps, qg, wq, wqr, kg, wk, wv) = wts
    pool_w = ps.shape[1]
    q_lora = qg.shape[1]
    kv_lora = kg.shape[1]
    n_heads = wk.shape[1] // LANES
    n_st = s_len // ts
    kern = functools.partial(_front_kernel, ts=ts, pool_w=pool_w, q_lora=q_lora, kv_lora=kv_lora,
                             n_heads=n_heads)
    row = lambda b, s: (b, s, 0)
    in_specs = [
        pl.BlockSpec((1, ts, d), row),
        _const_spec(mpool.shape),
        pl.BlockSpec((ts, LANES), lambda b, s: (s, 0)),
        pl.BlockSpec((ts, LANES), lambda b, s: (s, 0)),
    ] + [_const_spec(w.shape) for w in wts]
    out_shape = (
        jax.ShapeDtypeStruct((nb, s_len, pool_w), BF16),
        jax.ShapeDtypeStruct((nb, s_len, n_heads * HEAD_PAD), BF16),
        jax.ShapeDtypeStruct((nb, s_len, n_heads * HEAD_PAD), BF16),
        jax.ShapeDtypeStruct((nb, s_len, n_heads * V_DIM), BF16),
        jax.ShapeDtypeStruct((nb, n_st, HALO, pool_w), F32),
    )
    out_specs = (
        pl.BlockSpec((1, ts, pool_w), row),
        pl.BlockSpec((1, ts, n_heads * HEAD_PAD), row),
        pl.BlockSpec((1, ts, n_heads * HEAD_PAD), row),
        pl.BlockSpec((1, ts, n_heads * V_DIM), row),
        pl.BlockSpec((1, 1, HALO, pool_w), lambda b, s: (b, s, 0, 0)),
    )
    return pl.pallas_call(
        kern, grid=(nb, n_st), in_specs=in_specs, out_specs=out_specs, out_shape=out_shape,
        scratch_shapes=[pltpu.VMEM((SUBLANES + HALO + ts, pool_w), F32),
                        pltpu.VMEM((2, SUBLANES + HALO + ts, pool_w // len(POOL_WINDOWS)), F32)],
        compiler_params=pltpu.CompilerParams(dimension_semantics=("arbitrary", "arbitrary"),
                                             vmem_limit_bytes=VMEM_LIMIT_BYTES),
        name="front",
    )(x3, mpool, cs, sn, *wts)


def _attn_kernel(q_ref, k_ref, v_ref, km_ref, vm_ref, o_ref, m_ref, l_ref, acc_ref, *, tq, n_meta, n_heads,
                 scale):
    qi = pl.program_id(1)
    c2 = scale * 1.4426950408889634

    def update(h, s, vb, first):
        s_max = jnp.max(s, axis=1, keepdims=True)
        if first:
            m_new = jnp.broadcast_to(s_max, (tq, LANES))
        else:
            m_old = m_ref[h]
            m_new = jnp.maximum(m_old, s_max)
            alpha = jnp.exp2((m_old - m_new) * c2)
        p = jnp.exp2((s - jnp.concatenate([m_new] * (s.shape[1] // LANES), axis=1)) * c2)
        v1 = jnp.concatenate([vb, jnp.ones(vb.shape, BF16)], axis=1)
        pv = _dot(p.astype(BF16), v1)
        if first:
            l_ref[h] = pv[:, V_DIM:]
            acc_ref[h] = pv[:, :V_DIM]
        else:
            l_ref[h] = alpha * l_ref[h] + pv[:, V_DIM:]
            acc_ref[h] = alpha * acc_ref[h] + pv[:, :V_DIM]
        m_ref[h] = m_new

    def q_of(h):
        return q_ref[0, :, h * HEAD_PAD:(h + 1) * HEAD_PAD]

    r_diag = pl.multiple_of(qi * tq, tq)
    rc = lax.broadcasted_iota(I32, (tq, tq), 0) // CHUNK
    cc = lax.broadcasted_iota(I32, (tq, tq), 1) // CHUNK
    vis = jnp.concatenate([cc <= rc, lax.broadcasted_iota(I32, (tq, LANES), 1) < n_meta], axis=1)
    for h in range(n_heads):
        kd = jnp.concatenate([k_ref[0, pl.ds(r_diag, tq), h * HEAD_PAD:(h + 1) * HEAD_PAD],
                              km_ref[:, h * HEAD_PAD:(h + 1) * HEAD_PAD]], axis=0)
        vd = jnp.concatenate([v_ref[0, pl.ds(r_diag, tq), h * V_DIM:(h + 1) * V_DIM],
                              vm_ref[:, h * V_DIM:(h + 1) * V_DIM]], axis=0)
        s = jnp.where(vis, _dot_nt(q_of(h), kd), -jnp.inf)
        update(h, s, vd, True)

    def body(j, c):
        r0 = pl.multiple_of(j * tq, tq)
        for h in range(n_heads):
            s = _dot_nt(q_of(h), k_ref[0, pl.ds(r0, tq), h * HEAD_PAD:(h + 1) * HEAD_PAD])
            update(h, s, v_ref[0, pl.ds(r0, tq), h * V_DIM:(h + 1) * V_DIM], False)
        return c

    lax.fori_loop(0, qi, body, 0)
    for h in range(n_heads):
        o_ref[0, :, h * V_DIM:(h + 1) * V_DIM] = (acc_ref[h] / l_ref[h]).astype(BF16)


def _attn(q, k, v, km, vm, *, tq, n_meta):
    nb, s_len, hw = q.shape
    n_heads = hw // HEAD_PAD
    kern = functools.partial(_attn_kernel, tq=tq, n_meta=n_meta, n_heads=n_heads,
                             scale=float((QK_NOPE + QK_ROPE) ** -0.5))
    return pl.pallas_call(
        kern, grid=(nb, s_len // tq),
        in_specs=[
            pl.BlockSpec((1, tq, hw), lambda b, i: (b, i, 0)),
            pl.BlockSpec((1, s_len, hw), lambda b, i: (b, 0, 0)),
            pl.BlockSpec((1, s_len, n_heads * V_DIM), lambda b, i: (b, 0, 0)),
            _const_spec(km.shape),
            _const_spec(vm.shape),
        ],
        out_specs=pl.BlockSpec((1, tq, n_heads * V_DIM), lambda b, i: (b, i, 0)),
        out_shape=jax.ShapeDtypeStruct((nb, s_len, n_heads * V_DIM), BF16),
        scratch_shapes=[pltpu.VMEM((n_heads, tq, LANES), F32)] * 3,
        compiler_params=pltpu.CompilerParams(dimension_semantics=("arbitrary", "arbitrary"),
                                             vmem_limit_bytes=VMEM_LIMIT_BYTES),
        name="attn",
    )(q, k, v, km, vm)


def _mid_kernel(yp_ref, ym_ref, x_ref, woa_ref, wob_ref, fg_ref, wr_ref, br_ref,
                h1_ref, xn4_ref, tokmeta_ref, tokmeta_t_ref, counts_ref, run_ref, *, tm):
    i = pl.program_id(0)

    @pl.when(i == 0)
    def _():
        run_ref[...] = jnp.zeros_like(run_ref)

    h1 = x_ref[...] + _dot(yp_ref[...], woa_ref[...]) + _dot(ym_ref[...], wob_ref[...])
    h1_ref[...] = h1
    xn = _rms(h1, fg_ref[...])
    for c in range(xn4_ref.shape[1]):
        xn4_ref[:, c] = xn[:, c * LANES:(c + 1) * LANES].reshape(tm // SUBLANES, SUBLANES, LANES)
    xb = xn.astype(BF16)

    kh = xb.shape[1] // 2
    logits = _dot(xb[:, :kh], wr_ref[0:kh, :]) + _dot(xb[:, kh:], wr_ref[kh:2 * kh, :]) + br_ref[...]
    lane = lax.broadcasted_iota(I32, logits.shape, 1).astype(F32)
    work = logits
    idxs, vals = [], []
    for _ in range(TOP_K):
        mx = jnp.max(work, axis=1, keepdims=True)
        ix = jnp.min(jnp.where(work == mx, lane, float(LANES)), axis=1, keepdims=True)
        idxs.append(ix)
        vals.append(mx)
        work = jnp.where(lane == ix, -jnp.inf, work)
    es = [jnp.exp(vv - vals[0]) for vv in vals]
    den = es[0]
    for e in es[1:]:
        den = den + e
    hot = [jnp.where(lane == ix, 1.0, 0.0) for ix in idxs]
    cnt = hot[0]
    for hh in hot[1:]:
        cnt = cnt + hh
    rr = lax.broadcasted_iota(I32, (tm, tm), 0)
    cc = lax.broadcasted_iota(I32, (tm, tm), 1)
    ltri = jnp.where(rr > cc, 1.0, 0.0).astype(BF16)
    base = run_ref[0:1, :] + _dot(ltri, cnt.astype(BF16))
    out = jnp.zeros(logits.shape, F32)
    for k in range(TOP_K):
        rank = jnp.sum(hot[k] * base, axis=1, keepdims=True)
        out = jnp.where(lane == float(k), idxs[k], out)
        out = jnp.where(lane == float(TOP_K + k), rank, out)
        out = jnp.where(lane == float(2 * TOP_K + k), es[k] / den, out)
    tokmeta_ref[...] = out
    tokmeta_t_ref[...] = jnp.transpose(out)[0:tokmeta_t_ref.shape[0], :]
    run = run_ref[...] + jnp.sum(cnt, axis=0, keepdims=True)
    run_ref[...] = run
    counts_ref[...] = run


def _mid(yp, ym, x2, woa, wob, fg, wr, br, *, tm):
    t, d = x2.shape
    pw = yp.shape[1]
    mw = ym.shape[1]
    nt = d // LANES
    kern = functools.partial(_mid_kernel, tm=tm)
    row = lambda i: (i, 0)
    return pl.pallas_call(
        kern, grid=(t // tm,),
        in_specs=[
            pl.BlockSpec((tm, pw), row), pl.BlockSpec((tm, mw), row), pl.BlockSpec((tm, d), row),
            _const_spec(woa.shape), _const_spec(wob.shape), _const_spec(fg.shape),
            _const_spec(wr.shape), _const_spec(br.shape),
        ],
        out_specs=(
            pl.BlockSpec((tm, d), row), pl.BlockSpec((tm // SUBLANES, nt, SUBLANES, LANES), lambda i: (i, 0, 0, 0)),
            pl.BlockSpec((tm, LANES), row), pl.BlockSpec((2 * SUBLANES, tm), lambda i: (0, i)),
            pl.BlockSpec((8, LANES), lambda i: (0, 0)),
        ),
        out_shape=(
            jax.ShapeDtypeStruct((t, d), F32), jax.ShapeDtypeStruct((t // SUBLANES, nt, SUBLANES, LANES), F32),
            jax.ShapeDtypeStruct((t, LANES), F32), jax.ShapeDtypeStruct((2 * SUBLANES, t), F32),
            jax.ShapeDtypeStruct((8, LANES), F32),
        ),
        scratch_shapes=[pltpu.VMEM((8, LANES), F32)],
        compiler_params=pltpu.CompilerParams(dimension_semantics=("arbitrary",),
                                             vmem_limit_bytes=VMEM_LIMIT_BYTES),
        name="mid",
    )(yp, ym, x2, woa, wob, fg, wr, br)


ZERO_GROUPS = EXPERT_BLOCK // (2 * SUBLANES)


def _dispatch_kernel(zstart_ref, zlen_ref, tail_ref, *refs, td, n_exp):
    dest_refs = refs[:TOP_K]
    xn4_ref, xs_ref, zbuf, sem, sem_z = refs[TOP_K:]

    def body(r, c):
        for k in range(TOP_K):
            pltpu.make_async_copy(_row_of(xn4_ref, r), _row_of(xs_ref, dest_refs[k][r]), sem).start(
                priority=k % 2)
        return c

    lax.fori_loop(0, td, body, 0, unroll=8)
    for _ in range(TOP_K):
        pltpu.make_async_copy(xn4_ref, xn4_ref, sem).wait()

    @pl.when(pl.program_id(0) == pl.num_programs(0) - 1)
    def _zero_fill():
        zbuf[...] = jnp.zeros_like(zbuf)
        tail0 = tail_ref[0]
        n_tail = tail_ref[1]

        def pad_copies(e, wait):
            zs = zstart_ref[e]
            zl = zlen_ref[e]
            head = jnp.minimum((-zs) & (SUBLANES - 1), zl)
            for h in range(SUBLANES - 1):
                @pl.when(h < head)
                def _(h=h):
                    cp = pltpu.make_async_copy(_row_of(zbuf, 0), _row_of(xs_ref, zs + h), sem_z)
                    cp.wait() if wait else cp.start()
            g0 = lax.shift_right_logical(zs + head, 3)
            ng = lax.shift_right_logical(zl - head, 3)
            v = ZERO_GROUPS
            while v >= 1:
                @pl.when((ng & v) != 0)
                def _(v=v):
                    off = g0 + (ng & (-2 * v))
                    cp = pltpu.make_async_copy(zbuf.at[pl.ds(0, v)], xs_ref.at[pl.ds(off, v)], sem_z)
                    cp.wait() if wait else cp.start()
                v //= 2

        def tail_copy(i, wait):
            cp = pltpu.make_async_copy(zbuf, xs_ref.at[pl.ds(tail0 + i * ZERO_GROUPS, ZERO_GROUPS)], sem_z)
            cp.wait() if wait else cp.start()

        for wait in (False, True):
            lax.fori_loop(0, n_exp, lambda e, c, wait=wait: (pad_copies(e, wait), c)[1], 0)
            lax.fori_loop(0, n_tail, lambda i, c, wait=wait: (tail_copy(i, wait), c)[1], 0)


def _dispatch(zstart, zlen, tail, dest_flat, xn4, p_rows, *, td):
    tg, nt, _, _ = xn4.shape
    n_exp = zstart.shape[0]
    kern = functools.partial(_dispatch_kernel, td=td, n_exp=n_exp)
    nb = tg * SUBLANES // td
    grid_spec = pltpu.PrefetchScalarGridSpec(
        num_scalar_prefetch=3, grid=(nb,),
        in_specs=[pl.BlockSpec((td,), lambda i, *_, k=k: (k * nb + i,), memory_space=pltpu.SMEM)
                  for k in range(TOP_K)] + [
            pl.BlockSpec((td // SUBLANES, nt, SUBLANES, LANES), lambda i, *_: (i, 0, 0, 0)),
        ],
        out_specs=pl.BlockSpec(memory_space=pl.ANY),
        scratch_shapes=[pltpu.VMEM((ZERO_GROUPS, nt, SUBLANES, LANES), F32), pltpu.SemaphoreType.DMA,
                        pltpu.SemaphoreType.DMA],
    )
    return pl.pallas_call(
        kern, grid_spec=grid_spec,
        out_shape=jax.ShapeDtypeStruct((p_rows // SUBLANES, nt, SUBLANES, LANES), F32),
        compiler_params=pltpu.CompilerParams(dimension_semantics=("arbitrary",),
                                             vmem_limit_bytes=VMEM_LIMIT_BYTES),
        name="dispatch",
    )(zstart, zlen, tail, *([dest_flat] * TOP_K), xn4)


def _expert_kernel(item_e_ref, item_row_ref, item_n_ref, item_valid_ref,
                   xs_ref, w1_ref, b1_ref, w2_ref, b2_ref, ys_ref,
                   xbuf, abuf, stage, wperm, w2b, hcbuf, sem_s, *, rb, j1, tw, nt):
    del item_e_ref, item_valid_ref, xs_ref
    w = pl.program_id(0)
    j = pl.program_id(1)
    nrows = item_n_ref[w]
    row0 = pl.multiple_of(item_row_ref[w], rb)
    nrb = nrows // rb
    d = w2_ref.shape[1]
    tf = w2_ref.shape[0]

    def rows(i):
        return pl.ds(pl.multiple_of(i * rb, rb), rb)

    g0 = lax.shift_right_logical(row0, 3)
    gb = rb // SUBLANES

    def stage_copy(i, slot, to_hbm):
        hbm = ys_ref.at[pl.ds(g0 + i * gb, gb)]
        vmem = stage.at[slot]
        return pltpu.make_async_copy(vmem, hbm, sem_s.at[slot]) if to_hbm else pltpu.make_async_copy(
            hbm, vmem, sem_s.at[slot])

    def slot_of(jj, i):
        return (jj * nrb + i) & 1

    def dot1(i, x):
        hcbuf[slot_of(j, i)] = _dot(x, w1_ref[...].astype(BF16)) + b1_ref[...]

    lane = lax.broadcasted_iota(I32, (rb, LANES), 1)
    even = (lane & 1) == 0

    def tail1(jj, i):
        hc = hcbuf[slot_of(jj, i)]
        outs = []
        for q in range(tw // (2 * LANES)):
            c0 = hc[:, 2 * q * LANES:(2 * q + 1) * LANES]
            c1 = hc[:, (2 * q + 1) * LANES:(2 * q + 2) * LANES]
            glu = jnp.where(even, c0, pltpu.roll(c1, 1, 1))
            lin = jnp.where(even, pltpu.roll(c0, LANES - 1, 1), c1)
            glu = jnp.minimum(glu, SWIGLU_LIMIT)
            lin = jnp.clip(lin, -SWIGLU_LIMIT, SWIGLU_LIMIT)
            act = glu * (1.0 / (1.0 + jnp.exp(-SWIGLU_ALPHA * glu))) * (lin + 1.0)
            outs.append(act.astype(BF16))
        abuf[jj, rows(i), :] = jnp.concatenate(outs, axis=1)

    def permute_w2_tile():
        for s in range(d // LANES):
            for g in range(tf // LANES):
                top = w2_ref[g * LANES:g * LANES + LANES // 2, s * LANES:(s + 1) * LANES]
                bot = w2_ref[g * LANES + LANES // 2:(g + 1) * LANES, s * LANES:(s + 1) * LANES]
                wperm.at[s][pl.ds(g * LANES, LANES // 2, stride=2), :] = top
                wperm.at[s][pl.ds(g * LANES + 1, LANES // 2, stride=2), :] = bot
        k0 = pl.multiple_of(j * tf, tf)
        for s in range(d // LANES):
            w2b[pl.ds(k0, tf), s * LANES:(s + 1) * LANES] = wperm[s].astype(BF16)

    @pl.when((j == 0) & (nrows > 0))
    def _first_step():
        def fetch(i):
            stage_copy(i, i & 1, False).wait()

            @pl.when(i + 1 < nrb)
            def _():
                stage_copy(i + 1, (i + 1) & 1, False).start()

        def load_and_dot(i):
            x = jnp.concatenate([stage[i & 1, :, c].reshape(rb, LANES) for c in range(nt)], axis=1).astype(BF16)
            xbuf[rows(i), :] = x
            dot1(i, x)

        stage_copy(0, 0, False).start()
        fetch(0)
        load_and_dot(0)
        permute_w2_tile()

        def body(i, c):
            fetch(i)
            tail1(0, i - 1)
            load_and_dot(i)
            return c

        lax.fori_loop(1, nrb, body, 0)

    @pl.when((j > 0) & (j < j1) & (nrows > 0))
    def _next_steps():
        tail1(j - 1, nrb - 1)
        dot1(0, xbuf[rows(0), :])
        permute_w2_tile()

        def body(i, c):
            tail1(j, i - 1)
            dot1(i, xbuf[rows(i), :])
            return c

        lax.fori_loop(1, nrb, body, 0)

    @pl.when((j == j1) & (nrows > 0))
    def _last_step():
        tail1(j1 - 1, nrb - 1)

        def body(i, c):
            @pl.when(i >= 2)
            def _():
                stage_copy(i - 2, i & 1, True).wait()

            @pl.when(i >= 1)
            def _():
                stage_copy(i - 1, (i - 1) & 1, True).start()

            a = jnp.concatenate([abuf[jj, rows(i), :] for jj in range(j1)], axis=1)
            y = _dot(a, w2b[...]) + b2_ref[...]
            for c in range(nt):
                stage[i & 1, :, c] = y[:, c * LANES:(c + 1) * LANES].reshape(gb, SUBLANES, LANES)
            return c

        lax.fori_loop(0, nrb, body, 0)

        @pl.when(nrb >= 2)
        def _():
            stage_copy(nrb - 2, nrb & 1, True).wait()

        last = stage_copy(nrb - 1, (nrb - 1) & 1, True)
        last.start()
        last.wait()


def _experts(item_e, item_row, item_n, item_valid, xs, w1, b1, w2, b2, *, r_max, tw):
    n_exp, d, f2 = w1.shape
    f_dim = w2.shape[1]
    nt = xs.shape[1]
    j1 = f2 // tw
    tf = f_dim // j1
    assert tf == tw // 2 and tf % LANES == 0 and nt * LANES == d
    rb = EXPERT_BLOCK
    kern = functools.partial(_expert_kernel, rb=rb, j1=j1, tw=tw, nt=nt)

    def w_step(w, j, iv):
        return jnp.minimum(jnp.where(iv[w] == 1, j, j1), j1 - 1)

    def w1_map(w, j, ie, ir, inn, iv):
        return (ie[w], 0, w_step(w, j, iv))

    def w2_map(w, j, ie, ir, inn, iv):
        return (ie[w], w_step(w, j, iv), 0)

    def e_map(w, j, ie, ir, inn, iv):
        return (ie[w], 0, 0)

    grid_spec = pltpu.PrefetchScalarGridSpec(
        num_scalar_prefetch=4, grid=(item_e.shape[0], j1 + 1),
        in_specs=[
            pl.BlockSpec(memory_space=pl.ANY),
            pl.BlockSpec((None, d, tw), w1_map),
            pl.BlockSpec((None, 1, tw), w1_map),
            pl.BlockSpec((None, tf, d), w2_map),
            pl.BlockSpec((None, 1, d), e_map),
        ],
        out_specs=pl.BlockSpec(memory_space=pl.ANY),
        scratch_shapes=[
            pltpu.VMEM((r_max, d), BF16),
            pltpu.VMEM((j1, r_max, tf), BF16),
            pltpu.VMEM((2, rb // SUBLANES, nt, SUBLANES, LANES), F32),
            pltpu.VMEM((d // LANES, tf, LANES), F32),
            pltpu.VMEM((f_dim, d), BF16),
            pltpu.VMEM((2, rb, tw), F32),
            pltpu.SemaphoreType.DMA((2,)),
        ],
    )
    return pl.pallas_call(
        kern, grid_spec=grid_spec,
        out_shape=jax.ShapeDtypeStruct(xs.shape, F32),
        input_output_aliases={4: 0},
        compiler_params=pltpu.CompilerParams(dimension_semantics=("arbitrary", "arbitrary"),
                                             vmem_limit_bytes=VMEM_LIMIT_BYTES),
        name="experts",
    )(item_e, item_row, item_n, item_valid, xs, w1, b1.reshape(n_exp, 1, f2), w2, b2.reshape(n_exp, 1, d))


def _combine_kernel(*refs, tc, nt):
    dest_refs = refs[:TOP_K]
    tokmeta_ref, h1_ref, fg_ref, ys_ref, o_ref, buf, ssq_ref, sem = refs[TOP_K:]
    i = pl.program_id(0)
    last = pl.num_programs(0) - 1
    slot = i & 1
    d = nt * LANES

    per_row = nt // SUBLANES

    def loop(do_sum, do_issue):
        def body(g, c):
            r8 = pl.ds(pl.multiple_of(g * SUBLANES, SUBLANES), SUBLANES)
            if do_sum:
                tm = tokmeta_ref[r8, :]
                gates = [tm[:, 2 * TOP_K + k:2 * TOP_K + k + 1] for k in range(TOP_K)]
                ssq = jnp.zeros((SUBLANES, LANES), F32)
            for s in range(SUBLANES):
                if do_issue:
                    for k in range(TOP_K):
                        src = _row_of(ys_ref, dest_refs[k][g * SUBLANES + s])
                        pltpu.make_async_copy(src, buf.at[slot, k, g, :, s, :], sem.at[slot]).start(
                            priority=k % 2)
                if do_sum:
                    for cc in range(s * per_row, (s + 1) * per_row):
                        a = h1_ref[r8, cc * LANES:(cc + 1) * LANES]
                        for k in range(TOP_K):
                            a = a + gates[k] * buf[1 - slot, k, g, cc]
                        o_ref[r8, cc * LANES:(cc + 1) * LANES] = a
                        ssq = ssq + a * a
            if do_sum:
                ssq_ref[r8, :] = ssq
            return c

        lax.fori_loop(0, tc // SUBLANES, body, 0)
        if do_sum:
            inv = lax.rsqrt(jnp.sum(ssq_ref[...], axis=1, keepdims=True) * (1.0 / d) + EPS)
            o_ref[...] = o_ref[...] * inv * fg_ref[...]

    @pl.when(i > 0)
    def _():
        for k in range(TOP_K):
            pltpu.make_async_copy(buf.at[1 - slot, k], buf.at[1 - slot, k], sem.at[1 - slot]).wait()

    @pl.when(i == 0)
    def _():
        loop(False, True)

    @pl.when((i > 0) & (i < last))
    def _():
        loop(True, True)

    @pl.when(i == last)
    def _():
        loop(True, False)


def _combine(dest_flat, tokmeta, h1, fg, ys, *, tc):
    t, d = h1.shape
    nt = ys.shape[1]
    nb = t // tc
    kern = functools.partial(_combine_kernel, tc=tc, nt=nt)
    prev = lambda i: (jnp.maximum(i - 1, 0), 0)
    return pl.pallas_call(
        kern, grid=(nb + 1,),
        in_specs=[pl.BlockSpec((tc,), lambda i, k=k: (k * nb + jnp.minimum(i, nb - 1),), memory_space=pltpu.SMEM)
                  for k in range(TOP_K)] + [
            pl.BlockSpec((tc, LANES), prev), pl.BlockSpec((tc, d), prev), _const_spec(fg.shape),
            pl.BlockSpec(memory_space=pl.ANY),
        ],
        out_specs=pl.BlockSpec((tc, d), prev),
        out_shape=jax.ShapeDtypeStruct((t, d), F32),
        scratch_shapes=[pltpu.VMEM((2, TOP_K, tc // SUBLANES, nt, SUBLANES, LANES), F32),
                        pltpu.VMEM((tc, LANES), F32), pltpu.SemaphoreType.DMA((2,))],
        compiler_params=pltpu.CompilerParams(dimension_semantics=("arbitrary",),
                                             vmem_limit_bytes=VMEM_LIMIT_BYTES),
        name="combine",
    )(*([dest_flat] * TOP_K), tokmeta, h1, fg, ys)


def _rot_cols(w):
    h = QK_ROPE // 2
    return jnp.concatenate([-w[..., h:], w[..., :h]], axis=-1)


def _tile_rows(n, cap):
    t = min(n, cap)
    assert n % t == 0, (n, cap)
    return t


def kernel(x, meta_tokens, attn_norm_g, w_in, q_norm_g, w_uq, kv_norm_g, w_ukv, pool_w, pool_scale, w_o,
           ffn_norm_g, w_router, b_router, w1, b1, w2, b2, final_norm_g):
    nb, s_len, d = x.shape
    n_meta = meta_tokens.shape[0]
    assert w_in.shape[0] == 1, "one layer"
    assert n_meta == HALO and max(POOL_WINDOWS) - 1 <= HALO
    pw = pool_scale.shape[1]
    q_lora = q_norm_g.shape[1]
    kv_lora = kv_norm_g.shape[1]
    n_heads = w_uq.shape[2] // (QK_NOPE + QK_ROPE)
    n_exp = w_router.shape[2]
    f_dim = w2.shape[2]
    t = nb * s_len
    assert s_len % CHUNK == 0 and n_exp <= LANES and pw // len(POOL_WINDOWS) % LANES == 0

    win = w_in[0]
    o = pw + q_lora + kv_lora
    w_kr = win[:, o:o + QK_ROPE]
    zc = jnp.zeros((d, LANES - QK_ROPE), F32)
    win_b = jnp.concatenate([win[:, :o], w_kr, zc, _rot_cols(w_kr), zc], axis=1).astype(BF16)
    wq3 = w_uq[0].reshape(q_lora, n_heads, QK_NOPE + QK_ROPE)
    zq = jnp.zeros((q_lora, n_heads, LANES - QK_ROPE), F32)
    wq_b = jnp.concatenate([wq3, zq], axis=2).reshape(q_lora, n_heads * HEAD_PAD).astype(BF16)
    wqr_b = jnp.concatenate([_rot_cols(wq3[:, :, QK_NOPE:]), zq], axis=2).reshape(
        q_lora, n_heads * LANES).astype(BF16)
    wkv3 = w_ukv[0].reshape(kv_lora, n_heads, QK_NOPE + V_DIM)
    wk_b = wkv3[:, :, :QK_NOPE].reshape(kv_lora, n_heads * QK_NOPE).astype(BF16)
    wv_b = wkv3[:, :, QK_NOPE:].reshape(kv_lora, n_heads * V_DIM).astype(BF16)
    front_w = (attn_norm_g, win_b, pool_w[0].astype(BF16), pool_scale, q_norm_g, wq_b, wqr_b, kv_norm_g,
               wk_b, wv_b)
    woa = w_o[0, :pw].astype(BF16)
    wob = w_o[0, pw:].astype(BF16)
    wr_b = jnp.pad(w_router[0], ((0, 0), (0, LANES - n_exp))).astype(BF16)
    br = jnp.pad(b_router, ((0, 0), (0, LANES - n_exp)), constant_values=NEG_BIG)

    pos = jnp.arange(n_meta + s_len, dtype=F32)
    inv_freq = 1.0 / (ROPE_BASE ** (jnp.arange(0, QK_ROPE, 2, dtype=F32) / QK_ROPE))
    ang = pos[:, None] * inv_freq[None, :]
    ones = jnp.ones((n_meta + s_len, LANES - QK_ROPE), F32)
    cs = jnp.concatenate([jnp.cos(ang), jnp.cos(ang), ones], axis=1)
    sn = jnp.concatenate([jnp.sin(ang), jnp.sin(ang), 0.0 * ones], axis=1)

    zero_halo = jnp.zeros((HALO, pw), F32)
    _, _, k_meta, v_meta, p_meta = _front(meta_tokens[None], zero_halo, cs[:n_meta], sn[:n_meta], front_w,
                                          ts=n_meta)
    ts = _tile_rows(s_len, 512)
    y_pool, q, k, v, _ = _front(x, p_meta[0, 0], cs[n_meta:], sn[n_meta:], front_w, ts=ts)

    km = jnp.pad(k_meta[0], ((0, LANES - n_meta), (0, 0)))
    vm = jnp.pad(v_meta[0], ((0, LANES - n_meta), (0, 0)))
    y_mla = _attn(q, k, v, km, vm, tq=_tile_rows(s_len, 512), n_meta=n_meta)

    tm = _tile_rows(t, 512)
    h1, xn4, tokmeta, tokmeta_t, counts = _mid(y_pool.reshape(t, pw), y_mla.reshape(t, -1), x.reshape(t, d),
                                               woa, wob, ffn_norm_g, wr_b, br, tm=tm)

    r_max = 5 * EXPERT_BLOCK
    cnt = counts[0, :n_exp].astype(I32)
    padded = (cnt + EXPERT_BLOCK - 1) // EXPERT_BLOCK * EXPERT_BLOCK
    pad_end = jnp.cumsum(padded)
    pad_start = pad_end - padded
    n_assign = t * TOP_K
    p_rows = -(-(n_assign + n_exp * (EXPERT_BLOCK - 1)) // EXPERT_BLOCK) * EXPERT_BLOCK
    idx = tokmeta_t[0:TOP_K].astype(I32)
    rank = tokmeta_t[TOP_K:2 * TOP_K].astype(I32)
    start_of = jnp.zeros_like(idx)
    for e in range(n_exp):
        start_of = jnp.where(idx == e, pad_start[e], start_of)
    dest = (start_of + rank).reshape(-1)

    n_items = n_exp + p_rows // r_max
    per_e = (padded + r_max - 1) // r_max
    item_end = jnp.cumsum(per_e)
    total = item_end[-1]
    wi = jnp.arange(n_items, dtype=I32)
    valid = wi < total
    wc = jnp.minimum(wi, total - 1)
    ie = jnp.minimum(jnp.sum((item_end[None, :] <= wc[:, None]).astype(I32), axis=1), n_exp - 1)
    local = wc - (item_end[ie] - per_e[ie])
    item_row = jnp.where(valid, pad_start[ie] + local * r_max, 0).astype(I32)
    item_n = jnp.where(valid, jnp.clip(padded[ie] - local * r_max, 0, r_max), 0).astype(I32)

    zstart = (pad_start + cnt).astype(I32)
    zlen = (padded - cnt).astype(I32)
    tail = jnp.stack([pad_end[-1] // SUBLANES, (p_rows - pad_end[-1]) // (SUBLANES * ZERO_GROUPS)]).astype(I32)
    xs = _dispatch(zstart, zlen, tail, dest, xn4, p_rows, td=_tile_rows(t, 512))
    ys = _experts(ie, item_row, item_n, valid.astype(I32), xs, w1[0], b1[0], w2[0], b2[0],
                  r_max=r_max, tw=min(512, 2 * f_dim))
    out = _combine(dest, tokmeta, h1, final_norm_g.reshape(1, d), ys, tc=_tile_rows(t, 256))
    return out.reshape(nb, s_len, d)
```

```python
import functools

import jax
import jax.numpy as jnp
from jax import lax
from jax.experimental import pallas as pl
from jax.experimental.pallas import tpu as pltpu

F32 = jnp.float32
BF16 = jnp.bfloat16
I32 = jnp.int32

CHUNK = 64
POOL_WINDOWS = (2, 4, 8, 16)
V_DIM = 128
QK_NOPE = 128
QK_ROPE = 64
ROPE_BASE = 10000.0
TOP_K = 4
SWIGLU_LIMIT = 7.0
SWIGLU_ALPHA = 1.702
EPS = 1e-5
EXPERT_BLOCK = 512

LANES = 128
SUBLANES = 8
HEAD_PAD = 2 * LANES
VMEM_LIMIT_BYTES = 58 * 1024 * 1024

HALO = 16
NEG_BIG = -1e30


def _rms(x, g):
    ms = jnp.mean(x * x, axis=-1, keepdims=True)
    return x * lax.rsqrt(ms + EPS) * g


def _dot(a, b):
    return jnp.dot(a, b, preferred_element_type=F32)


def _dot_nt(a, b):
    return lax.dot_general(a, b, (((1,), (1,)), ((), ())), preferred_element_type=F32)


def _const_spec(shape):
    nd = len(shape)
    return pl.BlockSpec(shape, lambda *_: (0,) * nd)


def _row_of(ref, r):
    if isinstance(r, int):
        return ref.at[r // SUBLANES, :, r % SUBLANES, :]
    return ref.at[lax.shift_right_logical(r, 3), :, r & (SUBLANES - 1), :]


def _front_kernel(x_ref, mpool_ref, cs_ref, sn_ref, ag_ref, win_ref, pw_ref, ps_ref, qg_ref, wq_ref,
                  wqr_ref, kg_ref, wk_ref, wv_ref,
                  ypool_ref, q_ref, k_ref, v_ref, ptail_ref, ext_ref, lv_ref, *, ts, pool_w, q_lora, kv_lora,
                  n_heads):
    st = pl.program_id(1)
    hn = _rms(x_ref[0], ag_ref[...]).astype(BF16)
    proj = _dot(hn, win_ref[...])
    pool_in = proj[:, :pool_w]

    lo = SUBLANES
    hi = SUBLANES + HALO + ts

    @pl.when(st == 0)
    def _():
        ext_ref[0:lo, :] = jnp.zeros((lo, pool_w), F32)
        ext_ref[lo:lo + HALO, :] = mpool_ref[...]
        lv_ref[:, 0:lo, :] = jnp.zeros((2, lo, lv_ref.shape[2]), F32)

    ext_ref[lo + HALO:hi, :] = pool_in
    gw = pool_w // len(POOL_WINDOWS)
    for g, w in enumerate(POOL_WINDOWS):
        c0 = g * gw
        u = pool_in[:, c0:c0 + gw]
        cur = ext_ref[lo:hi, c0:c0 + gw]
        src, m = None, 1
        while m < w:
            prev = ext_ref[lo - m:hi - m, c0:c0 + gw] if src is None else lv_ref[src, lo - m:hi - m, :]
            cur = cur + prev
            m *= 2
            if m < w:
                src = 0 if src != 0 else 1
                lv_ref[src, lo:hi, :] = cur
        s = cur[HALO:, :]
        d = (s * (1.0 / w) - u).astype(BF16)
        y = _dot(d, pw_ref[g]) * ps_ref[:, c0:c0 + gw]
        ypool_ref[0, :, c0:c0 + gw] = y.astype(BF16)
    tail = pool_in[ts - HALO:ts, :]
    ext_ref[lo:lo + HALO, :] = tail
    ptail_ref[0, 0] = tail

    o = pool_w
    q_c = proj[:, o:o + q_lora]
    o += q_lora
    kv_c = proj[:, o:o + kv_lora]
    o += kv_lora
    kr = proj[:, o:o + LANES]
    kr_rot = proj[:, o + LANES:o + 2 * LANES]
    cs = cs_ref[...]
    sn = sn_ref[...]
    krope = (kr * cs + kr_rot * sn).astype(BF16)
    qn = _rms(q_c, qg_ref[...]).astype(BF16)
    qm = _dot(qn, wq_ref[...])
    qr = _dot(qn, wqr_ref[...])
    kvn = _rms(kv_c, kg_ref[...]).astype(BF16)
    kn = _dot(kvn, wk_ref[...])
    v_ref[0] = _dot(kvn, wv_ref[...]).astype(BF16)
    for h in range(n_heads):
        a = h * HEAD_PAD
        b = h * LANES
        q_ref[0, :, a:a + LANES] = qm[:, a:a + LANES].astype(BF16)
        q_ref[0, :, a + LANES:a + HEAD_PAD] = (
            qm[:, a + LANES:a + HEAD_PAD] * cs + qr[:, b:b + LANES] * sn).astype(BF16)
        k_ref[0, :, a:a + LANES] = kn[:, b:b + LANES].astype(BF16)
        k_ref[0, :, a + LANES:a + HEAD_PAD] = krope


def _front(x3, mpool, cs, sn, wts, *, ts):
    nb, s_len, d = x3.shape
    (ag, win, pw, ps, qg, wq, wqr, kg, wk, wv) = wts
    pool_w = ps.shape[1]
    q_lora = qg.shape[1]
    kv_lora = kg.shape[1]
    n_heads = wk.shape[1] // LANES
    n_st = s_len // ts
    kern = functools.partial(_front_kernel, ts=ts, pool_w=pool_w, q_lora=q_lora, kv_lora=kv_lora,
                             n_heads=n_heads)
    row = lambda b, s: (b, s, 0)
    in_specs = [
        pl.BlockSpec((1, ts, d), row),
        _const_spec(mpool.shape),
        pl.BlockSpec((ts, LANES), lambda b, s: (s, 0)),
        pl.BlockSpec((ts, LANES), lambda b, s: (s, 0)),
    ] + [_const_spec(w.shape) for w in wts]
    out_shape = (
        jax.ShapeDtypeStruct((nb, s_len, pool_w), BF16),
        jax.ShapeDtypeStruct((nb, s_len, n_heads * HEAD_PAD), BF16),
        jax.ShapeDtypeStruct((nb, s_len, n_heads * HEAD_PAD), BF16),
        jax.ShapeDtypeStruct((nb, s_len, n_heads * V_DIM), BF16),
        jax.ShapeDtypeStruct((nb, n_st, HALO, pool_w), F32),
    )
    out_specs = (
        pl.BlockSpec((1, ts, pool_w), row),
        pl.BlockSpec((1, ts, n_heads * HEAD_PAD), row),
        pl.BlockSpec((1, ts, n_heads * HEAD_PAD), row),
        pl.BlockSpec((1, ts, n_heads * V_DIM), row),
        pl.BlockSpec((1, 1, HALO, pool_w), lambda b, s: (b, s, 0, 0)),
    )
    return pl.pallas_call(
        kern, grid=(nb, n_st), in_specs=in_specs, out_specs=out_specs, out_shape=out_shape,
        scratch_shapes=[pltpu.VMEM((SUBLANES + HALO + ts, pool_w), F32),
                        pltpu.VMEM((2, SUBLANES + HALO + ts, pool_w // len(POOL_WINDOWS)), F32)],
        compiler_params=pltpu.CompilerParams(dimension_semantics=("arbitrary", "arbitrary"),
                                             vmem_limit_bytes=VMEM_LIMIT_BYTES),
        name="front",
    )(x3, mpool, cs, sn, *wts)


def _attn_kernel(q_ref, k_ref, v_ref, km_ref, vm_ref, o_ref, m_ref, l_ref, acc_ref, *, tq, n_meta, n_heads,
                 scale):
    qi = pl.program_id(1)
    c2 = scale * 1.4426950408889634

    def update(h, s, vb, first):
        s_max = jnp.max(s, axis=1, keepdims=True)
        if first:
            m_new = jnp.broadcast_to(s_max, (tq, LANES))
        else:
            m_old = m_ref[h]
            m_new = jnp.maximum(m_old, s_max)
            alpha = jnp.exp2((m_old - m_new) * c2)
        p = jnp.exp2((s - jnp.concatenate([m_new] * (s.shape[1] // LANES), axis=1)) * c2)
        v1 = jnp.concatenate([vb, jnp.ones(vb.shape, BF16)], axis=1)
        pv = _dot(p.astype(BF16), v1)
        if first:
            l_ref[h] = pv[:, V_DIM:]
            acc_ref[h] = pv[:, :V_DIM]
        else:
            l_ref[h] = alpha * l_ref[h] + pv[:, V_DIM:]
            acc_ref[h] = alpha * acc_ref[h] + pv[:, :V_DIM]
        m_ref[h] = m_new

    def q_of(h):
        return q_ref[0, :, h * HEAD_PAD:(h + 1) * HEAD_PAD]

    r_diag = pl.multiple_of(qi * tq, tq)
    rc = lax.broadcasted_iota(I32, (tq, tq), 0) // CHUNK
    cc = lax.broadcasted_iota(I32, (tq, tq), 1) // CHUNK
    vis = jnp.concatenate([cc <= rc, lax.broadcasted_iota(I32, (tq, LANES), 1) < n_meta], axis=1)
    for h in range(n_heads):
        kd = jnp.concatenate([k_ref[0, pl.ds(r_diag, tq), h * HEAD_PAD:(h + 1) * HEAD_PAD],
                              km_ref[:, h * HEAD_PAD:(h + 1) * HEAD_PAD]], axis=0)
        vd = jnp.concatenate([v_ref[0, pl.ds(r_diag, tq), h * V_DIM:(h + 1) * V_DIM],
                              vm_ref[:, h * V_DIM:(h + 1) * V_DIM]], axis=0)
        s = jnp.where(vis, _dot_nt(q_of(h), kd), -jnp.inf)
        update(h, s, vd, True)

    def body(j, c):
        r0 = pl.multiple_of(j * tq, tq)
        for h in range(n_heads):
            s = _dot_nt(q_of(h), k_ref[0, pl.ds(r0, tq), h * HEAD_PAD:(h + 1) * HEAD_PAD])
            update(h, s, v_ref[0, pl.ds(r0, tq), h * V_DIM:(h + 1) * V_DIM], False)
        return c

    lax.fori_loop(0, qi, body, 0)
    for h in range(n_heads):
        o_ref[0, :, h * V_DIM:(h + 1) * V_DIM] = (acc_ref[h] / l_ref[h]).astype(BF16)


def _attn(q, k, v, km, vm, *, tq, n_meta):
    nb, s_len, hw = q.shape
    n_heads = hw // HEAD_PAD
    kern = functools.partial(_attn_kernel, tq=tq, n_meta=n_meta, n_heads=n_heads,
                             scale=float((QK_NOPE + QK_ROPE) ** -0.5))
    return pl.pallas_call(
        kern, grid=(nb, s_len // tq),
        in_specs=[
            pl.BlockSpec((1, tq, hw), lambda b, i: (b, i, 0)),
            pl.BlockSpec((1, s_len, hw), lambda b, i: (b, 0, 0)),
            pl.BlockSpec((1, s_len, n_heads * V_DIM), lambda b, i: (b, 0, 0)),
            _const_spec(km.shape),
            _const_spec(vm.shape),
        ],
        out_specs=pl.BlockSpec((1, tq, n_heads * V_DIM), lambda b, i: (b, i, 0)),
        out_shape=jax.ShapeDtypeStruct((nb, s_len, n_heads * V_DIM), BF16),
        scratch_shapes=[pltpu.VMEM((n_heads, tq, LANES), F32)] * 3,
        compiler_params=pltpu.CompilerParams(dimension_semantics=("arbitrary", "arbitrary"),
                                             vmem_limit_bytes=VMEM_LIMIT_BYTES),
        name="attn",
    )(q, k, v, km, vm)


def _mid_kernel(yp_ref, ym_ref, x_ref, woa_ref, wob_ref, fg_ref, wr_ref, br_ref,
                h1_ref, xn4_ref, tokmeta_ref, tokmeta_t_ref, counts_ref, run_ref, *, tm):
    i = pl.program_id(0)

    @pl.when(i == 0)
    def _():
        run_ref[...] = jnp.zeros_like(run_ref)

    h1 = x_ref[...] + _dot(yp_ref[...], woa_ref[...]) + _dot(ym_ref[...], wob_ref[...])
    h1_ref[...] = h1
    xn = _rms(h1, fg_ref[...])
    for c in range(xn4_ref.shape[1]):
        xn4_ref[:, c] = xn[:, c * LANES:(c + 1) * LANES].reshape(tm // SUBLANES, SUBLANES, LANES)
    xb = xn.astype(BF16)

    kh = xb.shape[1] // 2
    logits = _dot(xb[:, :kh], wr_ref[0:kh, :]) + _dot(xb[:, kh:], wr_ref[kh:2 * kh, :]) + br_ref[...]
    lane = lax.broadcasted_iota(I32, logits.shape, 1).astype(F32)
    work = logits
    idxs, vals = [], []
    for _ in range(TOP_K):
        mx = jnp.max(work, axis=1, keepdims=True)
        ix = jnp.min(jnp.where(work == mx, lane, float(LANES)), axis=1, keepdims=True)
        idxs.append(ix)
        vals.append(mx)
        work = jnp.where(lane == ix, -jnp.inf, work)
    es = [jnp.exp(vv - vals[0]) for vv in vals]
    den = es[0]
    for e in es[1:]:
        den = den + e
    hot = [jnp.where(lane == ix, 1.0, 0.0) for ix in idxs]
    cnt = hot[0]
    for hh in hot[1:]:
        cnt = cnt + hh
    rr = lax.broadcasted_iota(I32, (tm, tm), 0)
    cc = lax.broadcasted_iota(I32, (tm, tm), 1)
    ltri = jnp.where(rr > cc, 1.0, 0.0).astype(BF16)
    base = run_ref[0:1, :] + _dot(ltri, cnt.astype(BF16))
    out = jnp.zeros(logits.shape, F32)
    for k in range(TOP_K):
        rank = jnp.sum(hot[k] * base, axis=1, keepdims=True)
        out = jnp.where(lane == float(k), idxs[k], out)
        out = jnp.where(lane == float(TOP_K + k), rank, out)
        out = jnp.where(lane == float(2 * TOP_K + k), es[k] / den, out)
    tokmeta_ref[...] = out
    tokmeta_t_ref[...] = jnp.transpose(out)[0:tokmeta_t_ref.shape[0], :]
    run = run_ref[...] + jnp.sum(cnt, axis=0, keepdims=True)
    run_ref[...] = run
    counts_ref[...] = run


def _mid(yp, ym, x2, woa, wob, fg, wr, br, *, tm):
    t, d = x2.shape
    pw = yp.shape[1]
    mw = ym.shape[1]
    nt = d // LANES
    kern = functools.partial(_mid_kernel, tm=tm)
    row = lambda i: (i, 0)
    return pl.pallas_call(
        kern, grid=(t // tm,),
        in_specs=[
            pl.BlockSpec((tm, pw), row), pl.BlockSpec((tm, mw), row), pl.BlockSpec((tm, d), row),
            _const_spec(woa.shape), _const_spec(wob.shape), _const_spec(fg.shape),
            _const_spec(wr.shape), _const_spec(br.shape),
        ],
        out_specs=(
            pl.BlockSpec((tm, d), row), pl.BlockSpec((tm // SUBLANES, nt, SUBLANES, LANES), lambda i: (i, 0, 0, 0)),
            pl.BlockSpec((tm, LANES), row), pl.BlockSpec((2 * SUBLANES, tm), lambda i: (0, i)),
            pl.BlockSpec((8, LANES), lambda i: (0, 0)),
        ),
        out_shape=(
            jax.ShapeDtypeStruct((t, d), F32), jax.ShapeDtypeStruct((t // SUBLANES, nt, SUBLANES, LANES), F32),
            jax.ShapeDtypeStruct((t, LANES), F32), jax.ShapeDtypeStruct((2 * SUBLANES, t), F32),
            jax.ShapeDtypeStruct((8, LANES), F32),
        ),
        scratch_shapes=[pltpu.VMEM((8, LANES), F32)],
        compiler_params=pltpu.CompilerParams(dimension_semantics=("arbitrary",),
                                             vmem_limit_bytes=VMEM_LIMIT_BYTES),
        name="mid",
    )(yp, ym, x2, woa, wob, fg, wr, br)


ZERO_GROUPS = EXPERT_BLOCK // (2 * SUBLANES)


def _dispatch_kernel(zstart_ref, zlen_ref, tail_ref, *refs, td, n_exp):
    dest_refs = refs[:TOP_K]
    xn4_ref, xs_ref, zbuf, sem, sem_z = refs[TOP_K:]

    def body(r, c):
        for k in range(TOP_K):
            pltpu.make_async_copy(_row_of(xn4_ref, r), _row_of(xs_ref, dest_refs[k][r]), sem).start(
                priority=k % 2)
        return c

    lax.fori_loop(0, td, body, 0, unroll=8)
    for _ in range(TOP_K):
        pltpu.make_async_copy(xn4_ref, xn4_ref, sem).wait()

    @pl.when(pl.program_id(0) == pl.num_programs(0) - 1)
    def _zero_fill():
        zbuf[...] = jnp.zeros_like(zbuf)
        tail0 = tail_ref[0]
        n_tail = tail_ref[1]

        def pad_copies(e, wait):
            zs = zstart_ref[e]
            zl = zlen_ref[e]
            head = jnp.minimum((-zs) & (SUBLANES - 1), zl)
            for h in range(SUBLANES - 1):
                @pl.when(h < head)
                def _(h=h):
                    cp = pltpu.make_async_copy(_row_of(zbuf, 0), _row_of(xs_ref, zs + h), sem_z)
                    cp.wait() if wait else cp.start()
            g0 = lax.shift_right_logical(zs + head, 3)
            ng = lax.shift_right_logical(zl - head, 3)
            v = ZERO_GROUPS
            while v >= 1:
                @pl.when((ng & v) != 0)
                def _(v=v):
                    off = g0 + (ng & (-2 * v))
                    cp = pltpu.make_async_copy(zbuf.at[pl.ds(0, v)], xs_ref.at[pl.ds(off, v)], sem_z)
                    cp.wait() if wait else cp.start()
                v //= 2

        def tail_copy(i, wait):
            cp = pltpu.make_async_copy(zbuf, xs_ref.at[pl.ds(tail0 + i * ZERO_GROUPS, ZERO_GROUPS)], sem_z)
            cp.wait() if wait else cp.start()

        for wait in (False, True):
            lax.fori_loop(0, n_exp, lambda e, c, wait=wait: (pad_copies(e, wait), c)[1], 0)
            lax.fori_loop(0, n_tail, lambda i, c, wait=wait: (tail_copy(i, wait), c)[1], 0)


def _dispatch(zstart, zlen, tail, dest_flat, xn4, p_rows, *, td):
    tg, nt, _, _ = xn4.shape
    n_exp = zstart.shape[0]
    kern = functools.partial(_dispatch_kernel, td=td, n_exp=n_exp)
    nb = tg * SUBLANES // td
    grid_spec = pltpu.PrefetchScalarGridSpec(
        num_scalar_prefetch=3, grid=(nb,),
        in_specs=[pl.BlockSpec((td,), lambda i, *_, k=k: (k * nb + i,), memory_space=pltpu.SMEM)
                  for k in range(TOP_K)] + [
            pl.BlockSpec((td // SUBLANES, nt, SUBLANES, LANES), lambda i, *_: (i, 0, 0, 0)),
        ],
        out_specs=pl.BlockSpec(memory_space=pl.ANY),
        scratch_shapes=[pltpu.VMEM((ZERO_GROUPS, nt, SUBLANES, LANES), F32), pltpu.SemaphoreType.DMA,
                        pltpu.SemaphoreType.DMA],
    )
    return pl.pallas_call(
        kern, grid_spec=grid_spec,
        out_shape=jax.ShapeDtypeStruct((p_rows // SUBLANES, nt, SUBLANES, LANES), F32),
        compiler_params=pltpu.CompilerParams(dimension_semantics=("arbitrary",),
                                             vmem_limit_bytes=VMEM_LIMIT_BYTES),
        name="dispatch",
    )(zstart, zlen, tail, *([dest_flat] * TOP_K), xn4)


def _expert_kernel(item_e_ref, item_row_ref, item_n_ref, item_valid_ref,
                   xs_ref, w1_ref, b1_ref, w2_ref, b2_ref, ys_ref,
                   xbuf, abuf, stage, xpre, wperm, w2b, hcbuf, sem_s, sem_p, *, rb, j1, tw, nt):
    del item_e_ref, item_valid_ref, xs_ref
    w = pl.program_id(0)
    j = pl.program_id(1)
    nrows = item_n_ref[w]
    row0 = pl.multiple_of(item_row_ref[w], rb)
    nrb = nrows // rb
    d = w2_ref.shape[1]
    tf = w2_ref.shape[0]

    def rows(i):
        return pl.ds(pl.multiple_of(i * rb, rb), rb)

    g0 = lax.shift_right_logical(row0, 3)
    gb = rb // SUBLANES

    def stage_copy(i, slot, to_hbm):
        hbm = ys_ref.at[pl.ds(g0 + i * gb, gb)]
        vmem = stage.at[slot]
        return pltpu.make_async_copy(vmem, hbm, sem_s.at[slot]) if to_hbm else pltpu.make_async_copy(
            hbm, vmem, sem_s.at[slot])

    def first_block_copy(item):
        g = lax.shift_right_logical(pl.multiple_of(item_row_ref[item], rb), 3)
        return pltpu.make_async_copy(ys_ref.at[pl.ds(g, gb)], xpre, sem_p)

    def slot_of(jj, i):
        return (jj * nrb + i) & 1

    def dot1(i, x):
        hcbuf[slot_of(j, i)] = _dot(x, w1_ref[...].astype(BF16)) + b1_ref[...]

    lane = lax.broadcasted_iota(I32, (rb, LANES), 1)
    even = (lane & 1) == 0

    def tail1(jj, i):
        hc = hcbuf[slot_of(jj, i)]
        outs = []
        for q in range(tw // (2 * LANES)):
            c0 = hc[:, 2 * q * LANES:(2 * q + 1) * LANES]
            c1 = hc[:, (2 * q + 1) * LANES:(2 * q + 2) * LANES]
            glu = jnp.where(even, c0, pltpu.roll(c1, 1, 1))
            lin = jnp.where(even, pltpu.roll(c0, LANES - 1, 1), c1)
            glu = jnp.minimum(glu, SWIGLU_LIMIT)
            lin = jnp.clip(lin, -SWIGLU_LIMIT, SWIGLU_LIMIT)
            act = glu * (1.0 / (1.0 + jnp.exp(-SWIGLU_ALPHA * glu))) * (lin + 1.0)
            outs.append(act.astype(BF16))
        abuf[jj, rows(i), :] = jnp.concatenate(outs, axis=1)

    def permute_w2_tile():
        for s in range(d // LANES):
            for g in range(tf // LANES):
                top = w2_ref[g * LANES:g * LANES + LANES // 2, s * LANES:(s + 1) * LANES]
                bot = w2_ref[g * LANES + LANES // 2:(g + 1) * LANES, s * LANES:(s + 1) * LANES]
                wperm.at[s][pl.ds(g * LANES, LANES // 2, stride=2), :] = top
                wperm.at[s][pl.ds(g * LANES + 1, LANES // 2, stride=2), :] = bot
        k0 = pl.multiple_of(j * tf, tf)
        for s in range(d // LANES):
            w2b[pl.ds(k0, tf), s * LANES:(s + 1) * LANES] = wperm[s].astype(BF16)

    @pl.when((j == 0) & (nrows > 0))
    def _first_step():
        def fetch(i):
            stage_copy(i, i & 1, False).wait()

            @pl.when(i + 1 < nrb)
            def _():
                stage_copy(i + 1, (i + 1) & 1, False).start()

        def load_and_dot(i, src=None):
            blk = stage.at[i & 1] if src is None else src
            x = jnp.concatenate([blk[:, c].reshape(rb, LANES) for c in range(nt)], axis=1).astype(BF16)
            xbuf[rows(i), :] = x
            dot1(i, x)

        @pl.when(w == 0)
        def _():
            stage_copy(0, 0, False).start()
            fetch(0)
            load_and_dot(0)
            permute_w2_tile()

        @pl.when(w > 0)
        def _():
            first_block_copy(w).wait()

            @pl.when(1 < nrb)
            def _():
                stage_copy(1, 1, False).start()

            load_and_dot(0, xpre)
            permute_w2_tile()

        def body(i, c):
            fetch(i)
            tail1(0, i - 1)
            load_and_dot(i)
            return c

        lax.fori_loop(1, nrb, body, 0)

    @pl.when((j > 0) & (j < j1) & (nrows > 0))
    def _next_steps():
        tail1(j - 1, nrb - 1)
        dot1(0, xbuf[rows(0), :])
        permute_w2_tile()

        def body(i, c):
            tail1(j, i - 1)
            dot1(i, xbuf[rows(i), :])
            return c

        lax.fori_loop(1, nrb, body, 0)

    @pl.when((j == j1) & (nrows > 0))
    def _last_step():
        nxt = jnp.minimum(w + 1, pl.num_programs(0) - 1)

        @pl.when((w + 1 < pl.num_programs(0)) & (item_n_ref[nxt] > 0))
        def _():
            first_block_copy(nxt).start()

        tail1(j1 - 1, nrb - 1)

        def body(i, c):
            @pl.when(i >= 2)
            def _():
                stage_copy(i - 2, i & 1, True).wait()

            @pl.when(i >= 1)
            def _():
                stage_copy(i - 1, (i - 1) & 1, True).start()

            a = jnp.concatenate([abuf[jj, rows(i), :] for jj in range(j1)], axis=1)
            y = _dot(a, w2b[...]) + b2_ref[...]
            for c in range(nt):
                stage[i & 1, :, c] = y[:, c * LANES:(c + 1) * LANES].reshape(gb, SUBLANES, LANES)
            return c

        lax.fori_loop(0, nrb, body, 0)

        @pl.when(nrb >= 2)
        def _():
            stage_copy(nrb - 2, nrb & 1, True).wait()

        last = stage_copy(nrb - 1, (nrb - 1) & 1, True)
        last.start()
        last.wait()


def _experts(item_e, item_row, item_n, item_valid, xs, w1, b1, w2, b2, *, r_max, tw):
    n_exp, d, f2 = w1.shape
    f_dim = w2.shape[1]
    nt = xs.shape[1]
    j1 = f2 // tw
    tf = f_dim // j1
    assert tf == tw // 2 and tf % LANES == 0 and nt * LANES == d
    rb = EXPERT_BLOCK
    kern = functools.partial(_expert_kernel, rb=rb, j1=j1, tw=tw, nt=nt)

    def w_step(w, j, iv):
        return jnp.minimum(jnp.where(iv[w] == 1, j, j1), j1 - 1)

    def w1_map(w, j, ie, ir, inn, iv):
        return (ie[w], 0, w_step(w, j, iv))

    def w2_map(w, j, ie, ir, inn, iv):
        return (ie[w], w_step(w, j, iv), 0)

    def e_map(w, j, ie, ir, inn, iv):
        return (ie[w], 0, 0)

    grid_spec = pltpu.PrefetchScalarGridSpec(
        num_scalar_prefetch=4, grid=(item_e.shape[0], j1 + 1),
        in_specs=[
            pl.BlockSpec(memory_space=pl.ANY),
            pl.BlockSpec((None, d, tw), w1_map),
            pl.BlockSpec((None, 1, tw), w1_map),
            pl.BlockSpec((None, tf, d), w2_map),
            pl.BlockSpec((None, 1, d), e_map),
        ],
        out_specs=pl.BlockSpec(memory_space=pl.ANY),
        scratch_shapes=[
            pltpu.VMEM((r_max, d), BF16),
            pltpu.VMEM((j1, r_max, tf), BF16),
            pltpu.VMEM((2, rb // SUBLANES, nt, SUBLANES, LANES), F32),
            pltpu.VMEM((rb // SUBLANES, nt, SUBLANES, LANES), F32),
            pltpu.VMEM((d // LANES, tf, LANES), F32),
            pltpu.VMEM((f_dim, d), BF16),
            pltpu.VMEM((2, rb, tw), F32),
            pltpu.SemaphoreType.DMA((2,)),
            pltpu.SemaphoreType.DMA,
        ],
    )
    return pl.pallas_call(
        kern, grid_spec=grid_spec,
        out_shape=jax.ShapeDtypeStruct(xs.shape, F32),
        input_output_aliases={4: 0},
        compiler_params=pltpu.CompilerParams(dimension_semantics=("arbitrary", "arbitrary"),
                                             vmem_limit_bytes=VMEM_LIMIT_BYTES),
        name="experts",
    )(item_e, item_row, item_n, item_valid, xs, w1, b1.reshape(n_exp, 1, f2), w2, b2.reshape(n_exp, 1, d))


def _combine_kernel(*refs, tc, nt):
    dest_refs = refs[:TOP_K]
    tokmeta_ref, h1_ref, fg_ref, ys_ref, o_ref, buf, ssq_ref, sem = refs[TOP_K:]
    i = pl.program_id(0)
    last = pl.num_programs(0) - 1
    slot = i & 1
    d = nt * LANES

    per_row = nt // SUBLANES

    def loop(do_sum, do_issue):
        def body(g, c):
            r8 = pl.ds(pl.multiple_of(g * SUBLANES, SUBLANES), SUBLANES)
            if do_sum:
                tm = tokmeta_ref[r8, :]
                gates = [tm[:, 2 * TOP_K + k:2 * TOP_K + k + 1] for k in range(TOP_K)]
                ssq = jnp.zeros((SUBLANES, LANES), F32)
            for s in range(SUBLANES):
                if do_issue:
                    for k in range(TOP_K):
                        src = _row_of(ys_ref, dest_refs[k][g * SUBLANES + s])
                        pltpu.make_async_copy(src, buf.at[slot, k, g, :, s, :], sem.at[slot]).start(
                            priority=k % 2)
                if do_sum:
                    for cc in range(s * per_row, (s + 1) * per_row):
                        a = h1_ref[r8, cc * LANES:(cc + 1) * LANES]
                        for k in range(TOP_K):
                            a = a + gates[k] * buf[1 - slot, k, g, cc]
                        o_ref[r8, cc * LANES:(cc + 1) * LANES] = a
                        ssq = ssq + a * a
            if do_sum:
                ssq_ref[r8, :] = ssq
            return c

        lax.fori_loop(0, tc // SUBLANES, body, 0)
        if do_sum:
            inv = lax.rsqrt(jnp.sum(ssq_ref[...], axis=1, keepdims=True) * (1.0 / d) + EPS)
            o_ref[...] = o_ref[...] * inv * fg_ref[...]

    @pl.when(i > 0)
    def _():
        for k in range(TOP_K):
            pltpu.make_async_copy(buf.at[1 - slot, k], buf.at[1 - slot, k], sem.at[1 - slot]).wait()

    @pl.when(i == 0)
    def _():
        loop(False, True)

    @pl.when((i > 0) & (i < last))
    def _():
        loop(True, True)

    @pl.when(i == last)
    def _():
        loop(True, False)


def _combine(dest_flat, tokmeta, h1, fg, ys, *, tc):
    t, d = h1.shape
    nt = ys.shape[1]
    nb = t // tc
    kern = functools.partial(_combine_kernel, tc=tc, nt=nt)
    prev = lambda i: (jnp.maximum(i - 1, 0), 0)
    return pl.pallas_call(
        kern, grid=(nb + 1,),
        in_specs=[pl.BlockSpec((tc,), lambda i, k=k: (k * nb + jnp.minimum(i, nb - 1),), memory_space=pltpu.SMEM)
                  for k in range(TOP_K)] + [
            pl.BlockSpec((tc, LANES), prev), pl.BlockSpec((tc, d), prev), _const_spec(fg.shape),
            pl.BlockSpec(memory_space=pl.ANY),
        ],
        out_specs=pl.BlockSpec((tc, d), prev),
        out_shape=jax.ShapeDtypeStruct((t, d), F32),
        scratch_shapes=[pltpu.VMEM((2, TOP_K, tc // SUBLANES, nt, SUBLANES, LANES), F32),
                        pltpu.VMEM((tc, LANES), F32), pltpu.SemaphoreType.DMA((2,))],
        compiler_params=pltpu.CompilerParams(dimension_semantics=("arbitrary",),
                                             vmem_limit_bytes=VMEM_LIMIT_BYTES),
        name="combine",
    )(*([dest_flat] * TOP_K), tokmeta, h1, fg, ys)


def _rot_cols(w):
    h = QK_ROPE // 2
    return jnp.concatenate([-w[..., h:], w[..., :h]], axis=-1)


def _tile_rows(n, cap):
    t = min(n, cap)
    assert n % t == 0, (n, cap)
    return t


def kernel(x, meta_tokens, attn_norm_g, w_in, q_norm_g, w_uq, kv_norm_g, w_ukv, pool_w, pool_scale, w_o,
           ffn_norm_g, w_router, b_router, w1, b1, w2, b2, final_norm_g):
    nb, s_len, d = x.shape
    n_meta = meta_tokens.shape[0]
    assert w_in.shape[0] == 1, "one layer"
    assert n_meta == HALO and max(POOL_WINDOWS) - 1 <= HALO
    pw = pool_scale.shape[1]
    q_lora = q_norm_g.shape[1]
    kv_lora = kv_norm_g.shape[1]
    n_heads = w_uq.shape[2] // (QK_NOPE + QK_ROPE)
    n_exp = w_router.shape[2]
    f_dim = w2.shape[2]
    t = nb * s_len
    assert s_len % CHUNK == 0 and n_exp <= LANES and pw // len(POOL_WINDOWS) % LANES == 0

    win = w_in[0]
    o = pw + q_lora + kv_lora
    w_kr = win[:, o:o + QK_ROPE]
    zc = jnp.zeros((d, LANES - QK_ROPE), F32)
    win_b = jnp.concatenate([win[:, :o], w_kr, zc, _rot_cols(w_kr), zc], axis=1).astype(BF16)
    wq3 = w_uq[0].reshape(q_lora, n_heads, QK_NOPE + QK_ROPE)
    zq = jnp.zeros((q_lora, n_heads, LANES - QK_ROPE), F32)
    wq_b = jnp.concatenate([wq3, zq], axis=2).reshape(q_lora, n_heads * HEAD_PAD).astype(BF16)
    wqr_b = jnp.concatenate([_rot_cols(wq3[:, :, QK_NOPE:]), zq], axis=2).reshape(
        q_lora, n_heads * LANES).astype(BF16)
    wkv3 = w_ukv[0].reshape(kv_lora, n_heads, QK_NOPE + V_DIM)
    wk_b = wkv3[:, :, :QK_NOPE].reshape(kv_lora, n_heads * QK_NOPE).astype(BF16)
    wv_b = wkv3[:, :, QK_NOPE:].reshape(kv_lora, n_heads * V_DIM).astype(BF16)
    front_w = (attn_norm_g, win_b, pool_w[0].astype(BF16), pool_scale, q_norm_g, wq_b, wqr_b, kv_norm_g,
               wk_b, wv_b)
    woa = w_o[0, :pw].astype(BF16)
    wob = w_o[0, pw:].astype(BF16)
    wr_b = jnp.pad(w_router[0], ((0, 0), (0, LANES - n_exp))).astype(BF16)
    br = jnp.pad(b_router, ((0, 0), (0, LANES - n_exp)), constant_values=NEG_BIG)

    pos = jnp.arange(n_meta + s_len, dtype=F32)
    inv_freq = 1.0 / (ROPE_BASE ** (jnp.arange(0, QK_ROPE, 2, dtype=F32) / QK_ROPE))
    ang = pos[:, None] * inv_freq[None, :]
    ones = jnp.ones((n_meta + s_len, LANES - QK_ROPE), F32)
    cs = jnp.concatenate([jnp.cos(ang), jnp.cos(ang), ones], axis=1)
    sn = jnp.concatenate([jnp.sin(ang), jnp.sin(ang), 0.0 * ones], axis=1)

    zero_halo = jnp.zeros((HALO, pw), F32)
    _, _, k_meta, v_meta, p_meta = _front(meta_tokens[None], zero_halo, cs[:n_meta], sn[:n_meta], front_w,
                                          ts=n_meta)
    ts = _tile_rows(s_len, 512)
    y_pool, q, k, v, _ = _front(x, p_meta[0, 0], cs[n_meta:], sn[n_meta:], front_w, ts=ts)

    km = jnp.pad(k_meta[0], ((0, LANES - n_meta), (0, 0)))
    vm = jnp.pad(v_meta[0], ((0, LANES - n_meta), (0, 0)))
    y_mla = _attn(q, k, v, km, vm, tq=_tile_rows(s_len, 512), n_meta=n_meta)

    tm = _tile_rows(t, 512)
    h1, xn4, tokmeta, tokmeta_t, counts = _mid(y_pool.reshape(t, pw), y_mla.reshape(t, -1), x.reshape(t, d),
                                               woa, wob, ffn_norm_g, wr_b, br, tm=tm)

    r_max = 5 * EXPERT_BLOCK
    cnt = counts[0, :n_exp].astype(I32)
    padded = (cnt + EXPERT_BLOCK - 1) // EXPERT_BLOCK * EXPERT_BLOCK
    pad_end = jnp.cumsum(padded)
    pad_start = pad_end - padded
    n_assign = t * TOP_K
    p_rows = -(-(n_assign + n_exp * (EXPERT_BLOCK - 1)) // EXPERT_BLOCK) * EXPERT_BLOCK
    idx = tokmeta_t[0:TOP_K].astype(I32)
    rank = tokmeta_t[TOP_K:2 * TOP_K].astype(I32)
    start_of = jnp.zeros_like(idx)
    for e in range(n_exp):
        start_of = jnp.where(idx == e, pad_start[e], start_of)
    dest = (start_of + rank).reshape(-1)

    n_items = n_exp + p_rows // r_max
    per_e = (padded + r_max - 1) // r_max
    item_end = jnp.cumsum(per_e)
    total = item_end[-1]
    wi = jnp.arange(n_items, dtype=I32)
    valid = wi < total
    wc = jnp.minimum(wi, total - 1)
    ie = jnp.minimum(jnp.sum((item_end[None, :] <= wc[:, None]).astype(I32), axis=1), n_exp - 1)
    local = wc - (item_end[ie] - per_e[ie])
    item_row = jnp.where(valid, pad_start[ie] + local * r_max, 0).astype(I32)
    item_n = jnp.where(valid, jnp.clip(padded[ie] - local * r_max, 0, r_max), 0).astype(I32)

    zstart = (pad_start + cnt).astype(I32)
    zlen = (padded - cnt).astype(I32)
    tail = jnp.stack([pad_end[-1] // SUBLANES, (p_rows - pad_end[-1]) // (SUBLANES * ZERO_GROUPS)]).astype(I32)
    xs = _dispatch(zstart, zlen, tail, dest, xn4, p_rows, td=_tile_rows(t, 1024))
    ys = _experts(ie, item_row, item_n, valid.astype(I32), xs, w1[0], b1[0], w2[0], b2[0],
                  r_max=r_max, tw=min(512, 2 * f_dim))
    out = _combine(dest, tokmeta, h1, final_norm_g.reshape(1, d), ys, tc=_tile_rows(t, 256))
    return out.reshape(nb, s_len, d)
```

```python
import functools

import jax
import jax.numpy as jnp
from jax import lax
from jax.experimental import pallas as pl
from jax.experimental.pallas import tpu as pltpu

F32 = jnp.float32
BF16 = jnp.bfloat16
I32 = jnp.int32

CHUNK = 64
POOL_WINDOWS = (2, 4, 8, 16)
V_DIM = 128
QK_NOPE = 128
QK_ROPE = 64
ROPE_BASE = 10000.0
TOP_K = 4
SWIGLU_LIMIT = 7.0
SWIGLU_ALPHA = 1.702
EPS = 1e-5
EXPERT_BLOCK = 512

LANES = 128
SUBLANES = 8
HEAD_PAD = 2 * LANES
VMEM_LIMIT_BYTES = 58 * 1024 * 1024

HALO = 16
NEG_BIG = -1e30


def _rms(x, g):
    ms = jnp.mean(x * x, axis=-1, keepdims=True)
    return x * lax.rsqrt(ms + EPS) * g


def _dot(a, b):
    return jnp.dot(a, b, preferred_element_type=F32)


def _dot_nt(a, b):
    return lax.dot_general(a, b, (((1,), (1,)), ((), ())), preferred_element_type=F32)


def _const_spec(shape):
    nd = len(shape)
    return pl.BlockSpec(shape, lambda *_: (0,) * nd)


def _row_of(ref, r):
    if isinstance(r, int):
        return ref.at[r // SUBLANES, :, r % SUBLANES, :]
    return ref.at[lax.shift_right_logical(r, 3), :, r & (SUBLANES - 1), :]


def _front_kernel(x_ref, mpool_ref, cs_ref, sn_ref, ag_ref, win_ref, pw_ref, ps_ref, qg_ref, wq_ref,
                  wqr_ref, kg_ref, wk_ref, wv_ref,
                  ypool_ref, q_ref, k_ref, v_ref, ptail_ref, ext_ref, lv_ref, *, ts, pool_w, q_lora, kv_lora,
                  n_heads):
    st = pl.program_id(1)
    hn = _rms(x_ref[0], ag_ref[...]).astype(BF16)
    proj = _dot(hn, win_ref[...])
    pool_in = proj[:, :pool_w]

    lo = SUBLANES
    hi = SUBLANES + HALO + ts

    @pl.when(st == 0)
    def _():
        ext_ref[0:lo, :] = jnp.zeros((lo, pool_w), F32)
        ext_ref[lo:lo + HALO, :] = mpool_ref[...]
        lv_ref[:, 0:lo, :] = jnp.zeros((2, lo, lv_ref.shape[2]), F32)

    ext_ref[lo + HALO:hi, :] = pool_in
    gw = pool_w // len(POOL_WINDOWS)
    for g, w in enumerate(POOL_WINDOWS):
        c0 = g * gw
        u = pool_in[:, c0:c0 + gw]
        cur = ext_ref[lo:hi, c0:c0 + gw]
        src, m = None, 1
        while m < w:
            prev = ext_ref[lo - m:hi - m, c0:c0 + gw] if src is None else lv_ref[src, lo - m:hi - m, :]
            cur = cur + prev
            m *= 2
            if m < w:
                src = 0 if src != 0 else 1
                lv_ref[src, lo:hi, :] = cur
        s = cur[HALO:, :]
        d = (s * (1.0 / w) - u).astype(BF16)
        y = _dot(d, pw_ref[g]) * ps_ref[:, c0:c0 + gw]
        ypool_ref[0, :, c0:c0 + gw] = y.astype(BF16)
    tail = pool_in[ts - HALO:ts, :]
    ext_ref[lo:lo + HALO, :] = tail
    ptail_ref[0, 0] = tail

    o = pool_w
    q_c = proj[:, o:o + q_lora]
    o += q_lora
    kv_c = proj[:, o:o + kv_lora]
    o += kv_lora
    kr = proj[:, o:o + LANES]
    kr_rot = proj[:, o + LANES:o + 2 * LANES]
    cs = cs_ref[...]
    sn = sn_ref[...]
    krope = (kr * cs + kr_rot * sn).astype(BF16)
    qn = _rms(q_c, qg_ref[...]).astype(BF16)
    qm = _dot(qn, wq_ref[...])
    qr = _dot(qn, wqr_ref[...])
    kvn = _rms(kv_c, kg_ref[...]).astype(BF16)
    kn = _dot(kvn, wk_ref[...])
    v_ref[0] = _dot(kvn, wv_ref[...]).astype(BF16)
    for h in range(n_heads):
        a = h * HEAD_PAD
        b = h * LANES
        q_ref[0, :, a:a + LANES] = qm[:, a:a + LANES].astype(BF16)
        q_ref[0, :, a + LANES:a + HEAD_PAD] = (
            qm[:, a + LANES:a + HEAD_PAD] * cs + qr[:, b:b + LANES] * sn).astype(BF16)
        k_ref[0, :, a:a + LANES] = kn[:, b:b + LANES].astype(BF16)
        k_ref[0, :, a + LANES:a + HEAD_PAD] = krope


def _front(x3, mpool, cs, sn, wts, *, ts):
    nb, s_len, d = x3.shape
    (ag, win, pw, ps, qg, wq, wqr, kg, wk, wv) = wts
    pool_w = ps.shape[1]
    q_lora = qg.shape[1]
    kv_lora = kg.shape[1]
    n_heads = wk.shape[1] // LANES
    n_st = s_len // ts
    kern = functools.partial(_front_kernel, ts=ts, pool_w=pool_w, q_lora=q_lora, kv_lora=kv_lora,
                             n_heads=n_heads)
    row = lambda b, s: (b, s, 0)
    in_specs = [
        pl.BlockSpec((1, ts, d), row),
        _const_spec(mpool.shape),
        pl.BlockSpec((ts, LANES), lambda b, s: (s, 0)),
        pl.BlockSpec((ts, LANES), lambda b, s: (s, 0)),
    ] + [_const_spec(w.shape) for w in wts]
    out_shape = (
        jax.ShapeDtypeStruct((nb, s_len, pool_w), BF16),
        jax.ShapeDtypeStruct((nb, s_len, n_heads * HEAD_PAD), BF16),
        jax.ShapeDtypeStruct((nb, s_len, n_heads * HEAD_PAD), BF16),
        jax.ShapeDtypeStruct((nb, s_len, n_heads * V_DIM), BF16),
        jax.ShapeDtypeStruct((nb, n_st, HALO, pool_w), F32),
    )
    out_specs = (
        pl.BlockSpec((1, ts, pool_w), row),
        pl.BlockSpec((1, ts, n_heads * HEAD_PAD), row),
        pl.BlockSpec((1, ts, n_heads * HEAD_PAD), row),
        pl.BlockSpec((1, ts, n_heads * V_DIM), row),
        pl.BlockSpec((1, 1, HALO, pool_w), lambda b, s: (b, s, 0, 0)),
    )
    return pl.pallas_call(
        kern, grid=(nb, n_st), in_specs=in_specs, out_specs=out_specs, out_shape=out_shape,
        scratch_shapes=[pltpu.VMEM((SUBLANES + HALO + ts, pool_w), F32),
                        pltpu.VMEM((2, SUBLANES + HALO + ts, pool_w // len(POOL_WINDOWS)), F32)],
        compiler_params=pltpu.CompilerParams(dimension_semantics=("arbitrary", "arbitrary"),
                                             vmem_limit_bytes=VMEM_LIMIT_BYTES),
        name="front",
    )(x3, mpool, cs, sn, *wts)


def _attn_kernel(q_ref, k_ref, v_ref, km_ref, vm_ref, o_ref, m_ref, l_ref, acc_ref, *, tq, n_meta, n_heads,
                 scale):
    qi = pl.program_id(1)
    c2 = scale * 1.4426950408889634

    def update(h, s, vb, first):
        s_max = jnp.max(s, axis=1, keepdims=True)
        if first:
            m_new = jnp.broadcast_to(s_max, (tq, LANES))
        else:
            m_old = m_ref[h]
            m_new = jnp.maximum(m_old, s_max)
            alpha = jnp.exp2((m_old - m_new) * c2)
        p = jnp.exp2((s - jnp.concatenate([m_new] * (s.shape[1] // LANES), axis=1)) * c2)
        v1 = jnp.concatenate([vb, jnp.ones(vb.shape, BF16)], axis=1)
        pv = _dot(p.astype(BF16), v1)
        if first:
            l_ref[h] = pv[:, V_DIM:]
            acc_ref[h] = pv[:, :V_DIM]
        else:
            l_ref[h] = alpha * l_ref[h] + pv[:, V_DIM:]
            acc_ref[h] = alpha * acc_ref[h] + pv[:, :V_DIM]
        m_ref[h] = m_new

    def q_of(h):
        return q_ref[0, :, h * HEAD_PAD:(h + 1) * HEAD_PAD]

    r_diag = pl.multiple_of(qi * tq, tq)
    rc = lax.broadcasted_iota(I32, (tq, tq), 0) // CHUNK
    cc = lax.broadcasted_iota(I32, (tq, tq), 1) // CHUNK
    vis = jnp.concatenate([cc <= rc, lax.broadcasted_iota(I32, (tq, LANES), 1) < n_meta], axis=1)
    for h in range(n_heads):
        kd = jnp.concatenate([k_ref[0, pl.ds(r_diag, tq), h * HEAD_PAD:(h + 1) * HEAD_PAD],
                              km_ref[:, h * HEAD_PAD:(h + 1) * HEAD_PAD]], axis=0)
        vd = jnp.concatenate([v_ref[0, pl.ds(r_diag, tq), h * V_DIM:(h + 1) * V_DIM],
                              vm_ref[:, h * V_DIM:(h + 1) * V_DIM]], axis=0)
        s = jnp.where(vis, _dot_nt(q_of(h), kd), -jnp.inf)
        update(h, s, vd, True)

    def body(j, c):
        r0 = pl.multiple_of(j * tq, tq)
        for h in range(n_heads):
            s = _dot_nt(q_of(h), k_ref[0, pl.ds(r0, tq), h * HEAD_PAD:(h + 1) * HEAD_PAD])
            update(h, s, v_ref[0, pl.ds(r0, tq), h * V_DIM:(h + 1) * V_DIM], False)
        return c

    lax.fori_loop(0, qi, body, 0)
    for h in range(n_heads):
        o_ref[0, :, h * V_DIM:(h + 1) * V_DIM] = (acc_ref[h] / l_ref[h]).astype(BF16)


def _attn(q, k, v, km, vm, *, tq, n_meta):
    nb, s_len, hw = q.shape
    n_heads = hw // HEAD_PAD
    kern = functools.partial(_attn_kernel, tq=tq, n_meta=n_meta, n_heads=n_heads,
                             scale=float((QK_NOPE + QK_ROPE) ** -0.5))
    return pl.pallas_call(
        kern, grid=(nb, s_len // tq),
        in_specs=[
            pl.BlockSpec((1, tq, hw), lambda b, i: (b, i, 0)),
            pl.BlockSpec((1, s_len, hw), lambda b, i: (b, 0, 0)),
            pl.BlockSpec((1, s_len, n_heads * V_DIM), lambda b, i: (b, 0, 0)),
            _const_spec(km.shape),
            _const_spec(vm.shape),
        ],
        out_specs=pl.BlockSpec((1, tq, n_heads * V_DIM), lambda b, i: (b, i, 0)),
        out_shape=jax.ShapeDtypeStruct((nb, s_len, n_heads * V_DIM), BF16),
        scratch_shapes=[pltpu.VMEM((n_heads, tq, LANES), F32)] * 3,
        compiler_params=pltpu.CompilerParams(dimension_semantics=("arbitrary", "arbitrary"),
                                             vmem_limit_bytes=VMEM_LIMIT_BYTES),
        name="attn",
    )(q, k, v, km, vm)


def _mid_kernel(yp_ref, ym_ref, x_ref, woa_ref, wob_ref, fg_ref, wr_ref, br_ref,
                h1_ref, xn4_ref, tokmeta_ref, tokmeta_t_ref, counts_ref, run_ref, *, tm):
    i = pl.program_id(0)

    @pl.when(i == 0)
    def _():
        run_ref[...] = jnp.zeros_like(run_ref)

    h1 = x_ref[...] + _dot(yp_ref[...], woa_ref[...]) + _dot(ym_ref[...], wob_ref[...])
    h1_ref[...] = h1
    xn = _rms(h1, fg_ref[...])
    for c in range(xn4_ref.shape[1]):
        xn4_ref[:, c] = xn[:, c * LANES:(c + 1) * LANES].reshape(tm // SUBLANES, SUBLANES, LANES)
    xb = xn.astype(BF16)

    kh = xb.shape[1] // 2
    logits = _dot(xb[:, :kh], wr_ref[0:kh, :]) + _dot(xb[:, kh:], wr_ref[kh:2 * kh, :]) + br_ref[...]
    lane = lax.broadcasted_iota(I32, logits.shape, 1).astype(F32)
    work = logits
    idxs, vals = [], []
    for _ in range(TOP_K):
        mx = jnp.max(work, axis=1, keepdims=True)
        ix = jnp.min(jnp.where(work == mx, lane, float(LANES)), axis=1, keepdims=True)
        idxs.append(ix)
        vals.append(mx)
        work = jnp.where(lane == ix, -jnp.inf, work)
    es = [jnp.exp(vv - vals[0]) for vv in vals]
    den = es[0]
    for e in es[1:]:
        den = den + e
    hot = [jnp.where(lane == ix, 1.0, 0.0) for ix in idxs]
    cnt = hot[0]
    for hh in hot[1:]:
        cnt = cnt + hh
    rr = lax.broadcasted_iota(I32, (tm, tm), 0)
    cc = lax.broadcasted_iota(I32, (tm, tm), 1)
    ltri = jnp.where(rr > cc, 1.0, 0.0).astype(BF16)
    base = run_ref[0:1, :] + _dot(ltri, cnt.astype(BF16))
    out = jnp.zeros(logits.shape, F32)
    for k in range(TOP_K):
        rank = jnp.sum(hot[k] * base, axis=1, keepdims=True)
        out = jnp.where(lane == float(k), idxs[k], out)
        out = jnp.where(lane == float(TOP_K + k), rank, out)
        out = jnp.where(lane == float(2 * TOP_K + k), es[k] / den, out)
    tokmeta_ref[...] = out
    tokmeta_t_ref[...] = jnp.transpose(out)[0:tokmeta_t_ref.shape[0], :]
    run = run_ref[...] + jnp.sum(cnt, axis=0, keepdims=True)
    run_ref[...] = run
    counts_ref[...] = run


def _mid(yp, ym, x2, woa, wob, fg, wr, br, *, tm):
    t, d = x2.shape
    pw = yp.shape[1]
    mw = ym.shape[1]
    nt = d // LANES
    kern = functools.partial(_mid_kernel, tm=tm)
    row = lambda i: (i, 0)
    return pl.pallas_call(
        kern, grid=(t // tm,),
        in_specs=[
            pl.BlockSpec((tm, pw), row), pl.BlockSpec((tm, mw), row), pl.BlockSpec((tm, d), row),
            _const_spec(woa.shape), _const_spec(wob.shape), _const_spec(fg.shape),
            _const_spec(wr.shape), _const_spec(br.shape),
        ],
        out_specs=(
            pl.BlockSpec((tm, d), row), pl.BlockSpec((tm // SUBLANES, nt, SUBLANES, LANES), lambda i: (i, 0, 0, 0)),
            pl.BlockSpec((tm, LANES), row), pl.BlockSpec((2 * SUBLANES, tm), lambda i: (0, i)),
            pl.BlockSpec((8, LANES), lambda i: (0, 0)),
        ),
        out_shape=(
            jax.ShapeDtypeStruct((t, d), F32), jax.ShapeDtypeStruct((t // SUBLANES, nt, SUBLANES, LANES), F32),
            jax.ShapeDtypeStruct((t, LANES), F32), jax.ShapeDtypeStruct((2 * SUBLANES, t), F32),
            jax.ShapeDtypeStruct((8, LANES), F32),
        ),
        scratch_shapes=[pltpu.VMEM((8, LANES), F32)],
        compiler_params=pltpu.CompilerParams(dimension_semantics=("arbitrary",),
                                             vmem_limit_bytes=VMEM_LIMIT_BYTES),
        name="mid",
    )(yp, ym, x2, woa, wob, fg, wr, br)


ZERO_GROUPS = EXPERT_BLOCK // (2 * SUBLANES)


def _dispatch_kernel(zstart_ref, zlen_ref, tail_ref, *refs, td, n_exp):
    dest_refs = refs[:TOP_K]
    xn4_ref, xs_ref, zbuf, sem, sem_z = refs[TOP_K:]

    def body(r, c):
        for k in range(TOP_K):
            pltpu.make_async_copy(_row_of(xn4_ref, r), _row_of(xs_ref, dest_refs[k][r]), sem).start(
                priority=k % 2)
        return c

    lax.fori_loop(0, td, body, 0, unroll=8)
    for _ in range(TOP_K):
        pltpu.make_async_copy(xn4_ref, xn4_ref, sem).wait()

    @pl.when(pl.program_id(0) == pl.num_programs(0) - 1)
    def _zero_fill():
        zbuf[...] = jnp.zeros_like(zbuf)
        tail0 = tail_ref[0]
        n_tail = tail_ref[1]

        def pad_copies(e, wait):
            zs = zstart_ref[e]
            zl = zlen_ref[e]
            head = jnp.minimum((-zs) & (SUBLANES - 1), zl)
            for h in range(SUBLANES - 1):
                @pl.when(h < head)
                def _(h=h):
                    cp = pltpu.make_async_copy(_row_of(zbuf, 0), _row_of(xs_ref, zs + h), sem_z)
                    cp.wait() if wait else cp.start()
            g0 = lax.shift_right_logical(zs + head, 3)
            ng = lax.shift_right_logical(zl - head, 3)
            v = ZERO_GROUPS
            while v >= 1:
                @pl.when((ng & v) != 0)
                def _(v=v):
                    off = g0 + (ng & (-2 * v))
                    cp = pltpu.make_async_copy(zbuf.at[pl.ds(0, v)], xs_ref.at[pl.ds(off, v)], sem_z)
                    cp.wait() if wait else cp.start()
                v //= 2

        def tail_copy(i, wait):
            cp = pltpu.make_async_copy(zbuf, xs_ref.at[pl.ds(tail0 + i * ZERO_GROUPS, ZERO_GROUPS)], sem_z)
            cp.wait() if wait else cp.start()

        for wait in (False, True):
            lax.fori_loop(0, n_exp, lambda e, c, wait=wait: (pad_copies(e, wait), c)[1], 0)
            lax.fori_loop(0, n_tail, lambda i, c, wait=wait: (tail_copy(i, wait), c)[1], 0)


def _dispatch(zstart, zlen, tail, dest_flat, xn4, p_rows, *, td):
    tg, nt, _, _ = xn4.shape
    n_exp = zstart.shape[0]
    kern = functools.partial(_dispatch_kernel, td=td, n_exp=n_exp)
    nb = tg * SUBLANES // td
    grid_spec = pltpu.PrefetchScalarGridSpec(
        num_scalar_prefetch=3, grid=(nb,),
        in_specs=[pl.BlockSpec((td,), lambda i, *_, k=k: (k * nb + i,), memory_space=pltpu.SMEM)
                  for k in range(TOP_K)] + [
            pl.BlockSpec((td // SUBLANES, nt, SUBLANES, LANES), lambda i, *_: (i, 0, 0, 0)),
        ],
        out_specs=pl.BlockSpec(memory_space=pl.ANY),
        scratch_shapes=[pltpu.VMEM((ZERO_GROUPS, nt, SUBLANES, LANES), F32), pltpu.SemaphoreType.DMA,
                        pltpu.SemaphoreType.DMA],
    )
    return pl.pallas_call(
        kern, grid_spec=grid_spec,
        out_shape=jax.ShapeDtypeStruct((p_rows // SUBLANES, nt, SUBLANES, LANES), F32),
        compiler_params=pltpu.CompilerParams(dimension_semantics=("arbitrary",),
                                             vmem_limit_bytes=VMEM_LIMIT_BYTES),
        name="dispatch",
    )(zstart, zlen, tail, *([dest_flat] * TOP_K), xn4)


def _expert_kernel(item_e_ref, item_row_ref, item_n_ref, item_valid_ref,
                   xs_ref, w1_ref, b1_ref, w2_ref, b2_ref, ys_ref,
                   xbuf, abuf, stage, xpre, wperm, w2b, hcbuf, sem_s, sem_p, *, rb, j1, tw, nt):
    del item_e_ref, item_valid_ref, xs_ref
    w = pl.program_id(0)
    j = pl.program_id(1)
    nrows = item_n_ref[w]
    row0 = pl.multiple_of(item_row_ref[w], rb)
    nrb = nrows // rb
    d = w2_ref.shape[1]
    tf = w2_ref.shape[0]

    def rows(i):
        return pl.ds(pl.multiple_of(i * rb, rb), rb)

    g0 = lax.shift_right_logical(row0, 3)
    gb = rb // SUBLANES

    def stage_copy(i, slot, to_hbm):
        hbm = ys_ref.at[pl.ds(g0 + i * gb, gb)]
        vmem = stage.at[slot]
        return pltpu.make_async_copy(vmem, hbm, sem_s.at[slot]) if to_hbm else pltpu.make_async_copy(
            hbm, vmem, sem_s.at[slot])

    def first_block_copy(item):
        g = lax.shift_right_logical(pl.multiple_of(item_row_ref[item], rb), 3)
        return pltpu.make_async_copy(ys_ref.at[pl.ds(g, gb)], xpre, sem_p)

    def slot_of(jj, i):
        return (jj * nrb + i) & 1

    def dot1(i, x):
        hcbuf[slot_of(j, i)] = _dot(x, w1_ref[...].astype(BF16)) + b1_ref[...]

    lane = lax.broadcasted_iota(I32, (rb, LANES), 1)
    even = (lane & 1) == 0

    def tail1(jj, i):
        hc = hcbuf[slot_of(jj, i)]
        outs = []
        for q in range(tw // (2 * LANES)):
            c0 = hc[:, 2 * q * LANES:(2 * q + 1) * LANES]
            c1 = hc[:, (2 * q + 1) * LANES:(2 * q + 2) * LANES]
            glu = jnp.where(even, c0, pltpu.roll(c1, 1, 1))
            lin = jnp.where(even, pltpu.roll(c0, LANES - 1, 1), c1)
            glu = jnp.minimum(glu, SWIGLU_LIMIT)
            lin = jnp.clip(lin, -SWIGLU_LIMIT, SWIGLU_LIMIT)
            act = glu * (1.0 / (1.0 + jnp.exp(-SWIGLU_ALPHA * glu))) * (lin + 1.0)
            outs.append(act.astype(BF16))
        abuf[jj, rows(i), :] = jnp.concatenate(outs, axis=1)

    def permute_w2_tile():
        for s in range(d // LANES):
            for g in range(tf // LANES):
                top = w2_ref[g * LANES:g * LANES + LANES // 2, s * LANES:(s + 1) * LANES]
                bot = w2_ref[g * LANES + LANES // 2:(g + 1) * LANES, s * LANES:(s + 1) * LANES]
                wperm.at[s][pl.ds(g * LANES, LANES // 2, stride=2), :] = top
                wperm.at[s][pl.ds(g * LANES + 1, LANES // 2, stride=2), :] = bot
        k0 = pl.multiple_of(j * tf, tf)
        for s in range(d // LANES):
            w2b[pl.ds(k0, tf), s * LANES:(s + 1) * LANES] = wperm[s].astype(BF16)

    @pl.when((j == 0) & (nrows > 0))
    def _first_step():
        s_prev = (item_n_ref[jnp.maximum(w - 1, 0)] // rb - 1) & 1
        off = jnp.where(w > 0, s_prev, 0)

        def sx(i):
            return (i + off) & 1

        def wait_prev_result():
            stage_copy(0, s_prev, True).wait()

        def fetch(i):
            stage_copy(i, sx(i), False).wait()

            @pl.when((i == 1) & (w > 0))
            def _():
                wait_prev_result()

            @pl.when(i + 1 < nrb)
            def _():
                stage_copy(i + 1, sx(i + 1), False).start()

        def load_and_dot(i, src=None):
            blk = stage.at[sx(i)] if src is None else src
            x = jnp.concatenate([blk[:, c].reshape(rb, LANES) for c in range(nt)], axis=1).astype(BF16)
            xbuf[rows(i), :] = x
            dot1(i, x)

        @pl.when(w == 0)
        def _():
            stage_copy(0, 0, False).start()
            fetch(0)
            load_and_dot(0)
            permute_w2_tile()

        @pl.when(w > 0)
        def _():
            first_block_copy(w).wait()

            @pl.when(1 < nrb)
            def _():
                stage_copy(1, sx(1), False).start()

            load_and_dot(0, xpre)
            permute_w2_tile()

        def body(i, c):
            fetch(i)
            tail1(0, i - 1)
            load_and_dot(i)
            return c

        lax.fori_loop(1, nrb, body, 0)

        @pl.when((w > 0) & (nrb < 2))
        def _():
            wait_prev_result()

    @pl.when((j > 0) & (j < j1) & (nrows > 0))
    def _next_steps():
        tail1(j - 1, nrb - 1)
        dot1(0, xbuf[rows(0), :])
        permute_w2_tile()

        def body(i, c):
            tail1(j, i - 1)
            dot1(i, xbuf[rows(i), :])
            return c

        lax.fori_loop(1, nrb, body, 0)

    @pl.when((j == j1) & (nrows > 0))
    def _last_step():
        nxt = jnp.minimum(w + 1, pl.num_programs(0) - 1)
        has_next = (w + 1 < pl.num_programs(0)) & (item_n_ref[nxt] > 0)

        @pl.when(has_next)
        def _():
            first_block_copy(nxt).start()

        tail1(j1 - 1, nrb - 1)

        def body(i, c):
            @pl.when(i >= 2)
            def _():
                stage_copy(i - 2, i & 1, True).wait()

            @pl.when(i >= 1)
            def _():
                stage_copy(i - 1, (i - 1) & 1, True).start()

            a = jnp.concatenate([abuf[jj, rows(i), :] for jj in range(j1)], axis=1)
            y = _dot(a, w2b[...]) + b2_ref[...]
            for c in range(nt):
                stage[i & 1, :, c] = y[:, c * LANES:(c + 1) * LANES].reshape(gb, SUBLANES, LANES)
            return c

        lax.fori_loop(0, nrb, body, 0)

        @pl.when(nrb >= 2)
        def _():
            stage_copy(nrb - 2, nrb & 1, True).wait()

        last = stage_copy(nrb - 1, (nrb - 1) & 1, True)
        last.start()

        @pl.when(jnp.logical_not(has_next))
        def _():
            last.wait()


def _experts(item_e, item_row, item_n, item_valid, xs, w1, b1, w2, b2, *, r_max, tw):
    n_exp, d, f2 = w1.shape
    f_dim = w2.shape[1]
    nt = xs.shape[1]
    j1 = f2 // tw
    tf = f_dim // j1
    assert tf == tw // 2 and tf % LANES == 0 and nt * LANES == d
    rb = EXPERT_BLOCK
    kern = functools.partial(_expert_kernel, rb=rb, j1=j1, tw=tw, nt=nt)

    def w_step(w, j, iv):
        return jnp.minimum(jnp.where(iv[w] == 1, j, j1), j1 - 1)

    def w1_map(w, j, ie, ir, inn, iv):
        return (ie[w], 0, w_step(w, j, iv))

    def w2_map(w, j, ie, ir, inn, iv):
        return (ie[w], w_step(w, j, iv), 0)

    def e_map(w, j, ie, ir, inn, iv):
        return (ie[w], 0, 0)

    grid_spec = pltpu.PrefetchScalarGridSpec(
        num_scalar_prefetch=4, grid=(item_e.shape[0], j1 + 1),
        in_specs=[
            pl.BlockSpec(memory_space=pl.ANY),
            pl.BlockSpec((None, d, tw), w1_map),
            pl.BlockSpec((None, 1, tw), w1_map),
            pl.BlockSpec((None, tf, d), w2_map),
            pl.BlockSpec((None, 1, d), e_map),
        ],
        out_specs=pl.BlockSpec(memory_space=pl.ANY),
        scratch_shapes=[
            pltpu.VMEM((r_max, d), BF16),
            pltpu.VMEM((j1, r_max, tf), BF16),
            pltpu.VMEM((2, rb // SUBLANES, nt, SUBLANES, LANES), F32),
            pltpu.VMEM((rb // SUBLANES, nt, SUBLANES, LANES), F32),
            pltpu.VMEM((d // LANES, tf, LANES), F32),
            pltpu.VMEM((f_dim, d), BF16),
            pltpu.VMEM((2, rb, tw), F32),
            pltpu.SemaphoreType.DMA((2,)),
            pltpu.SemaphoreType.DMA,
        ],
    )
    return pl.pallas_call(
        kern, grid_spec=grid_spec,
        out_shape=jax.ShapeDtypeStruct(xs.shape, F32),
        input_output_aliases={4: 0},
        compiler_params=pltpu.CompilerParams(dimension_semantics=("arbitrary", "arbitrary"),
                                             vmem_limit_bytes=VMEM_LIMIT_BYTES),
        name="experts",
    )(item_e, item_row, item_n, item_valid, xs, w1, b1.reshape(n_exp, 1, f2), w2, b2.reshape(n_exp, 1, d))


def _combine_kernel(*refs, tc, nt):
    dest_refs = refs[:TOP_K]
    tokmeta_ref, h1_ref, fg_ref, ys_ref, o_ref, buf, ssq_ref, sem = refs[TOP_K:]
    i = pl.program_id(0)
    last = pl.num_programs(0) - 1
    slot = i & 1
    d = nt * LANES

    per_row = nt // SUBLANES

    def loop(do_sum, do_issue):
        def body(g, c):
            r8 = pl.ds(pl.multiple_of(g * SUBLANES, SUBLANES), SUBLANES)
            if do_sum:
                tm = tokmeta_ref[r8, :]
                gates = [tm[:, 2 * TOP_K + k:2 * TOP_K + k + 1] for k in range(TOP_K)]
                ssq = jnp.zeros((SUBLANES, LANES), F32)
            for s in range(SUBLANES):
                if do_issue:
                    for k in range(TOP_K):
                        src = _row_of(ys_ref, dest_refs[k][g * SUBLANES + s])
                        pltpu.make_async_copy(src, buf.at[slot, k, g, :, s, :], sem.at[slot]).start(
                            priority=k % 2)
                if do_sum:
                    for cc in range(s * per_row, (s + 1) * per_row):
                        a = h1_ref[r8, cc * LANES:(cc + 1) * LANES]
                        for k in range(TOP_K):
                            a = a + gates[k] * buf[1 - slot, k, g, cc]
                        o_ref[r8, cc * LANES:(cc + 1) * LANES] = a
                        ssq = ssq + a * a
            if do_sum:
                ssq_ref[r8, :] = ssq
            return c

        lax.fori_loop(0, tc // SUBLANES, body, 0)
        if do_sum:
            inv = lax.rsqrt(jnp.sum(ssq_ref[...], axis=1, keepdims=True) * (1.0 / d) + EPS)
            o_ref[...] = o_ref[...] * inv * fg_ref[...]

    @pl.when(i > 0)
    def _():
        for k in range(TOP_K):
            pltpu.make_async_copy(buf.at[1 - slot, k], buf.at[1 - slot, k], sem.at[1 - slot]).wait()

    @pl.when(i == 0)
    def _():
        loop(False, True)

    @pl.when((i > 0) & (i < last))
    def _():
        loop(True, True)

    @pl.when(i == last)
    def _():
        loop(True, False)


def _combine(dest_flat, tokmeta, h1, fg, ys, *, tc):
    t, d = h1.shape
    nt = ys.shape[1]
    nb = t // tc
    kern = functools.partial(_combine_kernel, tc=tc, nt=nt)
    prev = lambda i: (jnp.maximum(i - 1, 0), 0)
    return pl.pallas_call(
        kern, grid=(nb + 1,),
        in_specs=[pl.BlockSpec((tc,), lambda i, k=k: (k * nb + jnp.minimum(i, nb - 1),), memory_space=pltpu.SMEM)
                  for k in range(TOP_K)] + [
            pl.BlockSpec((tc, LANES), prev), pl.BlockSpec((tc, d), prev), _const_spec(fg.shape),
            pl.BlockSpec(memory_space=pl.ANY),
        ],
        out_specs=pl.BlockSpec((tc, d), prev),
        out_shape=jax.ShapeDtypeStruct((t, d), F32),
        scratch_shapes=[pltpu.VMEM((2, TOP_K, tc // SUBLANES, nt, SUBLANES, LANES), F32),
                        pltpu.VMEM((tc, LANES), F32), pltpu.SemaphoreType.DMA((2,))],
        compiler_params=pltpu.CompilerParams(dimension_semantics=("arbitrary",),
                                             vmem_limit_bytes=VMEM_LIMIT_BYTES),
        name="combine",
    )(*([dest_flat] * TOP_K), tokmeta, h1, fg, ys)


def _rot_cols(w):
    h = QK_ROPE // 2
    return jnp.concatenate([-w[..., h:], w[..., :h]], axis=-1)


def _tile_rows(n, cap):
    t = min(n, cap)
    assert n % t == 0, (n, cap)
    return t


def kernel(x, meta_tokens, attn_norm_g, w_in, q_norm_g, w_uq, kv_norm_g, w_ukv, pool_w, pool_scale, w_o,
           ffn_norm_g, w_router, b_router, w1, b1, w2, b2, final_norm_g):
    nb, s_len, d = x.shape
    n_meta = meta_tokens.shape[0]
    assert w_in.shape[0] == 1, "one layer"
    assert n_meta == HALO and max(POOL_WINDOWS) - 1 <= HALO
    pw = pool_scale.shape[1]
    q_lora = q_norm_g.shape[1]
    kv_lora = kv_norm_g.shape[1]
    n_heads = w_uq.shape[2] // (QK_NOPE + QK_ROPE)
    n_exp = w_router.shape[2]
    f_dim = w2.shape[2]
    t = nb * s_len
    assert s_len % CHUNK == 0 and n_exp <= LANES and pw // len(POOL_WINDOWS) % LANES == 0

    win = w_in[0]
    o = pw + q_lora + kv_lora
    w_kr = win[:, o:o + QK_ROPE]
    zc = jnp.zeros((d, LANES - QK_ROPE), F32)
    win_b = jnp.concatenate([win[:, :o], w_kr, zc, _rot_cols(w_kr), zc], axis=1).astype(BF16)
    wq3 = w_uq[0].reshape(q_lora, n_heads, QK_NOPE + QK_ROPE)
    zq = jnp.zeros((q_lora, n_heads, LANES - QK_ROPE), F32)
    wq_b = jnp.concatenate([wq3, zq], axis=2).reshape(q_lora, n_heads * HEAD_PAD).astype(BF16)
    wqr_b = jnp.concatenate([_rot_cols(wq3[:, :, QK_NOPE:]), zq], axis=2).reshape(
        q_lora, n_heads * LANES).astype(BF16)
    wkv3 = w_ukv[0].reshape(kv_lora, n_heads, QK_NOPE + V_DIM)
    wk_b = wkv3[:, :, :QK_NOPE].reshape(kv_lora, n_heads * QK_NOPE).astype(BF16)
    wv_b = wkv3[:, :, QK_NOPE:].reshape(kv_lora, n_heads * V_DIM).astype(BF16)
    front_w = (attn_norm_g, win_b, pool_w[0].astype(BF16), pool_scale, q_norm_g, wq_b, wqr_b, kv_norm_g,
               wk_b, wv_b)
    woa = w_o[0, :pw].astype(BF16)
    wob = w_o[0, pw:].astype(BF16)
    wr_b = jnp.pad(w_router[0], ((0, 0), (0, LANES - n_exp))).astype(BF16)
    br = jnp.pad(b_router, ((0, 0), (0, LANES - n_exp)), constant_values=NEG_BIG)

    pos = jnp.arange(n_meta + s_len, dtype=F32)
    inv_freq = 1.0 / (ROPE_BASE ** (jnp.arange(0, QK_ROPE, 2, dtype=F32) / QK_ROPE))
    ang = pos[:, None] * inv_freq[None, :]
    ones = jnp.ones((n_meta + s_len, LANES - QK_ROPE), F32)
    cs = jnp.concatenate([jnp.cos(ang), jnp.cos(ang), ones], axis=1)
    sn = jnp.concatenate([jnp.sin(ang), jnp.sin(ang), 0.0 * ones], axis=1)

    zero_halo = jnp.zeros((HALO, pw), F32)
    _, _, k_meta, v_meta, p_meta = _front(meta_tokens[None], zero_halo, cs[:n_meta], sn[:n_meta], front_w,
                                          ts=n_meta)
    ts = _tile_rows(s_len, 512)
    y_pool, q, k, v, _ = _front(x, p_meta[0, 0], cs[n_meta:], sn[n_meta:], front_w, ts=ts)

    km = jnp.pad(k_meta[0], ((0, LANES - n_meta), (0, 0)))
    vm = jnp.pad(v_meta[0], ((0, LANES - n_meta), (0, 0)))
    y_mla = _attn(q, k, v, km, vm, tq=_tile_rows(s_len, 512), n_meta=n_meta)

    tm = _tile_rows(t, 512)
    h1, xn4, tokmeta, tokmeta_t, counts = _mid(y_pool.reshape(t, pw), y_mla.reshape(t, -1), x.reshape(t, d),
                                               woa, wob, ffn_norm_g, wr_b, br, tm=tm)

    r_max = 5 * EXPERT_BLOCK
    cnt = counts[0, :n_exp].astype(I32)
    padded = (cnt + EXPERT_BLOCK - 1) // EXPERT_BLOCK * EXPERT_BLOCK
    pad_end = jnp.cumsum(padded)
    pad_start = pad_end - padded
    n_assign = t * TOP_K
    p_rows = -(-(n_assign + n_exp * (EXPERT_BLOCK - 1)) // EXPERT_BLOCK) * EXPERT_BLOCK
    idx = tokmeta_t[0:TOP_K].astype(I32)
    rank = tokmeta_t[TOP_K:2 * TOP_K].astype(I32)
    start_of = jnp.zeros_like(idx)
    for e in range(n_exp):
        start_of = jnp.where(idx == e, pad_start[e], start_of)
    dest = (start_of + rank).reshape(-1)

    n_items = n_exp + p_rows // r_max
    per_e = (padded + r_max - 1) // r_max
    item_end = jnp.cumsum(per_e)
    total = item_end[-1]
    wi = jnp.arange(n_items, dtype=I32)
    valid = wi < total
    wc = jnp.minimum(wi, total - 1)
    ie = jnp.minimum(jnp.sum((item_end[None, :] <= wc[:, None]).astype(I32), axis=1), n_exp - 1)
    local = wc - (item_end[ie] - per_e[ie])
    item_row = jnp.where(valid, pad_start[ie] + local * r_max, 0).astype(I32)
    item_n = jnp.where(valid, jnp.clip(padded[ie] - local * r_max, 0, r_max), 0).astype(I32)

    zstart = (pad_start + cnt).astype(I32)
    zlen = (padded - cnt).astype(I32)
    tail = jnp.stack([pad_end[-1] // SUBLANES, (p_rows - pad_end[-1]) // (SUBLANES * ZERO_GROUPS)]).astype(I32)
    xs = _dispatch(zstart, zlen, tail, dest, xn4, p_rows, td=_tile_rows(t, 1024))
    ys = _experts(ie, item_row, item_n, valid.astype(I32), xs, w1[0], b1[0], w2[0], b2[0],
                  r_max=r_max, tw=min(512, 2 * f_dim))
    out = _combine(dest, tokmeta, h1, final_norm_g.reshape(1, d), ys, tc=_tile_rows(t, 256))
    return out.reshape(nb, s_len, d)
```

```python
import functools

import jax
import jax.numpy as jnp
from jax import lax
from jax.experimental import pallas as pl
from jax.experimental.pallas import tpu as pltpu

F32 = jnp.float32
BF16 = jnp.bfloat16
I32 = jnp.int32

CHUNK = 64
POOL_WINDOWS = (2, 4, 8, 16)
V_DIM = 128
QK_NOPE = 128
QK_ROPE = 64
ROPE_BASE = 10000.0
TOP_K = 4
SWIGLU_LIMIT = 7.0
SWIGLU_ALPHA = 1.702
EPS = 1e-5

LANES = 128
SUBLANES = 8
MXU_DIM = 256
HEAD_PAD = MXU_DIM
VMEM_LIMIT_BYTES = 58 * 1024 * 1024

ROW_TILE = 2 * MXU_DIM
EXPERT_BLOCK = ROW_TILE
EXPERT_ITEM_BLOCKS = 5
W1_TILE_COLS = 2 * MXU_DIM
DISPATCH_TILE = 4 * MXU_DIM
COMBINE_TILE = MXU_DIM

HALO = 16
NEG_BIG = -1e30


def _rms(x, g):
    ms = jnp.mean(x * x, axis=-1, keepdims=True)
    return x * lax.rsqrt(ms + EPS) * g


def _dot(a, b):
    return jnp.dot(a, b, preferred_element_type=F32)


def _dot_nt(a, b):
    return lax.dot_general(a, b, (((1,), (1,)), ((), ())), preferred_element_type=F32)


def _const_spec(shape):
    nd = len(shape)
    return pl.BlockSpec(shape, lambda *_: (0,) * nd)


def _row_of(ref, r):
    if isinstance(r, int):
        return ref.at[r // SUBLANES, :, r % SUBLANES, :]
    return ref.at[lax.shift_right_logical(r, 3), :, r & (SUBLANES - 1), :]


def _front_kernel(x_ref, mpool_ref, cs_ref, sn_ref, ag_ref, win_ref, pw_ref, ps_ref, qg_ref, wq_ref,
                  wqr_ref, kg_ref, wk_ref, wv_ref,
                  ypool_ref, q_ref, k_ref, v_ref, ptail_ref, ext_ref, lv_ref, *, ts, pool_w, q_lora, kv_lora,
                  n_heads):
    st = pl.program_id(1)
    hn = _rms(x_ref[0], ag_ref[...]).astype(BF16)
    proj = _dot(hn, win_ref[...])
    pool_in = proj[:, :pool_w]

    lo = SUBLANES
    hi = SUBLANES + HALO + ts

    @pl.when(st == 0)
    def _():
        ext_ref[0:lo, :] = jnp.zeros((lo, pool_w), F32)
        ext_ref[lo:lo + HALO, :] = mpool_ref[...]
        lv_ref[:, 0:lo, :] = jnp.zeros((2, lo, lv_ref.shape[2]), F32)

    ext_ref[lo + HALO:hi, :] = pool_in
    gw = pool_w // len(POOL_WINDOWS)
    for g, w in enumerate(POOL_WINDOWS):
        c0 = g * gw
        u = pool_in[:, c0:c0 + gw]
        cur = ext_ref[lo:hi, c0:c0 + gw]
        src, m = None, 1
        while m < w:
            prev = ext_ref[lo - m:hi - m, c0:c0 + gw] if src is None else lv_ref[src, lo - m:hi - m, :]
            cur = cur + prev
            m *= 2
            if m < w:
                src = 0 if src != 0 else 1
                lv_ref[src, lo:hi, :] = cur
        s = cur[HALO:, :]
        d = (s * (1.0 / w) - u).astype(BF16)
        y = _dot(d, pw_ref[g]) * ps_ref[:, c0:c0 + gw]
        ypool_ref[0, :, c0:c0 + gw] = y.astype(BF16)
    tail = pool_in[ts - HALO:ts, :]
    ext_ref[lo:lo + HALO, :] = tail
    ptail_ref[0, 0] = tail

    o = pool_w
    q_c = proj[:, o:o + q_lora]
    o += q_lora
    kv_c = proj[:, o:o + kv_lora]
    o += kv_lora
    kr = proj[:, o:o + LANES]
    kr_rot = proj[:, o + LANES:o + 2 * LANES]
    cs = cs_ref[...]
    sn = sn_ref[...]
    krope = (kr * cs + kr_rot * sn).astype(BF16)
    qn = _rms(q_c, qg_ref[...]).astype(BF16)
    qm = _dot(qn, wq_ref[...])
    qr = _dot(qn, wqr_ref[...])
    kvn = _rms(kv_c, kg_ref[...]).astype(BF16)
    kn = _dot(kvn, wk_ref[...])
    v_ref[0] = _dot(kvn, wv_ref[...]).astype(BF16)
    for h in range(n_heads):
        a = h * HEAD_PAD
        b = h * LANES
        q_ref[0, :, a:a + LANES] = qm[:, a:a + LANES].astype(BF16)
        q_ref[0, :, a + LANES:a + HEAD_PAD] = (
            qm[:, a + LANES:a + HEAD_PAD] * cs + qr[:, b:b + LANES] * sn).astype(BF16)
        k_ref[0, :, a:a + LANES] = kn[:, b:b + LANES].astype(BF16)
        k_ref[0, :, a + LANES:a + HEAD_PAD] = krope


def _front(x3, mpool, cs, sn, wts, *, ts):
    nb, s_len, d = x3.shape
    (ag, win, pw, ps, qg, wq, wqr, kg, wk, wv) = wts
    pool_w = ps.shape[1]
    q_lora = qg.shape[1]
    kv_lora = kg.shape[1]
    n_heads = wk.shape[1] // LANES
    n_st = s_len // ts
    kern = functools.partial(_front_kernel, ts=ts, pool_w=pool_w, q_lora=q_lora, kv_lora=kv_lora,
                             n_heads=n_heads)
    row = lambda b, s: (b, s, 0)
    in_specs = [
        pl.BlockSpec((1, ts, d), row),
        _const_spec(mpool.shape),
        pl.BlockSpec((ts, LANES), lambda b, s: (s, 0)),
        pl.BlockSpec((ts, LANES), lambda b, s: (s, 0)),
    ] + [_const_spec(w.shape) for w in wts]
    out_shape = (
        jax.ShapeDtypeStruct((nb, s_len, pool_w), BF16),
        jax.ShapeDtypeStruct((nb, s_len, n_heads * HEAD_PAD), BF16),
        jax.ShapeDtypeStruct((nb, s_len, n_heads * HEAD_PAD), BF16),
        jax.ShapeDtypeStruct((nb, s_len, n_heads * V_DIM), BF16),
        jax.ShapeDtypeStruct((nb, n_st, HALO, pool_w), F32),
    )
    out_specs = (
        pl.BlockSpec((1, ts, pool_w), row),
        pl.BlockSpec((1, ts, n_heads * HEAD_PAD), row),
        pl.BlockSpec((1, ts, n_heads * HEAD_PAD), row),
        pl.BlockSpec((1, ts, n_heads * V_DIM), row),
        pl.BlockSpec((1, 1, HALO, pool_w), lambda b, s: (b, s, 0, 0)),
    )
    return pl.pallas_call(
        kern, grid=(nb, n_st), in_specs=in_specs, out_specs=out_specs, out_shape=out_shape,
        scratch_shapes=[pltpu.VMEM((SUBLANES + HALO + ts, pool_w), F32),
                        pltpu.VMEM((2, SUBLANES + HALO + ts, pool_w // len(POOL_WINDOWS)), F32)],
        compiler_params=pltpu.CompilerParams(dimension_semantics=("arbitrary", "arbitrary"),
                                             vmem_limit_bytes=VMEM_LIMIT_BYTES),
        name="front",
    )(x3, mpool, cs, sn, *wts)


def _attn_kernel(q_ref, k_ref, v_ref, km_ref, vm_ref, o_ref, m_ref, l_ref, acc_ref, *, tq, n_meta, n_heads,
                 scale):
    qi = pl.program_id(1)
    c2 = scale * 1.4426950408889634

    def update(h, s, vb, first):
        s_max = jnp.max(s, axis=1, keepdims=True)
        if first:
            m_new = jnp.broadcast_to(s_max, (tq, LANES))
        else:
            m_old = m_ref[h]
            m_new = jnp.maximum(m_old, s_max)
            alpha = jnp.exp2((m_old - m_new) * c2)
        p = jnp.exp2((s - jnp.concatenate([m_new] * (s.shape[1] // LANES), axis=1)) * c2)
        v1 = jnp.concatenate([vb, jnp.ones(vb.shape, BF16)], axis=1)
        pv = _dot(p.astype(BF16), v1)
        if first:
            l_ref[h] = pv[:, V_DIM:]
            acc_ref[h] = pv[:, :V_DIM]
        else:
            l_ref[h] = alpha * l_ref[h] + pv[:, V_DIM:]
            acc_ref[h] = alpha * acc_ref[h] + pv[:, :V_DIM]
        m_ref[h] = m_new

    def q_of(h):
        return q_ref[0, :, h * HEAD_PAD:(h + 1) * HEAD_PAD]

    r_diag = pl.multiple_of(qi * tq, tq)
    rc = lax.broadcasted_iota(I32, (tq, tq), 0) // CHUNK
    cc = lax.broadcasted_iota(I32, (tq, tq), 1) // CHUNK
    vis = jnp.concatenate([cc <= rc, lax.broadcasted_iota(I32, (tq, LANES), 1) < n_meta], axis=1)
    for h in range(n_heads):
        kd = jnp.concatenate([k_ref[0, pl.ds(r_diag, tq), h * HEAD_PAD:(h + 1) * HEAD_PAD],
                              km_ref[:, h * HEAD_PAD:(h + 1) * HEAD_PAD]], axis=0)
        vd = jnp.concatenate([v_ref[0, pl.ds(r_diag, tq), h * V_DIM:(h + 1) * V_DIM],
                              vm_ref[:, h * V_DIM:(h + 1) * V_DIM]], axis=0)
        s = jnp.where(vis, _dot_nt(q_of(h), kd), -jnp.inf)
        update(h, s, vd, True)

    def body(j, c):
        r0 = pl.multiple_of(j * tq, tq)
        for h in range(n_heads):
            s = _dot_nt(q_of(h), k_ref[0, pl.ds(r0, tq), h * HEAD_PAD:(h + 1) * HEAD_PAD])
            update(h, s, v_ref[0, pl.ds(r0, tq), h * V_DIM:(h + 1) * V_DIM], False)
        return c

    lax.fori_loop(0, qi, body, 0)
    for h in range(n_heads):
        o_ref[0, :, h * V_DIM:(h + 1) * V_DIM] = (acc_ref[h] / l_ref[h]).astype(BF16)


def _attn(q, k, v, km, vm, *, tq, n_meta):
    nb, s_len, hw = q.shape
    n_heads = hw // HEAD_PAD
    kern = functools.partial(_attn_kernel, tq=tq, n_meta=n_meta, n_heads=n_heads,
                             scale=float((QK_NOPE + QK_ROPE) ** -0.5))
    return pl.pallas_call(
        kern, grid=(nb, s_len // tq),
        in_specs=[
            pl.BlockSpec((1, tq, hw), lambda b, i: (b, i, 0)),
            pl.BlockSpec((1, s_len, hw), lambda b, i: (b, 0, 0)),
            pl.BlockSpec((1, s_len, n_heads * V_DIM), lambda b, i: (b, 0, 0)),
            _const_spec(km.shape),
            _const_spec(vm.shape),
        ],
        out_specs=pl.BlockSpec((1, tq, n_heads * V_DIM), lambda b, i: (b, i, 0)),
        out_shape=jax.ShapeDtypeStruct((nb, s_len, n_heads * V_DIM), BF16),
        scratch_shapes=[pltpu.VMEM((n_heads, tq, LANES), F32)] * 3,
        compiler_params=pltpu.CompilerParams(dimension_semantics=("arbitrary", "arbitrary"),
                                             vmem_limit_bytes=VMEM_LIMIT_BYTES),
        name="attn",
    )(q, k, v, km, vm)


def _mid_kernel(yp_ref, ym_ref, x_ref, woa_ref, wob_ref, fg_ref, wr_ref, br_ref,
                h1_ref, xn4_ref, tokmeta_ref, tokmeta_t_ref, counts_ref, run_ref, *, tm):
    i = pl.program_id(0)

    @pl.when(i == 0)
    def _():
        run_ref[...] = jnp.zeros_like(run_ref)

    h1 = x_ref[...] + _dot(yp_ref[...], woa_ref[...]) + _dot(ym_ref[...], wob_ref[...])
    h1_ref[...] = h1
    xn = _rms(h1, fg_ref[...])
    for c in range(xn4_ref.shape[1]):
        xn4_ref[:, c] = xn[:, c * LANES:(c + 1) * LANES].reshape(tm // SUBLANES, SUBLANES, LANES)
    xb = xn.astype(BF16)

    kh = xb.shape[1] // 2
    logits = _dot(xb[:, :kh], wr_ref[0:kh, :]) + _dot(xb[:, kh:], wr_ref[kh:2 * kh, :]) + br_ref[...]
    lane = lax.broadcasted_iota(I32, logits.shape, 1).astype(F32)
    work = logits
    idxs, vals = [], []
    for _ in range(TOP_K):
        mx = jnp.max(work, axis=1, keepdims=True)
        ix = jnp.min(jnp.where(work == mx, lane, float(LANES)), axis=1, keepdims=True)
        idxs.append(ix)
        vals.append(mx)
        work = jnp.where(lane == ix, -jnp.inf, work)
    es = [jnp.exp(vv - vals[0]) for vv in vals]
    den = es[0]
    for e in es[1:]:
        den = den + e
    hot = [jnp.where(lane == ix, 1.0, 0.0) for ix in idxs]
    cnt = hot[0]
    for hh in hot[1:]:
        cnt = cnt + hh
    rr = lax.broadcasted_iota(I32, (tm, tm), 0)
    cc = lax.broadcasted_iota(I32, (tm, tm), 1)
    ltri = jnp.where(rr > cc, 1.0, 0.0).astype(BF16)
    base = run_ref[0:1, :] + _dot(ltri, cnt.astype(BF16))
    out = jnp.zeros(logits.shape, F32)
    for k in range(TOP_K):
        rank = jnp.sum(hot[k] * base, axis=1, keepdims=True)
        out = jnp.where(lane == float(k), idxs[k], out)
        out = jnp.where(lane == float(TOP_K + k), rank, out)
        out = jnp.where(lane == float(2 * TOP_K + k), es[k] / den, out)
    tokmeta_ref[...] = out
    tokmeta_t_ref[...] = jnp.transpose(out)[0:tokmeta_t_ref.shape[0], :]
    run = run_ref[...] + jnp.sum(cnt, axis=0, keepdims=True)
    run_ref[...] = run
    counts_ref[...] = run


def _mid(yp, ym, x2, woa, wob, fg, wr, br, *, tm):
    t, d = x2.shape
    pw = yp.shape[1]
    mw = ym.shape[1]
    nt = d // LANES
    kern = functools.partial(_mid_kernel, tm=tm)
    row = lambda i: (i, 0)
    return pl.pallas_call(
        kern, grid=(t // tm,),
        in_specs=[
            pl.BlockSpec((tm, pw), row), pl.BlockSpec((tm, mw), row), pl.BlockSpec((tm, d), row),
            _const_spec(woa.shape), _const_spec(wob.shape), _const_spec(fg.shape),
            _const_spec(wr.shape), _const_spec(br.shape),
        ],
        out_specs=(
            pl.BlockSpec((tm, d), row), pl.BlockSpec((tm // SUBLANES, nt, SUBLANES, LANES), lambda i: (i, 0, 0, 0)),
            pl.BlockSpec((tm, LANES), row), pl.BlockSpec((2 * SUBLANES, tm), lambda i: (0, i)),
            pl.BlockSpec((8, LANES), lambda i: (0, 0)),
        ),
        out_shape=(
            jax.ShapeDtypeStruct((t, d), F32), jax.ShapeDtypeStruct((t // SUBLANES, nt, SUBLANES, LANES), F32),
            jax.ShapeDtypeStruct((t, LANES), F32), jax.ShapeDtypeStruct((2 * SUBLANES, t), F32),
            jax.ShapeDtypeStruct((8, LANES), F32),
        ),
        scratch_shapes=[pltpu.VMEM((8, LANES), F32)],
        compiler_params=pltpu.CompilerParams(dimension_semantics=("arbitrary",),
                                             vmem_limit_bytes=VMEM_LIMIT_BYTES),
        name="mid",
    )(yp, ym, x2, woa, wob, fg, wr, br)


ZERO_GROUPS = EXPERT_BLOCK // (2 * SUBLANES)


def _dispatch_kernel(zstart_ref, zlen_ref, tail_ref, *refs, td, n_exp):
    dest_refs = refs[:TOP_K]
    xn4_ref, xs_ref, zbuf, sem, sem_z = refs[TOP_K:]

    def body(r, c):
        for k in range(TOP_K):
            pltpu.make_async_copy(_row_of(xn4_ref, r), _row_of(xs_ref, dest_refs[k][r]), sem).start(
                priority=k % 2)
        return c

    lax.fori_loop(0, td, body, 0, unroll=8)
    for _ in range(TOP_K):
        pltpu.make_async_copy(xn4_ref, xn4_ref, sem).wait()

    @pl.when(pl.program_id(0) == pl.num_programs(0) - 1)
    def _zero_fill():
        zbuf[...] = jnp.zeros_like(zbuf)
        tail0 = tail_ref[0]
        n_tail = tail_ref[1]

        def pad_copies(e, wait):
            zs = zstart_ref[e]
            zl = zlen_ref[e]
            head = jnp.minimum((-zs) & (SUBLANES - 1), zl)
            for h in range(SUBLANES - 1):
                @pl.when(h < head)
                def _(h=h):
                    cp = pltpu.make_async_copy(_row_of(zbuf, 0), _row_of(xs_ref, zs + h), sem_z)
                    cp.wait() if wait else cp.start()
            g0 = lax.shift_right_logical(zs + head, 3)
            ng = lax.shift_right_logical(zl - head, 3)
            v = ZERO_GROUPS
            while v >= 1:
                @pl.when((ng & v) != 0)
                def _(v=v):
                    off = g0 + (ng & (-2 * v))
                    cp = pltpu.make_async_copy(zbuf.at[pl.ds(0, v)], xs_ref.at[pl.ds(off, v)], sem_z)
                    cp.wait() if wait else cp.start()
                v //= 2

        def tail_copy(i, wait):
            cp = pltpu.make_async_copy(zbuf, xs_ref.at[pl.ds(tail0 + i * ZERO_GROUPS, ZERO_GROUPS)], sem_z)
            cp.wait() if wait else cp.start()

        for wait in (False, True):
            lax.fori_loop(0, n_exp, lambda e, c, wait=wait: (pad_copies(e, wait), c)[1], 0)
            lax.fori_loop(0, n_tail, lambda i, c, wait=wait: (tail_copy(i, wait), c)[1], 0)


def _dispatch(zstart, zlen, tail, dest_flat, xn4, p_rows, *, td):
    tg, nt, _, _ = xn4.shape
    n_exp = zstart.shape[0]
    kern = functools.partial(_dispatch_kernel, td=td, n_exp=n_exp)
    nb = tg * SUBLANES // td
    grid_spec = pltpu.PrefetchScalarGridSpec(
        num_scalar_prefetch=3, grid=(nb,),
        in_specs=[pl.BlockSpec((td,), lambda i, *_, k=k: (k * nb + i,), memory_space=pltpu.SMEM)
                  for k in range(TOP_K)] + [
            pl.BlockSpec((td // SUBLANES, nt, SUBLANES, LANES), lambda i, *_: (i, 0, 0, 0)),
        ],
        out_specs=pl.BlockSpec(memory_space=pl.ANY),
        scratch_shapes=[pltpu.VMEM((ZERO_GROUPS, nt, SUBLANES, LANES), F32), pltpu.SemaphoreType.DMA,
                        pltpu.SemaphoreType.DMA],
    )
    return pl.pallas_call(
        kern, grid_spec=grid_spec,
        out_shape=jax.ShapeDtypeStruct((p_rows // SUBLANES, nt, SUBLANES, LANES), F32),
        compiler_params=pltpu.CompilerParams(dimension_semantics=("arbitrary",),
                                             vmem_limit_bytes=VMEM_LIMIT_BYTES),
        name="dispatch",
    )(zstart, zlen, tail, *([dest_flat] * TOP_K), xn4)


def _expert_kernel(item_e_ref, item_row_ref, item_n_ref, item_valid_ref,
                   xs_ref, w1_ref, b1_ref, w2_ref, b2_ref, ys_ref,
                   xbuf, abuf, stage, xpre, wperm, w2b, hcbuf, sem_s, sem_p, *, rb, j1, tw, nt):
    del item_e_ref, item_valid_ref, xs_ref
    w = pl.program_id(0)
    j = pl.program_id(1)
    nrows = item_n_ref[w]
    row0 = pl.multiple_of(item_row_ref[w], rb)
    nrb = nrows // rb
    d = w2_ref.shape[1]
    tf = w2_ref.shape[0]

    def rows(i):
        return pl.ds(pl.multiple_of(i * rb, rb), rb)

    g0 = lax.shift_right_logical(row0, 3)
    gb = rb // SUBLANES

    def stage_copy(i, slot, to_hbm):
        hbm = ys_ref.at[pl.ds(g0 + i * gb, gb)]
        vmem = stage.at[slot]
        return pltpu.make_async_copy(vmem, hbm, sem_s.at[slot]) if to_hbm else pltpu.make_async_copy(
            hbm, vmem, sem_s.at[slot])

    def first_block_copy(item):
        g = lax.shift_right_logical(pl.multiple_of(item_row_ref[item], rb), 3)
        return pltpu.make_async_copy(ys_ref.at[pl.ds(g, gb)], xpre, sem_p)

    def slot_of(jj, i):
        return (jj * nrb + i) & 1

    def dot1(i, x):
        hcbuf[slot_of(j, i)] = _dot(x, w1_ref[...].astype(BF16)) + b1_ref[...]

    lane = lax.broadcasted_iota(I32, (rb, LANES), 1)
    even = (lane & 1) == 0

    def tail1(jj, i):
        hc = hcbuf[slot_of(jj, i)]
        outs = []
        for q in range(tw // (2 * LANES)):
            c0 = hc[:, 2 * q * LANES:(2 * q + 1) * LANES]
            c1 = hc[:, (2 * q + 1) * LANES:(2 * q + 2) * LANES]
            glu = jnp.where(even, c0, pltpu.roll(c1, 1, 1))
            lin = jnp.where(even, pltpu.roll(c0, LANES - 1, 1), c1)
            glu = jnp.minimum(glu, SWIGLU_LIMIT)
            lin = jnp.clip(lin, -SWIGLU_LIMIT, SWIGLU_LIMIT)
            act = glu * (1.0 / (1.0 + jnp.exp(-SWIGLU_ALPHA * glu))) * (lin + 1.0)
            outs.append(act.astype(BF16))
        abuf[jj, rows(i), :] = jnp.concatenate(outs, axis=1)

    def permute_w2_tile():
        for s in range(d // LANES):
            for g in range(tf // LANES):
                top = w2_ref[g * LANES:g * LANES + LANES // 2, s * LANES:(s + 1) * LANES]
                bot = w2_ref[g * LANES + LANES // 2:(g + 1) * LANES, s * LANES:(s + 1) * LANES]
                wperm.at[s][pl.ds(g * LANES, LANES // 2, stride=2), :] = top
                wperm.at[s][pl.ds(g * LANES + 1, LANES // 2, stride=2), :] = bot
        k0 = pl.multiple_of(j * tf, tf)
        for s in range(d // LANES):
            w2b[pl.ds(k0, tf), s * LANES:(s + 1) * LANES] = wperm[s].astype(BF16)

    @pl.when((j == 0) & (nrows > 0))
    def _first_step():
        def fetch(i):
            stage_copy(i, i & 1, False).wait()

            @pl.when(i + 1 < nrb)
            def _():
                stage_copy(i + 1, (i + 1) & 1, False).start()

        def load_and_dot(i, src=None):
            blk = stage.at[i & 1] if src is None else src
            x = jnp.concatenate([blk[:, c].reshape(rb, LANES) for c in range(nt)], axis=1).astype(BF16)
            xbuf[rows(i), :] = x
            dot1(i, x)

        @pl.when(w == 0)
        def _():
            stage_copy(0, 0, False).start()
            fetch(0)
            load_and_dot(0)
            permute_w2_tile()

        @pl.when(w > 0)
        def _():
            first_block_copy(w).wait()

            @pl.when(1 < nrb)
            def _():
                stage_copy(1, 1, False).start()

            load_and_dot(0, xpre)
            permute_w2_tile()

        def body(i, c):
            fetch(i)
            tail1(0, i - 1)
            load_and_dot(i)
            return c

        lax.fori_loop(1, nrb, body, 0)

    @pl.when((j > 0) & (j < j1) & (nrows > 0))
    def _next_steps():
        tail1(j - 1, nrb - 1)
        dot1(0, xbuf[rows(0), :])
        permute_w2_tile()

        def body(i, c):
            tail1(j, i - 1)
            dot1(i, xbuf[rows(i), :])
            return c

        lax.fori_loop(1, nrb, body, 0)

    @pl.when((j == j1) & (nrows > 0))
    def _last_step():
        nxt = jnp.minimum(w + 1, pl.num_programs(0) - 1)

        @pl.when((w + 1 < pl.num_programs(0)) & (item_n_ref[nxt] > 0))
        def _():
            first_block_copy(nxt).start()

        tail1(j1 - 1, nrb - 1)

        def body(i, c):
            @pl.when(i >= 2)
            def _():
                stage_copy(i - 2, i & 1, True).wait()

            @pl.when(i >= 1)
            def _():
                stage_copy(i - 1, (i - 1) & 1, True).start()

            a = jnp.concatenate([abuf[jj, rows(i), :] for jj in range(j1)], axis=1)
            y = _dot(a, w2b[...]) + b2_ref[...]
            for c in range(nt):
                stage[i & 1, :, c] = y[:, c * LANES:(c + 1) * LANES].reshape(gb, SUBLANES, LANES)
            return c

        lax.fori_loop(0, nrb, body, 0)

        @pl.when(nrb >= 2)
        def _():
            stage_copy(nrb - 2, nrb & 1, True).wait()

        last = stage_copy(nrb - 1, (nrb - 1) & 1, True)
        last.start()
        last.wait()


def _experts(item_e, item_row, item_n, item_valid, xs, w1, b1, w2, b2, *, r_max, tw):
    n_exp, d, f2 = w1.shape
    f_dim = w2.shape[1]
    nt = xs.shape[1]
    j1 = f2 // tw
    tf = f_dim // j1
    assert tf == tw // 2 and tf % LANES == 0 and nt * LANES == d
    rb = EXPERT_BLOCK
    kern = functools.partial(_expert_kernel, rb=rb, j1=j1, tw=tw, nt=nt)

    def w_step(w, j, iv):
        return jnp.minimum(jnp.where(iv[w] == 1, j, j1), j1 - 1)

    def w1_map(w, j, ie, ir, inn, iv):
        return (ie[w], 0, w_step(w, j, iv))

    def w2_map(w, j, ie, ir, inn, iv):
        return (ie[w], w_step(w, j, iv), 0)

    def e_map(w, j, ie, ir, inn, iv):
        return (ie[w], 0, 0)

    grid_spec = pltpu.PrefetchScalarGridSpec(
        num_scalar_prefetch=4, grid=(item_e.shape[0], j1 + 1),
        in_specs=[
            pl.BlockSpec(memory_space=pl.ANY),
            pl.BlockSpec((None, d, tw), w1_map),
            pl.BlockSpec((None, 1, tw), w1_map),
            pl.BlockSpec((None, tf, d), w2_map),
            pl.BlockSpec((None, 1, d), e_map),
        ],
        out_specs=pl.BlockSpec(memory_space=pl.ANY),
        scratch_shapes=[
            pltpu.VMEM((r_max, d), BF16),
            pltpu.VMEM((j1, r_max, tf), BF16),
            pltpu.VMEM((2, rb // SUBLANES, nt, SUBLANES, LANES), F32),
            pltpu.VMEM((rb // SUBLANES, nt, SUBLANES, LANES), F32),
            pltpu.VMEM((d // LANES, tf, LANES), F32),
            pltpu.VMEM((f_dim, d), BF16),
            pltpu.VMEM((2, rb, tw), F32),
            pltpu.SemaphoreType.DMA((2,)),
            pltpu.SemaphoreType.DMA,
        ],
    )
    return pl.pallas_call(
        kern, grid_spec=grid_spec,
        out_shape=jax.ShapeDtypeStruct(xs.shape, F32),
        input_output_aliases={4: 0},
        compiler_params=pltpu.CompilerParams(dimension_semantics=("arbitrary", "arbitrary"),
                                             vmem_limit_bytes=VMEM_LIMIT_BYTES),
        name="experts",
    )(item_e, item_row, item_n, item_valid, xs, w1, b1.reshape(n_exp, 1, f2), w2, b2.reshape(n_exp, 1, d))


def _combine_kernel(*refs, tc, nt):
    dest_refs = refs[:TOP_K]
    tokmeta_ref, h1_ref, fg_ref, ys_ref, o_ref, buf, ssq_ref, sem = refs[TOP_K:]
    i = pl.program_id(0)
    last = pl.num_programs(0) - 1
    slot = i & 1
    d = nt * LANES

    per_row = nt // SUBLANES

    def loop(do_sum, do_issue):
        def body(g, c):
            r8 = pl.ds(pl.multiple_of(g * SUBLANES, SUBLANES), SUBLANES)
            if do_sum:
                tm = tokmeta_ref[r8, :]
                gates = [tm[:, 2 * TOP_K + k:2 * TOP_K + k + 1] for k in range(TOP_K)]
                ssq = jnp.zeros((SUBLANES, LANES), F32)
            for s in range(SUBLANES):
                if do_issue:
                    for k in range(TOP_K):
                        src = _row_of(ys_ref, dest_refs[k][g * SUBLANES + s])
                        pltpu.make_async_copy(src, buf.at[slot, k, g, :, s, :], sem.at[slot]).start(
                            priority=k % 2)
                if do_sum:
                    for cc in range(s * per_row, (s + 1) * per_row):
                        a = h1_ref[r8, cc * LANES:(cc + 1) * LANES]
                        for k in range(TOP_K):
                            a = a + gates[k] * buf[1 - slot, k, g, cc]
                        o_ref[r8, cc * LANES:(cc + 1) * LANES] = a
                        ssq = ssq + a * a
            if do_sum:
                ssq_ref[r8, :] = ssq
            return c

        lax.fori_loop(0, tc // SUBLANES, body, 0)
        if do_sum:
            inv = lax.rsqrt(jnp.sum(ssq_ref[...], axis=1, keepdims=True) * (1.0 / d) + EPS)
            o_ref[...] = o_ref[...] * inv * fg_ref[...]

    @pl.when(i > 0)
    def _():
        for k in range(TOP_K):
            pltpu.make_async_copy(buf.at[1 - slot, k], buf.at[1 - slot, k], sem.at[1 - slot]).wait()

    @pl.when(i == 0)
    def _():
        loop(False, True)

    @pl.when((i > 0) & (i < last))
    def _():
        loop(True, True)

    @pl.when(i == last)
    def _():
        loop(True, False)


def _combine(dest_flat, tokmeta, h1, fg, ys, *, tc):
    t, d = h1.shape
    nt = ys.shape[1]
    nb = t // tc
    kern = functools.partial(_combine_kernel, tc=tc, nt=nt)
    prev = lambda i: (jnp.maximum(i - 1, 0), 0)
    return pl.pallas_call(
        kern, grid=(nb + 1,),
        in_specs=[pl.BlockSpec((tc,), lambda i, k=k: (k * nb + jnp.minimum(i, nb - 1),), memory_space=pltpu.SMEM)
                  for k in range(TOP_K)] + [
            pl.BlockSpec((tc, LANES), prev), pl.BlockSpec((tc, d), prev), _const_spec(fg.shape),
            pl.BlockSpec(memory_space=pl.ANY),
        ],
        out_specs=pl.BlockSpec((tc, d), prev),
        out_shape=jax.ShapeDtypeStruct((t, d), F32),
        scratch_shapes=[pltpu.VMEM((2, TOP_K, tc // SUBLANES, nt, SUBLANES, LANES), F32),
                        pltpu.VMEM((tc, LANES), F32), pltpu.SemaphoreType.DMA((2,))],
        compiler_params=pltpu.CompilerParams(dimension_semantics=("arbitrary",),
                                             vmem_limit_bytes=VMEM_LIMIT_BYTES),
        name="combine",
    )(*([dest_flat] * TOP_K), tokmeta, h1, fg, ys)


def _rot_cols(w):
    h = QK_ROPE // 2
    return jnp.concatenate([-w[..., h:], w[..., :h]], axis=-1)


def _tile_rows(n, cap):
    t = min(n, cap)
    assert n % t == 0, (n, cap)
    return t


def kernel(x, meta_tokens, attn_norm_g, w_in, q_norm_g, w_uq, kv_norm_g, w_ukv, pool_w, pool_scale, w_o,
           ffn_norm_g, w_router, b_router, w1, b1, w2, b2, final_norm_g):
    nb, s_len, d = x.shape
    n_meta = meta_tokens.shape[0]
    assert w_in.shape[0] == 1, "one layer"
    assert n_meta == HALO and max(POOL_WINDOWS) - 1 <= HALO
    pw = pool_scale.shape[1]
    q_lora = q_norm_g.shape[1]
    kv_lora = kv_norm_g.shape[1]
    n_heads = w_uq.shape[2] // (QK_NOPE + QK_ROPE)
    n_exp = w_router.shape[2]
    f_dim = w2.shape[2]
    t = nb * s_len
    assert s_len % CHUNK == 0 and n_exp <= LANES and pw // len(POOL_WINDOWS) % LANES == 0

    win = w_in[0]
    o = pw + q_lora + kv_lora
    w_kr = win[:, o:o + QK_ROPE]
    zc = jnp.zeros((d, LANES - QK_ROPE), F32)
    win_b = jnp.concatenate([win[:, :o], w_kr, zc, _rot_cols(w_kr), zc], axis=1).astype(BF16)
    wq3 = w_uq[0].reshape(q_lora, n_heads, QK_NOPE + QK_ROPE)
    zq = jnp.zeros((q_lora, n_heads, LANES - QK_ROPE), F32)
    wq_b = jnp.concatenate([wq3, zq], axis=2).reshape(q_lora, n_heads * HEAD_PAD).astype(BF16)
    wqr_b = jnp.concatenate([_rot_cols(wq3[:, :, QK_NOPE:]), zq], axis=2).reshape(
        q_lora, n_heads * LANES).astype(BF16)
    wkv3 = w_ukv[0].reshape(kv_lora, n_heads, QK_NOPE + V_DIM)
    wk_b = wkv3[:, :, :QK_NOPE].reshape(kv_lora, n_heads * QK_NOPE).astype(BF16)
    wv_b = wkv3[:, :, QK_NOPE:].reshape(kv_lora, n_heads * V_DIM).astype(BF16)
    front_w = (attn_norm_g, win_b, pool_w[0].astype(BF16), pool_scale, q_norm_g, wq_b, wqr_b, kv_norm_g,
               wk_b, wv_b)
    woa = w_o[0, :pw].astype(BF16)
    wob = w_o[0, pw:].astype(BF16)
    wr_b = jnp.pad(w_router[0], ((0, 0), (0, LANES - n_exp))).astype(BF16)
    br = jnp.pad(b_router, ((0, 0), (0, LANES - n_exp)), constant_values=NEG_BIG)

    pos = jnp.arange(n_meta + s_len, dtype=F32)
    inv_freq = 1.0 / (ROPE_BASE ** (jnp.arange(0, QK_ROPE, 2, dtype=F32) / QK_ROPE))
    ang = pos[:, None] * inv_freq[None, :]
    ones = jnp.ones((n_meta + s_len, LANES - QK_ROPE), F32)
    cs = jnp.concatenate([jnp.cos(ang), jnp.cos(ang), ones], axis=1)
    sn = jnp.concatenate([jnp.sin(ang), jnp.sin(ang), 0.0 * ones], axis=1)

    zero_halo = jnp.zeros((HALO, pw), F32)
    _, _, k_meta, v_meta, p_meta = _front(meta_tokens[None], zero_halo, cs[:n_meta], sn[:n_meta], front_w,
                                          ts=n_meta)
    ts = _tile_rows(s_len, ROW_TILE)
    y_pool, q, k, v, _ = _front(x, p_meta[0, 0], cs[n_meta:], sn[n_meta:], front_w, ts=ts)

    km = jnp.pad(k_meta[0], ((0, LANES - n_meta), (0, 0)))
    vm = jnp.pad(v_meta[0], ((0, LANES - n_meta), (0, 0)))
    y_mla = _attn(q, k, v, km, vm, tq=_tile_rows(s_len, ROW_TILE), n_meta=n_meta)

    tm = _tile_rows(t, ROW_TILE)
    h1, xn4, tokmeta, tokmeta_t, counts = _mid(y_pool.reshape(t, pw), y_mla.reshape(t, -1), x.reshape(t, d),
                                               woa, wob, ffn_norm_g, wr_b, br, tm=tm)

    r_max = EXPERT_ITEM_BLOCKS * EXPERT_BLOCK
    cnt = counts[0, :n_exp].astype(I32)
    padded = (cnt + EXPERT_BLOCK - 1) // EXPERT_BLOCK * EXPERT_BLOCK
    pad_end = jnp.cumsum(padded)
    pad_start = pad_end - padded
    n_assign = t * TOP_K
    p_rows = -(-(n_assign + n_exp * (EXPERT_BLOCK - 1)) // EXPERT_BLOCK) * EXPERT_BLOCK
    idx = tokmeta_t[0:TOP_K].astype(I32)
    rank = tokmeta_t[TOP_K:2 * TOP_K].astype(I32)
    start_of = jnp.zeros_like(idx)
    for e in range(n_exp):
        start_of = jnp.where(idx == e, pad_start[e], start_of)
    dest = (start_of + rank).reshape(-1)

    n_items = n_exp + p_rows // r_max
    per_e = (padded + r_max - 1) // r_max
    item_end = jnp.cumsum(per_e)
    total = item_end[-1]
    wi = jnp.arange(n_items, dtype=I32)
    valid = wi < total
    wc = jnp.minimum(wi, total - 1)
    ie = jnp.minimum(jnp.sum((item_end[None, :] <= wc[:, None]).astype(I32), axis=1), n_exp - 1)
    local = wc - (item_end[ie] - per_e[ie])
    item_row = jnp.where(valid, pad_start[ie] + local * r_max, 0).astype(I32)
    item_n = jnp.where(valid, jnp.clip(padded[ie] - local * r_max, 0, r_max), 0).astype(I32)

    zstart = (pad_start + cnt).astype(I32)
    zlen = (padded - cnt).astype(I32)
    tail = jnp.stack([pad_end[-1] // SUBLANES, (p_rows - pad_end[-1]) // (SUBLANES * ZERO_GROUPS)]).astype(I32)
    xs = _dispatch(zstart, zlen, tail, dest, xn4, p_rows, td=_tile_rows(t, DISPATCH_TILE))
    ys = _experts(ie, item_row, item_n, valid.astype(I32), xs, w1[0], b1[0], w2[0], b2[0],
                  r_max=r_max, tw=min(W1_TILE_COLS, 2 * f_dim))
    out = _combine(dest, tokmeta, h1, final_norm_g.reshape(1, d), ys, tc=_tile_rows(t, COMBINE_TILE))
    return out.reshape(nb, s_len, d)
```

```python
import functools

import jax
import jax.numpy as jnp
from jax import lax
from jax.experimental import pallas as pl
from jax.experimental.pallas import tpu as pltpu

F32 = jnp.float32
BF16 = jnp.bfloat16
I32 = jnp.int32

CHUNK = 64
POOL_WINDOWS = (2, 4, 8, 16)
V_DIM = 128
QK_NOPE = 128
QK_ROPE = 64
ROPE_BASE = 10000.0
TOP_K = 4
SWIGLU_LIMIT = 7.0
SWIGLU_ALPHA = 1.702
EPS = 1e-5

LANES = 128
SUBLANES = 8
MXU_DIM = 256
HEAD_PAD = MXU_DIM
VMEM_LIMIT_BYTES = 58 * 1024 * 1024

ROW_TILE = 2 * MXU_DIM
EXPERT_BLOCK = ROW_TILE
EXPERT_ITEM_BLOCKS = 5
W1_TILE_COLS = 2 * MXU_DIM
DISPATCH_TILE = 8 * MXU_DIM
COMBINE_TILE = 2 * MXU_DIM

HALO = 16
NEG_BIG = -1e30


def _rms(x, g):
    ms = jnp.mean(x * x, axis=-1, keepdims=True)
    return x * lax.rsqrt(ms + EPS) * g


def _dot(a, b):
    return jnp.dot(a, b, preferred_element_type=F32)


def _dot_nt(a, b):
    return lax.dot_general(a, b, (((1,), (1,)), ((), ())), preferred_element_type=F32)


def _const_spec(shape):
    nd = len(shape)
    return pl.BlockSpec(shape, lambda *_: (0,) * nd)


def _row_of(ref, r):
    if isinstance(r, int):
        return ref.at[r // SUBLANES, :, r % SUBLANES, :]
    return ref.at[lax.shift_right_logical(r, 3), :, r & (SUBLANES - 1), :]


def _front_kernel(x_ref, mpool_ref, cs_ref, sn_ref, ag_ref, win_ref, pw_ref, ps_ref, qg_ref, wq_ref,
                  wqr_ref, kg_ref, wk_ref, wv_ref,
                  ypool_ref, q_ref, k_ref, v_ref, ptail_ref, ext_ref, lv_ref, *, ts, pool_w, q_lora, kv_lora,
                  n_heads):
    st = pl.program_id(1)
    hn = _rms(x_ref[0], ag_ref[...]).astype(BF16)
    proj = _dot(hn, win_ref[...])
    pool_in = proj[:, :pool_w]

    lo = SUBLANES
    hi = SUBLANES + HALO + ts

    @pl.when(st == 0)
    def _():
        ext_ref[0:lo, :] = jnp.zeros((lo, pool_w), F32)
        ext_ref[lo:lo + HALO, :] = mpool_ref[...]
        lv_ref[:, 0:lo, :] = jnp.zeros((2, lo, lv_ref.shape[2]), F32)

    ext_ref[lo + HALO:hi, :] = pool_in
    gw = pool_w // len(POOL_WINDOWS)
    for g, w in enumerate(POOL_WINDOWS):
        c0 = g * gw
        u = pool_in[:, c0:c0 + gw]
        cur = ext_ref[lo:hi, c0:c0 + gw]
        src, m = None, 1
        while m < w:
            prev = ext_ref[lo - m:hi - m, c0:c0 + gw] if src is None else lv_ref[src, lo - m:hi - m, :]
            cur = cur + prev
            m *= 2
            if m < w:
                src = 0 if src != 0 else 1
                lv_ref[src, lo:hi, :] = cur
        s = cur[HALO:, :]
        d = (s * (1.0 / w) - u).astype(BF16)
        y = _dot(d, pw_ref[g]) * ps_ref[:, c0:c0 + gw]
        ypool_ref[0, :, c0:c0 + gw] = y.astype(BF16)
    tail = pool_in[ts - HALO:ts, :]
    ext_ref[lo:lo + HALO, :] = tail
    ptail_ref[0, 0] = tail

    o = pool_w
    q_c = proj[:, o:o + q_lora]
    o += q_lora
    kv_c = proj[:, o:o + kv_lora]
    o += kv_lora
    kr = proj[:, o:o + LANES]
    kr_rot = proj[:, o + LANES:o + 2 * LANES]
    cs = cs_ref[...]
    sn = sn_ref[...]
    krope = (kr * cs + kr_rot * sn).astype(BF16)
    qn = _rms(q_c, qg_ref[...]).astype(BF16)
    qm = _dot(qn, wq_ref[...])
    qr = _dot(qn, wqr_ref[...])
    kvn = _rms(kv_c, kg_ref[...]).astype(BF16)
    kn = _dot(kvn, wk_ref[...])
    v_ref[0] = _dot(kvn, wv_ref[...]).astype(BF16)
    for h in range(n_heads):
        a = h * HEAD_PAD
        b = h * LANES
        q_ref[0, :, a:a + LANES] = qm[:, a:a + LANES].astype(BF16)
        q_ref[0, :, a + LANES:a + HEAD_PAD] = (
            qm[:, a + LANES:a + HEAD_PAD] * cs + qr[:, b:b + LANES] * sn).astype(BF16)
        k_ref[0, :, a:a + LANES] = kn[:, b:b + LANES].astype(BF16)
        k_ref[0, :, a + LANES:a + HEAD_PAD] = krope


def _front(x3, mpool, cs, sn, wts, *, ts):
    nb, s_len, d = x3.shape
    (ag, win, pw, ps, qg, wq, wqr, kg, wk, wv) = wts
    pool_w = ps.shape[1]
    q_lora = qg.shape[1]
    kv_lora = kg.shape[1]
    n_heads = wk.shape[1] // LANES
    n_st = s_len // ts
    kern = functools.partial(_front_kernel, ts=ts, pool_w=pool_w, q_lora=q_lora, kv_lora=kv_lora,
                             n_heads=n_heads)
    row = lambda b, s: (b, s, 0)
    in_specs = [
        pl.BlockSpec((1, ts, d), row),
        _const_spec(mpool.shape),
        pl.BlockSpec((ts, LANES), lambda b, s: (s, 0)),
        pl.BlockSpec((ts, LANES), lambda b, s: (s, 0)),
    ] + [_const_spec(w.shape) for w in wts]
    out_shape = (
        jax.ShapeDtypeStruct((nb, s_len, pool_w), BF16),
        jax.ShapeDtypeStruct((nb, s_len, n_heads * HEAD_PAD), BF16),
        jax.ShapeDtypeStruct((nb, s_len, n_heads * HEAD_PAD), BF16),
        jax.ShapeDtypeStruct((nb, s_len, n_heads * V_DIM), BF16),
        jax.ShapeDtypeStruct((nb, n_st, HALO, pool_w), F32),
    )
    out_specs = (
        pl.BlockSpec((1, ts, pool_w), row),
        pl.BlockSpec((1, ts, n_heads * HEAD_PAD), row),
        pl.BlockSpec((1, ts, n_heads * HEAD_PAD), row),
        pl.BlockSpec((1, ts, n_heads * V_DIM), row),
        pl.BlockSpec((1, 1, HALO, pool_w), lambda b, s: (b, s, 0, 0)),
    )
    return pl.pallas_call(
        kern, grid=(nb, n_st), in_specs=in_specs, out_specs=out_specs, out_shape=out_shape,
        scratch_shapes=[pltpu.VMEM((SUBLANES + HALO + ts, pool_w), F32),
                        pltpu.VMEM((2, SUBLANES + HALO + ts, pool_w // len(POOL_WINDOWS)), F32)],
        compiler_params=pltpu.CompilerParams(dimension_semantics=("arbitrary", "arbitrary"),
                                             vmem_limit_bytes=VMEM_LIMIT_BYTES),
        name="front",
    )(x3, mpool, cs, sn, *wts)


def _attn_kernel(q_ref, k_ref, v_ref, km_ref, vm_ref, o_ref, m_ref, l_ref, acc_ref, *, tq, n_meta, n_heads,
                 scale):
    qi = pl.program_id(1)
    c2 = scale * 1.4426950408889634

    def update(h, s, vb, first):
        s_max = jnp.max(s, axis=1, keepdims=True)
        if first:
            m_new = jnp.broadcast_to(s_max, (tq, LANES))
        else:
            m_old = m_ref[h]
            m_new = jnp.maximum(m_old, s_max)
            alpha = jnp.exp2((m_old - m_new) * c2)
        p = jnp.exp2((s - jnp.concatenate([m_new] * (s.shape[1] // LANES), axis=1)) * c2)
        v1 = jnp.concatenate([vb, jnp.ones(vb.shape, BF16)], axis=1)
        pv = _dot(p.astype(BF16), v1)
        if first:
            l_ref[h] = pv[:, V_DIM:]
            acc_ref[h] = pv[:, :V_DIM]
        else:
            l_ref[h] = alpha * l_ref[h] + pv[:, V_DIM:]
            acc_ref[h] = alpha * acc_ref[h] + pv[:, :V_DIM]
        m_ref[h] = m_new

    def q_of(h):
        return q_ref[0, :, h * HEAD_PAD:(h + 1) * HEAD_PAD]

    r_diag = pl.multiple_of(qi * tq, tq)
    rc = lax.broadcasted_iota(I32, (tq, tq), 0) // CHUNK
    cc = lax.broadcasted_iota(I32, (tq, tq), 1) // CHUNK
    vis = jnp.concatenate([cc <= rc, lax.broadcasted_iota(I32, (tq, LANES), 1) < n_meta], axis=1)
    for h in range(n_heads):
        kd = jnp.concatenate([k_ref[0, pl.ds(r_diag, tq), h * HEAD_PAD:(h + 1) * HEAD_PAD],
                              km_ref[:, h * HEAD_PAD:(h + 1) * HEAD_PAD]], axis=0)
        vd = jnp.concatenate([v_ref[0, pl.ds(r_diag, tq), h * V_DIM:(h + 1) * V_DIM],
                              vm_ref[:, h * V_DIM:(h + 1) * V_DIM]], axis=0)
        s = jnp.where(vis, _dot_nt(q_of(h), kd), -jnp.inf)
        update(h, s, vd, True)

    def body(j, c):
        r0 = pl.multiple_of(j * tq, tq)
        for h in range(n_heads):
            s = _dot_nt(q_of(h), k_ref[0, pl.ds(r0, tq), h * HEAD_PAD:(h + 1) * HEAD_PAD])
            update(h, s, v_ref[0, pl.ds(r0, tq), h * V_DIM:(h + 1) * V_DIM], False)
        return c

    lax.fori_loop(0, qi, body, 0)
    for h in range(n_heads):
        o_ref[0, :, h * V_DIM:(h + 1) * V_DIM] = (acc_ref[h] / l_ref[h]).astype(BF16)


def _attn(q, k, v, km, vm, *, tq, n_meta):
    nb, s_len, hw = q.shape
    n_heads = hw // HEAD_PAD
    kern = functools.partial(_attn_kernel, tq=tq, n_meta=n_meta, n_heads=n_heads,
                             scale=float((QK_NOPE + QK_ROPE) ** -0.5))
    return pl.pallas_call(
        kern, grid=(nb, s_len // tq),
        in_specs=[
            pl.BlockSpec((1, tq, hw), lambda b, i: (b, i, 0)),
            pl.BlockSpec((1, s_len, hw), lambda b, i: (b, 0, 0)),
            pl.BlockSpec((1, s_len, n_heads * V_DIM), lambda b, i: (b, 0, 0)),
            _const_spec(km.shape),
            _const_spec(vm.shape),
        ],
        out_specs=pl.BlockSpec((1, tq, n_heads * V_DIM), lambda b, i: (b, i, 0)),
        out_shape=jax.ShapeDtypeStruct((nb, s_len, n_heads * V_DIM), BF16),
        scratch_shapes=[pltpu.VMEM((n_heads, tq, LANES), F32)] * 3,
        compiler_params=pltpu.CompilerParams(dimension_semantics=("arbitrary", "arbitrary"),
                                             vmem_limit_bytes=VMEM_LIMIT_BYTES),
        name="attn",
    )(q, k, v, km, vm)


def _mid_kernel(yp_ref, ym_ref, x_ref, woa_ref, wob_ref, fg_ref, wr_ref, br_ref,
                h1_ref, xn4_ref, tokmeta_ref, tokmeta_t_ref, counts_ref, run_ref, *, tm):
    i = pl.program_id(0)

    @pl.when(i == 0)
    def _():
        run_ref[...] = jnp.zeros_like(run_ref)

    h1 = x_ref[...] + _dot(yp_ref[...], woa_ref[...]) + _dot(ym_ref[...], wob_ref[...])
    h1_ref[...] = h1
    xn = _rms(h1, fg_ref[...])
    for c in range(xn4_ref.shape[1]):
        xn4_ref[:, c] = xn[:, c * LANES:(c + 1) * LANES].reshape(tm // SUBLANES, SUBLANES, LANES)
    xb = xn.astype(BF16)

    kh = xb.shape[1] // 2
    logits = _dot(xb[:, :kh], wr_ref[0:kh, :]) + _dot(xb[:, kh:], wr_ref[kh:2 * kh, :]) + br_ref[...]
    lane = lax.broadcasted_iota(I32, logits.shape, 1).astype(F32)
    work = logits
    idxs, vals = [], []
    for _ in range(TOP_K):
        mx = jnp.max(work, axis=1, keepdims=True)
        ix = jnp.min(jnp.where(work == mx, lane, float(LANES)), axis=1, keepdims=True)
        idxs.append(ix)
        vals.append(mx)
        work = jnp.where(lane == ix, -jnp.inf, work)
    es = [jnp.exp(vv - vals[0]) for vv in vals]
    den = es[0]
    for e in es[1:]:
        den = den + e
    hot = [jnp.where(lane == ix, 1.0, 0.0) for ix in idxs]
    cnt = hot[0]
    for hh in hot[1:]:
        cnt = cnt + hh
    rr = lax.broadcasted_iota(I32, (tm, tm), 0)
    cc = lax.broadcasted_iota(I32, (tm, tm), 1)
    ltri = jnp.where(rr > cc, 1.0, 0.0).astype(BF16)
    base = run_ref[0:1, :] + _dot(ltri, cnt.astype(BF16))
    out = jnp.zeros(logits.shape, F32)
    for k in range(TOP_K):
        rank = jnp.sum(hot[k] * base, axis=1, keepdims=True)
        out = jnp.where(lane == float(k), idxs[k], out)
        out = jnp.where(lane == float(TOP_K + k), rank, out)
        out = jnp.where(lane == float(2 * TOP_K + k), es[k] / den, out)
    tokmeta_ref[...] = out
    tokmeta_t_ref[...] = jnp.transpose(out)[0:tokmeta_t_ref.shape[0], :]
    run = run_ref[...] + jnp.sum(cnt, axis=0, keepdims=True)
    run_ref[...] = run
    counts_ref[...] = run


def _mid(yp, ym, x2, woa, wob, fg, wr, br, *, tm):
    t, d = x2.shape
    pw = yp.shape[1]
    mw = ym.shape[1]
    nt = d // LANES
    kern = functools.partial(_mid_kernel, tm=tm)
    row = lambda i: (i, 0)
    return pl.pallas_call(
        kern, grid=(t // tm,),
        in_specs=[
            pl.BlockSpec((tm, pw), row), pl.BlockSpec((tm, mw), row), pl.BlockSpec((tm, d), row),
            _const_spec(woa.shape), _const_spec(wob.shape), _const_spec(fg.shape),
            _const_spec(wr.shape), _const_spec(br.shape),
        ],
        out_specs=(
            pl.BlockSpec((tm, d), row), pl.BlockSpec((tm // SUBLANES, nt, SUBLANES, LANES), lambda i: (i, 0, 0, 0)),
            pl.BlockSpec((tm, LANES), row), pl.BlockSpec((2 * SUBLANES, tm), lambda i: (0, i)),
            pl.BlockSpec((8, LANES), lambda i: (0, 0)),
        ),
        out_shape=(
            jax.ShapeDtypeStruct((t, d), F32), jax.ShapeDtypeStruct((t // SUBLANES, nt, SUBLANES, LANES), F32),
            jax.ShapeDtypeStruct((t, LANES), F32), jax.ShapeDtypeStruct((2 * SUBLANES, t), F32),
            jax.ShapeDtypeStruct((8, LANES), F32),
        ),
        scratch_shapes=[pltpu.VMEM((8, LANES), F32)],
        compiler_params=pltpu.CompilerParams(dimension_semantics=("arbitrary",),
                                             vmem_limit_bytes=VMEM_LIMIT_BYTES),
        name="mid",
    )(yp, ym, x2, woa, wob, fg, wr, br)


ZERO_GROUPS = EXPERT_BLOCK // (2 * SUBLANES)


def _dispatch_kernel(zstart_ref, zlen_ref, tail_ref, *refs, td, n_exp):
    dest_refs = refs[:TOP_K]
    xn4_ref, xs_ref, zbuf, sem, sem_z = refs[TOP_K:]

    def body(r, c):
        for k in range(TOP_K):
            pltpu.make_async_copy(_row_of(xn4_ref, r), _row_of(xs_ref, dest_refs[k][r]), sem).start(
                priority=k % 2)
        return c

    lax.fori_loop(0, td, body, 0, unroll=8)
    for _ in range(TOP_K):
        pltpu.make_async_copy(xn4_ref, xn4_ref, sem).wait()

    @pl.when(pl.program_id(0) == pl.num_programs(0) - 1)
    def _zero_fill():
        zbuf[...] = jnp.zeros_like(zbuf)
        tail0 = tail_ref[0]
        n_tail = tail_ref[1]

        def pad_copies(e, wait):
            zs = zstart_ref[e]
            zl = zlen_ref[e]
            head = jnp.minimum((-zs) & (SUBLANES - 1), zl)
            for h in range(SUBLANES - 1):
                @pl.when(h < head)
                def _(h=h):
                    cp = pltpu.make_async_copy(_row_of(zbuf, 0), _row_of(xs_ref, zs + h), sem_z)
                    cp.wait() if wait else cp.start()
            g0 = lax.shift_right_logical(zs + head, 3)
            ng = lax.shift_right_logical(zl - head, 3)
            v = ZERO_GROUPS
            while v >= 1:
                @pl.when((ng & v) != 0)
                def _(v=v):
                    off = g0 + (ng & (-2 * v))
                    cp = pltpu.make_async_copy(zbuf.at[pl.ds(0, v)], xs_ref.at[pl.ds(off, v)], sem_z)
                    cp.wait() if wait else cp.start()
                v //= 2

        def tail_copy(i, wait):
            cp = pltpu.make_async_copy(zbuf, xs_ref.at[pl.ds(tail0 + i * ZERO_GROUPS, ZERO_GROUPS)], sem_z)
            cp.wait() if wait else cp.start()

        for wait in (False, True):
            lax.fori_loop(0, n_exp, lambda e, c, wait=wait: (pad_copies(e, wait), c)[1], 0)
            lax.fori_loop(0, n_tail, lambda i, c, wait=wait: (tail_copy(i, wait), c)[1], 0)


def _dispatch(zstart, zlen, tail, dest_flat, xn4, p_rows, *, td):
    tg, nt, _, _ = xn4.shape
    n_exp = zstart.shape[0]
    kern = functools.partial(_dispatch_kernel, td=td, n_exp=n_exp)
    nb = tg * SUBLANES // td
    grid_spec = pltpu.PrefetchScalarGridSpec(
        num_scalar_prefetch=3, grid=(nb,),
        in_specs=[pl.BlockSpec((td,), lambda i, *_, k=k: (k * nb + i,), memory_space=pltpu.SMEM)
                  for k in range(TOP_K)] + [
            pl.BlockSpec((td // SUBLANES, nt, SUBLANES, LANES), lambda i, *_: (i, 0, 0, 0)),
        ],
        out_specs=pl.BlockSpec(memory_space=pl.ANY),
        scratch_shapes=[pltpu.VMEM((ZERO_GROUPS, nt, SUBLANES, LANES), F32), pltpu.SemaphoreType.DMA,
                        pltpu.SemaphoreType.DMA],
    )
    return pl.pallas_call(
        kern, grid_spec=grid_spec,
        out_shape=jax.ShapeDtypeStruct((p_rows // SUBLANES, nt, SUBLANES, LANES), F32),
        compiler_params=pltpu.CompilerParams(dimension_semantics=("arbitrary",),
                                             vmem_limit_bytes=VMEM_LIMIT_BYTES),
        name="dispatch",
    )(zstart, zlen, tail, *([dest_flat] * TOP_K), xn4)


def _expert_kernel(item_e_ref, item_row_ref, item_n_ref, item_valid_ref,
                   xs_ref, w1_ref, b1_ref, w2_ref, b2_ref, ys_ref,
                   xbuf, abuf, stage, xpre, wperm, w2b, hcbuf, sem_s, sem_p, *, rb, j1, tw, nt):
    del item_e_ref, item_valid_ref, xs_ref
    w = pl.program_id(0)
    j = pl.program_id(1)
    nrows = item_n_ref[w]
    row0 = pl.multiple_of(item_row_ref[w], rb)
    nrb = nrows // rb
    d = w2_ref.shape[1]
    tf = w2_ref.shape[0]

    def rows(i):
        return pl.ds(pl.multiple_of(i * rb, rb), rb)

    g0 = lax.shift_right_logical(row0, 3)
    gb = rb // SUBLANES

    def stage_copy(i, slot, to_hbm):
        hbm = ys_ref.at[pl.ds(g0 + i * gb, gb)]
        vmem = stage.at[slot]
        return pltpu.make_async_copy(vmem, hbm, sem_s.at[slot]) if to_hbm else pltpu.make_async_copy(
            hbm, vmem, sem_s.at[slot])

    def first_block_copy(item):
        g = lax.shift_right_logical(pl.multiple_of(item_row_ref[item], rb), 3)
        return pltpu.make_async_copy(ys_ref.at[pl.ds(g, gb)], xpre, sem_p)

    def slot_of(jj, i):
        return (jj * nrb + i) & 1

    def dot1(i, x):
        hcbuf[slot_of(j, i)] = _dot(x, w1_ref[...].astype(BF16)) + b1_ref[...]

    lane = lax.broadcasted_iota(I32, (rb, LANES), 1)
    even = (lane & 1) == 0

    def tail1(jj, i):
        hc = hcbuf[slot_of(jj, i)]
        outs = []
        for q in range(tw // (2 * LANES)):
            c0 = hc[:, 2 * q * LANES:(2 * q + 1) * LANES]
            c1 = hc[:, (2 * q + 1) * LANES:(2 * q + 2) * LANES]
            glu = jnp.where(even, c0, pltpu.roll(c1, 1, 1))
            lin = jnp.where(even, pltpu.roll(c0, LANES - 1, 1), c1)
            glu = jnp.minimum(glu, SWIGLU_LIMIT)
            lin = jnp.clip(lin, -SWIGLU_LIMIT, SWIGLU_LIMIT)
            act = glu * (1.0 / (1.0 + jnp.exp(-SWIGLU_ALPHA * glu))) * (lin + 1.0)
            outs.append(act.astype(BF16))
        abuf[jj, rows(i), :] = jnp.concatenate(outs, axis=1)

    def permute_w2_tile():
        for s in range(d // LANES):
            for g in range(tf // LANES):
                top = w2_ref[g * LANES:g * LANES + LANES // 2, s * LANES:(s + 1) * LANES]
                bot = w2_ref[g * LANES + LANES // 2:(g + 1) * LANES, s * LANES:(s + 1) * LANES]
                wperm.at[s][pl.ds(g * LANES, LANES // 2, stride=2), :] = top
                wperm.at[s][pl.ds(g * LANES + 1, LANES // 2, stride=2), :] = bot
        k0 = pl.multiple_of(j * tf, tf)
        for s in range(d // LANES):
            w2b[pl.ds(k0, tf), s * LANES:(s + 1) * LANES] = wperm[s].astype(BF16)

    @pl.when((j == 0) & (nrows > 0))
    def _first_step():
        def fetch(i):
            stage_copy(i, i & 1, False).wait()

            @pl.when(i + 1 < nrb)
            def _():
                stage_copy(i + 1, (i + 1) & 1, False).start()

        def load_and_dot(i, src=None):
            blk = stage.at[i & 1] if src is None else src
            x = jnp.concatenate([blk[:, c].reshape(rb, LANES) for c in range(nt)], axis=1).astype(BF16)
            xbuf[rows(i), :] = x
            dot1(i, x)

        @pl.when(w == 0)
        def _():
            stage_copy(0, 0, False).start()
            fetch(0)
            load_and_dot(0)
            permute_w2_tile()

        @pl.when(w > 0)
        def _():
            first_block_copy(w).wait()

            @pl.when(1 < nrb)
            def _():
                stage_copy(1, 1, False).start()

            load_and_dot(0, xpre)
            permute_w2_tile()

        def body(i, c):
            fetch(i)
            tail1(0, i - 1)
            load_and_dot(i)
            return c

        lax.fori_loop(1, nrb, body, 0)

    @pl.when((j > 0) & (j < j1) & (nrows > 0))
    def _next_steps():
        tail1(j - 1, nrb - 1)
        dot1(0, xbuf[rows(0), :])
        permute_w2_tile()

        def body(i, c):
            tail1(j, i - 1)
            dot1(i, xbuf[rows(i), :])
            return c

        lax.fori_loop(1, nrb, body, 0)

    @pl.when((j == j1) & (nrows > 0))
    def _last_step():
        nxt = jnp.minimum(w + 1, pl.num_programs(0) - 1)

        @pl.when((w + 1 < pl.num_programs(0)) & (item_n_ref[nxt] > 0))
        def _():
            first_block_copy(nxt).start()

        tail1(j1 - 1, nrb - 1)

        def body(i, c):
            @pl.when(i >= 2)
            def _():
                stage_copy(i - 2, i & 1, True).wait()

            @pl.when(i >= 1)
            def _():
                stage_copy(i - 1, (i - 1) & 1, True).start()

            a = jnp.concatenate([abuf[jj, rows(i), :] for jj in range(j1)], axis=1)
            y = _dot(a, w2b[...]) + b2_ref[...]
            for c in range(nt):
                stage[i & 1, :, c] = y[:, c * LANES:(c + 1) * LANES].reshape(gb, SUBLANES, LANES)
            return c

        lax.fori_loop(0, nrb, body, 0)

        @pl.when(nrb >= 2)
        def _():
            stage_copy(nrb - 2, nrb & 1, True).wait()

        last = stage_copy(nrb - 1, (nrb - 1) & 1, True)
        last.start()
        last.wait()


def _experts(item_e, item_row, item_n, item_valid, xs, w1, b1, w2, b2, *, r_max, tw):
    n_exp, d, f2 = w1.shape
    f_dim = w2.shape[1]
    nt = xs.shape[1]
    j1 = f2 // tw
    tf = f_dim // j1
    assert tf == tw // 2 and tf % LANES == 0 and nt * LANES == d
    rb = EXPERT_BLOCK
    kern = functools.partial(_expert_kernel, rb=rb, j1=j1, tw=tw, nt=nt)

    def w_step(w, j, iv):
        return jnp.minimum(jnp.where(iv[w] == 1, j, j1), j1 - 1)

    def w1_map(w, j, ie, ir, inn, iv):
        return (ie[w], 0, w_step(w, j, iv))

    def w2_map(w, j, ie, ir, inn, iv):
        return (ie[w], w_step(w, j, iv), 0)

    def e_map(w, j, ie, ir, inn, iv):
        return (ie[w], 0, 0)

    grid_spec = pltpu.PrefetchScalarGridSpec(
        num_scalar_prefetch=4, grid=(item_e.shape[0], j1 + 1),
        in_specs=[
            pl.BlockSpec(memory_space=pl.ANY),
            pl.BlockSpec((None, d, tw), w1_map),
            pl.BlockSpec((None, 1, tw), w1_map),
            pl.BlockSpec((None, tf, d), w2_map),
            pl.BlockSpec((None, 1, d), e_map),
        ],
        out_specs=pl.BlockSpec(memory_space=pl.ANY),
        scratch_shapes=[
            pltpu.VMEM((r_max, d), BF16),
            pltpu.VMEM((j1, r_max, tf), BF16),
            pltpu.VMEM((2, rb // SUBLANES, nt, SUBLANES, LANES), F32),
            pltpu.VMEM((rb // SUBLANES, nt, SUBLANES, LANES), F32),
            pltpu.VMEM((d // LANES, tf, LANES), F32),
            pltpu.VMEM((f_dim, d), BF16),
            pltpu.VMEM((2, rb, tw), F32),
            pltpu.SemaphoreType.DMA((2,)),
            pltpu.SemaphoreType.DMA,
        ],
    )
    return pl.pallas_call(
        kern, grid_spec=grid_spec,
        out_shape=jax.ShapeDtypeStruct(xs.shape, F32),
        input_output_aliases={4: 0},
        compiler_params=pltpu.CompilerParams(dimension_semantics=("arbitrary", "arbitrary"),
                                             vmem_limit_bytes=VMEM_LIMIT_BYTES),
        name="experts",
    )(item_e, item_row, item_n, item_valid, xs, w1, b1.reshape(n_exp, 1, f2), w2, b2.reshape(n_exp, 1, d))


def _combine_kernel(*refs, tc, nt):
    dest_refs = refs[:TOP_K]
    tokmeta_ref, h1_ref, fg_ref, ys_ref, o_ref, buf, ssq_ref, sem = refs[TOP_K:]
    i = pl.program_id(0)
    last = pl.num_programs(0) - 1
    slot = i & 1
    d = nt * LANES

    per_row = nt // SUBLANES

    def loop(do_sum, do_issue):
        def body(g, c):
            r8 = pl.ds(pl.multiple_of(g * SUBLANES, SUBLANES), SUBLANES)
            if do_sum:
                tm = tokmeta_ref[r8, :]
                gates = [tm[:, 2 * TOP_K + k:2 * TOP_K + k + 1] for k in range(TOP_K)]
                ssq = jnp.zeros((SUBLANES, LANES), F32)
            for s in range(SUBLANES):
                if do_issue:
                    for k in range(TOP_K):
                        src = _row_of(ys_ref, dest_refs[k][g * SUBLANES + s])
                        pltpu.make_async_copy(src, buf.at[slot, k, g, :, s, :], sem.at[slot]).start(
                            priority=k % 2)
                if do_sum:
                    for cc in range(s * per_row, (s + 1) * per_row):
                        a = h1_ref[r8, cc * LANES:(cc + 1) * LANES]
                        for k in range(TOP_K):
                            a = a + gates[k] * buf[1 - slot, k, g, cc]
                        o_ref[r8, cc * LANES:(cc + 1) * LANES] = a
                        ssq = ssq + a * a
            if do_sum:
                ssq_ref[r8, :] = ssq
            return c

        lax.fori_loop(0, tc // SUBLANES, body, 0)
        if do_sum:
            inv = lax.rsqrt(jnp.sum(ssq_ref[...], axis=1, keepdims=True) * (1.0 / d) + EPS)
            o_ref[...] = o_ref[...] * inv * fg_ref[...]

    @pl.when(i > 0)
    def _():
        for k in range(TOP_K):
            pltpu.make_async_copy(buf.at[1 - slot, k], buf.at[1 - slot, k], sem.at[1 - slot]).wait()

    @pl.when(i == 0)
    def _():
        loop(False, True)

    @pl.when((i > 0) & (i < last))
    def _():
        loop(True, True)

    @pl.when(i == last)
    def _():
        loop(True, False)


def _combine(dest_flat, tokmeta, h1, fg, ys, *, tc):
    t, d = h1.shape
    nt = ys.shape[1]
    nb = t // tc
    kern = functools.partial(_combine_kernel, tc=tc, nt=nt)
    prev = lambda i: (jnp.maximum(i - 1, 0), 0)
    return pl.pallas_call(
        kern, grid=(nb + 1,),
        in_specs=[pl.BlockSpec((tc,), lambda i, k=k: (k * nb + jnp.minimum(i, nb - 1),), memory_space=pltpu.SMEM)
                  for k in range(TOP_K)] + [
            pl.BlockSpec((tc, LANES), prev), pl.BlockSpec((tc, d), prev), _const_spec(fg.shape),
            pl.BlockSpec(memory_space=pl.ANY),
        ],
        out_specs=pl.BlockSpec((tc, d), prev),
        out_shape=jax.ShapeDtypeStruct((t, d), F32),
        scratch_shapes=[pltpu.VMEM((2, TOP_K, tc // SUBLANES, nt, SUBLANES, LANES), F32),
                        pltpu.VMEM((tc, LANES), F32), pltpu.SemaphoreType.DMA((2,))],
        compiler_params=pltpu.CompilerParams(dimension_semantics=("arbitrary",),
                                             vmem_limit_bytes=VMEM_LIMIT_BYTES),
        name="combine",
    )(*([dest_flat] * TOP_K), tokmeta, h1, fg, ys)


def _rot_cols(w):
    h = QK_ROPE // 2
    return jnp.concatenate([-w[..., h:], w[..., :h]], axis=-1)


def _tile_rows(n, cap):
    t = min(n, cap)
    assert n % t == 0, (n, cap)
    return t


def kernel(x, meta_tokens, attn_norm_g, w_in, q_norm_g, w_uq, kv_norm_g, w_ukv, pool_w, pool_scale, w_o,
           ffn_norm_g, w_router, b_router, w1, b1, w2, b2, final_norm_g):
    nb, s_len, d = x.shape
    n_meta = meta_tokens.shape[0]
    assert w_in.shape[0] == 1, "one layer"
    assert n_meta == HALO and max(POOL_WINDOWS) - 1 <= HALO
    pw = pool_scale.shape[1]
    q_lora = q_norm_g.shape[1]
    kv_lora = kv_norm_g.shape[1]
    n_heads = w_uq.shape[2] // (QK_NOPE + QK_ROPE)
    n_exp = w_router.shape[2]
    f_dim = w2.shape[2]
    t = nb * s_len
    assert s_len % CHUNK == 0 and n_exp <= LANES and pw // len(POOL_WINDOWS) % LANES == 0

    win = w_in[0]
    o = pw + q_lora + kv_lora
    w_kr = win[:, o:o + QK_ROPE]
    zc = jnp.zeros((d, LANES - QK_ROPE), F32)
    win_b = jnp.concatenate([win[:, :o], w_kr, zc, _rot_cols(w_kr), zc], axis=1).astype(BF16)
    wq3 = w_uq[0].reshape(q_lora, n_heads, QK_NOPE + QK_ROPE)
    zq = jnp.zeros((q_lora, n_heads, LANES - QK_ROPE), F32)
    wq_b = jnp.concatenate([wq3, zq], axis=2).reshape(q_lora, n_heads * HEAD_PAD).astype(BF16)
    wqr_b = jnp.concatenate([_rot_cols(wq3[:, :, QK_NOPE:]), zq], axis=2).reshape(
        q_lora, n_heads * LANES).astype(BF16)
    wkv3 = w_ukv[0].reshape(kv_lora, n_heads, QK_NOPE + V_DIM)
    wk_b = wkv3[:, :, :QK_NOPE].reshape(kv_lora, n_heads * QK_NOPE).astype(BF16)
    wv_b = wkv3[:, :, QK_NOPE:].reshape(kv_lora, n_heads * V_DIM).astype(BF16)
    front_w = (attn_norm_g, win_b, pool_w[0].astype(BF16), pool_scale, q_norm_g, wq_b, wqr_b, kv_norm_g,
               wk_b, wv_b)
    woa = w_o[0, :pw].astype(BF16)
    wob = w_o[0, pw:].astype(BF16)
    wr_b = jnp.pad(w_router[0], ((0, 0), (0, LANES - n_exp))).astype(BF16)
    br = jnp.pad(b_router, ((0, 0), (0, LANES - n_exp)), constant_values=NEG_BIG)

    pos = jnp.arange(n_meta + s_len, dtype=F32)
    inv_freq = 1.0 / (ROPE_BASE ** (jnp.arange(0, QK_ROPE, 2, dtype=F32) / QK_ROPE))
    ang = pos[:, None] * inv_freq[None, :]
    ones = jnp.ones((n_meta + s_len, LANES - QK_ROPE), F32)
    cs = jnp.concatenate([jnp.cos(ang), jnp.cos(ang), ones], axis=1)
    sn = jnp.concatenate([jnp.sin(ang), jnp.sin(ang), 0.0 * ones], axis=1)

    zero_halo = jnp.zeros((HALO, pw), F32)
    _, _, k_meta, v_meta, p_meta = _front(meta_tokens[None], zero_halo, cs[:n_meta], sn[:n_meta], front_w,
                                          ts=n_meta)
    ts = _tile_rows(s_len, ROW_TILE)
    y_pool, q, k, v, _ = _front(x, p_meta[0, 0], cs[n_meta:], sn[n_meta:], front_w, ts=ts)

    km = jnp.pad(k_meta[0], ((0, LANES - n_meta), (0, 0)))
    vm = jnp.pad(v_meta[0], ((0, LANES - n_meta), (0, 0)))
    y_mla = _attn(q, k, v, km, vm, tq=_tile_rows(s_len, ROW_TILE), n_meta=n_meta)

    tm = _tile_rows(t, ROW_TILE)
    h1, xn4, tokmeta, tokmeta_t, counts = _mid(y_pool.reshape(t, pw), y_mla.reshape(t, -1), x.reshape(t, d),
                                               woa, wob, ffn_norm_g, wr_b, br, tm=tm)

    r_max = EXPERT_ITEM_BLOCKS * EXPERT_BLOCK
    cnt = counts[0, :n_exp].astype(I32)
    padded = (cnt + EXPERT_BLOCK - 1) // EXPERT_BLOCK * EXPERT_BLOCK
    pad_end = jnp.cumsum(padded)
    pad_start = pad_end - padded
    n_assign = t * TOP_K
    p_rows = -(-(n_assign + n_exp * (EXPERT_BLOCK - 1)) // EXPERT_BLOCK) * EXPERT_BLOCK
    idx = tokmeta_t[0:TOP_K].astype(I32)
    rank = tokmeta_t[TOP_K:2 * TOP_K].astype(I32)
    start_of = jnp.zeros_like(idx)
    for e in range(n_exp):
        start_of = jnp.where(idx == e, pad_start[e], start_of)
    dest = (start_of + rank).reshape(-1)

    n_items = n_exp + p_rows // r_max
    per_e = (padded + r_max - 1) // r_max
    item_end = jnp.cumsum(per_e)
    total = item_end[-1]
    wi = jnp.arange(n_items, dtype=I32)
    valid = wi < total
    wc = jnp.minimum(wi, total - 1)
    ie = jnp.minimum(jnp.sum((item_end[None, :] <= wc[:, None]).astype(I32), axis=1), n_exp - 1)
    local = wc - (item_end[ie] - per_e[ie])
    item_row = jnp.where(valid, pad_start[ie] + local * r_max, 0).astype(I32)
    item_n = jnp.where(valid, jnp.clip(padded[ie] - local * r_max, 0, r_max), 0).astype(I32)

    zstart = (pad_start + cnt).astype(I32)
    zlen = (padded - cnt).astype(I32)
    tail = jnp.stack([pad_end[-1] // SUBLANES, (p_rows - pad_end[-1]) // (SUBLANES * ZERO_GROUPS)]).astype(I32)
    xs = _dispatch(zstart, zlen, tail, dest, xn4, p_rows, td=_tile_rows(t, DISPATCH_TILE))
    ys = _experts(ie, item_row, item_n, valid.astype(I32), xs, w1[0], b1[0], w2[0], b2[0],
                  r_max=r_max, tw=min(W1_TILE_COLS, 2 * f_dim))
    out = _combine(dest, tokmeta, h1, final_norm_g.reshape(1, d), ys, tc=_tile_rows(t, COMBINE_TILE))
    return out.reshape(nb, s_len, d)
```
